```python
import math
import jax, jax.numpy as jnp
from jax import lax
import numpy as np

D_MODEL = 1024
BATCH = 8
SEQ = 4096
DEPTH = 1
DEC_BATCH = 32
DEC_SEQ = 1
PAST_LEN = 16384
PAGE_SIZE = 128

D_MIX = D_MODEL
W_A = D_MIX // 2
W_B = D_MIX - W_A
DH_A = 64
E_A = 2 * DH_A
H_A = W_A // E_A
G_B = 4
C_B = W_B // G_B
CHUNK = 128
H_M = 4
DH_M = D_MODEL // H_M
N_MEM = 256
D_FF = ((8 * D_MODEL // 3 + 127) // 128) * 128
NB = 32
MAX_DIST = 128
BLOCK_Q = 128
MIX_IN = 3 * W_A + 2 * W_B
LN_EPS = 1e-5
NEG_INF = -1e30
ALPHA = (2 * DEPTH) ** 0.25
BETA = (8 * DEPTH) ** -0.25

kernel_name = 'hymba_diffattn_sgu_macaron_deepnorm_step'


def layer_norm(x, g, b):
    xf = x.astype(jnp.float32)
    mu = jnp.mean(xf, -1, keepdims=True)
    var = jnp.mean(jnp.square(xf - mu), -1, keepdims=True)
    y = (xf - mu) * lax.rsqrt(var + LN_EPS) * g.astype(jnp.float32) + b.astype(jnp.float32)
    return y.astype(x.dtype)


def swiglu(x, w_in, w_out):
    a, b = jnp.split(x @ w_in, 2, axis=-1)
    return (jax.nn.silu(a) * b) @ w_out


def rel_bucket(q_pos, k_pos):
    n = jnp.maximum(q_pos[:, None] - k_pos[None, :], 0)
    max_exact = NB // 2
    nf = jnp.maximum(n, 1).astype(jnp.float32)
    large = max_exact + (jnp.log(nf / max_exact) / math.log(MAX_DIST / max_exact)
                         * (NB - max_exact)).astype(jnp.int32)
    large = jnp.minimum(large, NB - 1)
    return jnp.where(n < max_exact, n, large)


def rel_bias_for(table, q_pos, k_pos):
    return jnp.transpose(table.astype(jnp.float32)[rel_bucket(q_pos, k_pos)], (2, 0, 1))


def diff_core(q, k, v, bias, mask, lam):
    s = jnp.einsum('bqhcd,bkhcd->bhcqk', q, k).astype(jnp.float32) * (DH_A ** -0.5) + bias[None, :, None]
    s = jnp.where(mask, s, NEG_INF)
    p = jax.nn.softmax(s, axis=-1)
    w = p[:, :, 0] - lam * p[:, :, 1]
    return jnp.einsum('bhqk,bkhe->bqhe', w.astype(v.dtype), v)


def diff_lambda(lq1, lk1, lq2, lk2, lambda_init):
    f = jnp.float32
    return (jnp.exp(jnp.sum(lq1.astype(f) * lk1.astype(f)))
            - jnp.exp(jnp.sum(lq2.astype(f) * lk2.astype(f))) + lambda_init)


def diff_head_out(o, g, lambda_init):
    of = o.astype(jnp.float32)
    y = of * lax.rsqrt(jnp.mean(of * of, -1, keepdims=True) + LN_EPS) * g.astype(jnp.float32) * (1.0 - lambda_init)
    B, T = o.shape[:2]
    return y.astype(o.dtype).reshape(B, T, W_A)


def prompt_attend(table):
    def attend(q, k, v, lam):
        B, S = q.shape[:2]
        nb = S // BLOCK_Q
        qb = jnp.swapaxes(q.reshape(B, nb, BLOCK_Q, H_A, 2, DH_A), 0, 1)
        k_pos = jnp.arange(S)

        def one(args):
            q_blk, i = args
            q_pos = i * BLOCK_Q + jnp.arange(BLOCK_Q)
            mask = q_pos[:, None] >= k_pos[None, :]
            return diff_core(q_blk, k, v, rel_bias_for(table, q_pos, k_pos), mask, lam)

        o = lax.map(one, (qb, jnp.arange(nb)))
        return jnp.swapaxes(o, 0, 1).reshape(B, S, H_A, E_A)
    return attend


def sample_attend(ck, cv, page_table, table):
    def attend(q, k, v, lam):
        Bd, Tn = q.shape[:2]
        k_past = ck[page_table].reshape(Bd, -1, H_A, 2, DH_A)
        v_past = cv[page_table].reshape(Bd, -1, H_A, E_A)
        past = k_past.shape[1]
        k_all = jnp.concatenate([k_past, k], axis=1)
        v_all = jnp.concatenate([v_past, v], axis=1)
        q_pos = past + jnp.arange(Tn)
        k_pos = jnp.arange(past + Tn)
        mask = k_pos[None, :] <= q_pos[:, None]
        return diff_core(q, k_all, v_all, rel_bias_for(table, q_pos, k_pos), mask, lam)
    return attend


def causal_sgu(u, v, ln_g, ln_b, w, b):
    u = jax.nn.gelu(u, approximate=False)
    v = layer_norm(jax.nn.gelu(v, approximate=False), ln_g, ln_b)
    B, T = v.shape[:2]
    Tp = -(-T // CHUNK) * CHUNK
    vp = jnp.pad(v, ((0, 0), (0, Tp - T), (0, 0))).reshape(B, Tp // CHUNK, CHUNK, G_B, C_B)
    mixed = (jnp.einsum('gts,bnsgc->bntgc', jnp.tril(w), vp)
             + jnp.swapaxes(b, 0, 1)[None, None, :, :, None])
    return u * mixed.reshape(B, Tp, W_B)[:, :T], v


def mem_kv(mem, w):
    B, N = mem.shape[:2]
    kv = (mem @ w).reshape(B, N, 2, H_M, DH_M)
    return kv[:, :, 0], kv[:, :, 1]


def cross_attn(x, mk, mv, wq, wo):
    B, T = x.shape[:2]
    q = (x @ wq).reshape(B, T, H_M, DH_M)
    s = jnp.einsum('bqhd,bmhd->bhqm', q, mk).astype(jnp.float32) * (DH_M ** -0.5)
    p = jax.nn.softmax(s, axis=-1)
    o = jnp.einsum('bhqm,bmhd->bqhd', p.astype(mv.dtype), mv).reshape(B, T, D_MODEL)
    return o @ wo


def trunk_layer(x, mk, mv, attend, lam, lambda_init, ln_g, ln_b, f1_in, f1_out, w_mix_in, w_mix_out,
                subln_g, sgu_ln_g, sgu_ln_b, sgu_w, sgu_b, xq_w, xo_w, f2_in, f2_out):
    x = layer_norm(ALPHA * x + 0.5 * swiglu(x, f1_in, f1_out), ln_g[0], ln_b[0])
    h = x @ w_mix_in
    B, T = h.shape[:2]
    q = h[..., :W_A].reshape(B, T, H_A, 2, DH_A)
    k = h[..., W_A:2 * W_A].reshape(B, T, H_A, 2, DH_A)
    v = h[..., 2 * W_A:3 * W_A].reshape(B, T, H_A, E_A)
    u = h[..., 3 * W_A:3 * W_A + W_B]
    vb = h[..., 3 * W_A + W_B:]
    oa = diff_head_out(attend(q, k, v, lam), subln_g, lambda_init)
    ob, vn = causal_sgu(u, vb, sgu_ln_g, sgu_ln_b, sgu_w, sgu_b)
    x = layer_norm(ALPHA * x + jnp.concatenate([oa, ob], axis=-1) @ w_mix_out, ln_g[1], ln_b[1])
    x = layer_norm(ALPHA * x + cross_attn(x, mk, mv, xq_w, xo_w), ln_g[2], ln_b[2])
    x = layer_norm(ALPHA * x + 0.5 * swiglu(x, f2_in, f2_out), ln_g[3], ln_b[3])
    return x, k.reshape(B, T, H_A, E_A), v, vn


def setup_inputs(seed: int = 0) -> dict:
    key = jax.random.key(seed)
    ks = jax.random.split(key, 32)
    f32 = jnp.float32
    n_pages = PAST_LEN // PAGE_SIZE
    n_used = DEC_BATCH * n_pages
    n_phys = (5 * n_used + 3) // 4
    sd = D_MODEL ** -0.5

    def nrm(k, shape, scale=1.0):
        return jax.random.normal(k, shape, f32) * scale

    page_table = jax.random.permutation(ks[7], n_phys)[:n_used].reshape(DEC_BATCH, n_pages).astype(jnp.int32)
    return {
        'x_prompt': nrm(ks[0], (BATCH, SEQ, D_MODEL)),
        'x_sample': nrm(ks[1], (DEC_BATCH, DEC_SEQ, D_MODEL)),
        'mem_prompt': nrm(ks[2], (BATCH, N_MEM, D_MODEL)),
        'cache_k': nrm(ks[3], (DEPTH, n_phys, PAGE_SIZE, H_A, E_A)),
        'cache_v': nrm(ks[4], (DEPTH, n_phys, PAGE_SIZE, H_A, E_A)),
        'cache_mem_k': nrm(ks[5], (DEPTH, DEC_BATCH, N_MEM, H_M, DH_M)),
        'cache_mem_v': nrm(ks[6], (DEPTH, DEC_BATCH, N_MEM, H_M, DH_M)),
        'page_table': page_table,
        'rel_bias': nrm(ks[8], (NB, H_A), 0.5),
        'ln_g': 1.0 + nrm(ks[9], (DEPTH, 4, D_MODEL), 0.02),
        'ln_b': nrm(ks[10], (DEPTH, 4, D_MODEL), 0.02),
        'ffn1_w_in': nrm(ks[11], (DEPTH, D_MODEL, 2 * D_FF), sd),
        'ffn1_w_out': nrm(ks[12], (DEPTH, D_FF, D_MODEL), D_FF ** -0.5 * BETA),
        'w_mix_in': jnp.concatenate([nrm(ks[15], (DEPTH, D_MODEL, 2 * W_A), sd),
                                     nrm(ks[16], (DEPTH, D_MODEL, W_A), sd * BETA),
                                     nrm(ks[17], (DEPTH, D_MODEL, 2 * W_B), sd)], axis=-1),
        'w_mix_out': nrm(ks[18], (DEPTH, D_MIX, D_MODEL), D_MIX ** -0.5 * BETA),
        'lambda_q1': nrm(ks[19], (DEPTH, DH_A), 0.1),
        'lambda_k1': nrm(ks[20], (DEPTH, DH_A), 0.1),
        'lambda_q2': nrm(ks[21], (DEPTH, DH_A), 0.1),
        'lambda_k2': nrm(ks[22], (DEPTH, DH_A), 0.1),
        'subln_g': 1.0 + nrm(ks[23], (DEPTH, E_A), 0.02),
        'sgu_ln_g': 1.0 + nrm(ks[24], (DEPTH, W_B), 0.02),
        'sgu_ln_b': nrm(ks[25], (DEPTH, W_B), 0.02),
        'sgu_w': nrm(ks[26], (DEPTH, G_B, CHUNK, CHUNK), CHUNK ** -0.5),
        'sgu_b': 1.0 + nrm(ks[27], (DEPTH, G_B, CHUNK), 0.1),
        'xq_w': nrm(ks[28], (DEPTH, D_MODEL, D_MODEL), sd),
        'xkv_w': jnp.concatenate([nrm(ks[29], (DEPTH, D_MODEL, D_MODEL), sd),
                                  nrm(ks[30], (DEPTH, D_MODEL, D_MODEL), sd * BETA)], axis=-1),
        'xo_w': nrm(ks[31], (DEPTH, D_MODEL, D_MODEL), sd * BETA),
        'ffn2_w_in': nrm(ks[13], (DEPTH, D_MODEL, 2 * D_FF), sd),
        'ffn2_w_out': nrm(ks[14], (DEPTH, D_FF, D_MODEL), D_FF ** -0.5 * BETA),
    }


def reference(x_prompt, x_sample, mem_prompt, cache_k, cache_v, cache_mem_k, cache_mem_v, page_table,
              rel_bias, ln_g, ln_b, ffn1_w_in, ffn1_w_out, w_mix_in, w_mix_out,
              lambda_q1, lambda_k1, lambda_q2, lambda_k2, subln_g, sgu_ln_g, sgu_ln_b, sgu_w, sgu_b,
              xq_w, xkv_w, xo_w, ffn2_w_in, ffn2_w_out):
    yp, ys = x_prompt, x_sample
    k_p, v_p, mk_p, mv_p, k_s, v_s, g_s = [], [], [], [], [], [], []
    for l in range(DEPTH):
        lambda_init = 0.8 - 0.6 * math.exp(-0.3 * l)
        lam = diff_lambda(lambda_q1[l], lambda_k1[l], lambda_q2[l], lambda_k2[l], lambda_init)
        w = (ln_g[l], ln_b[l], ffn1_w_in[l], ffn1_w_out[l], w_mix_in[l], w_mix_out[l],
             subln_g[l], sgu_ln_g[l], sgu_ln_b[l], sgu_w[l], sgu_b[l], xq_w[l], xo_w[l],
             ffn2_w_in[l], ffn2_w_out[l])
        mk, mv = mem_kv(mem_prompt, xkv_w[l])
        yp, kp, vp, _ = trunk_layer(yp, mk, mv, prompt_attend(rel_bias), lam, lambda_init, *w)
        k_p.append(kp); v_p.append(vp); mk_p.append(mk); mv_p.append(mv)
        ys, ksl, vsl, gsl = trunk_layer(ys, cache_mem_k[l], cache_mem_v[l],
                                        sample_attend(cache_k[l], cache_v[l], page_table, rel_bias),
                                        lam, lambda_init, *w)
        k_s.append(ksl); v_s.append(vsl); g_s.append(gsl)
    return (yp, ys, jnp.stack(k_p), jnp.stack(v_p), jnp.stack(mk_p), jnp.stack(mv_p),
            jnp.stack(k_s), jnp.stack(v_s), jnp.stack(g_s))
```

```python
import functools
import math

import jax
import jax.numpy as jnp
from jax import lax
from jax.experimental import pallas as pl
from jax.experimental.pallas import tpu as pltpu

F32 = jnp.float32
BF16 = jnp.bfloat16

LN_EPS = 1e-5
NEG_INF = -1e30
LOG2E = 1.4426950408889634
DH_A = 64
E_A = 2 * DH_A
CHUNK = 128
NB = 32
MAX_DIST = 128

LANES = 128
MXU_EDGE = 256
MIB = 1024 * 1024


def _params(semantics, vmem_mib):
    return pltpu.CompilerParams(dimension_semantics=semantics, vmem_limit_bytes=vmem_mib * MIB)


def _resident(shape):
    nd = len(shape)
    return pl.BlockSpec(shape, lambda *_: (0,) * nd, pipeline_mode=pl.Buffered(1))


def _layer_norm(x, g, b):
    mu = jnp.mean(x, -1, keepdims=True)
    xc = x - mu
    var = jnp.mean(xc * xc, -1, keepdims=True)
    return xc * lax.rsqrt(var + LN_EPS) * g + b


def _gelu(x):
    return 0.5 * x * (1.0 + lax.erf(x * math.sqrt(0.5)))


def _dot(a, b):
    return jnp.dot(a, b, preferred_element_type=F32)


def _dot_nt(a, b):
    return lax.dot_general(a, b, (((1,), (1,)), ((), ())), preferred_element_type=F32)


def _diff_lambda(lam_ref, lambda_init):
    lv = lam_ref[...]
    a = jnp.sum(lv[0:1] * lv[1:2], axis=-1, keepdims=True)
    b = jnp.sum(lv[2:3] * lv[3:4], axis=-1, keepdims=True)
    return jnp.exp(a) - jnp.exp(b) + lambda_init


def _head_rmsnorm(o, g, lambda_init):
    return o * lax.rsqrt(jnp.mean(o * o, -1, keepdims=True) + LN_EPS) * g * (1.0 - lambda_init)


def _ffn_ln_kernel(x_ref, wa_ref, wb_ref, wo_ref, g_ref, b_ref, o_ref, acc_ref, *, alpha):
    x = x_ref[...]
    xb = x.astype(BF16)
    acc_ref[...] = jnp.zeros_like(acc_ref)

    def body(j, carry):
        ha = _dot(xb, wa_ref[j])
        hb = _dot(xb, wb_ref[j])
        act = (ha * jax.nn.sigmoid(ha) * hb).astype(BF16)
        acc_ref[...] += _dot(act, wo_ref[j])
        return carry

    lax.fori_loop(0, wa_ref.shape[0], body, 0)
    o_ref[...] = _layer_norm(alpha * x + 0.5 * acc_ref[...], g_ref[...], b_ref[...])


def _ffn_ln(x, wa, wb, wo, g, b, *, alpha, tm):
    n, d = x.shape
    kern = functools.partial(_ffn_ln_kernel, alpha=alpha)
    return pl.pallas_call(
        kern,
        grid=(n // tm,),
        in_specs=[
            pl.BlockSpec((tm, d), lambda i: (i, 0)),
            _resident(wa.shape), _resident(wb.shape), _resident(wo.shape),
            _resident(g.shape), _resident(b.shape),
        ],
        out_specs=pl.BlockSpec((tm, d), lambda i: (i, 0)),
        out_shape=jax.ShapeDtypeStruct((n, d), F32),
        scratch_shapes=[pltpu.VMEM((tm, d), F32)],
        compiler_params=_params(("parallel",), 48),
        name="ffn_ln",
    )(x, wa, wb, wo, g, b)


def _mix_in_kernel(x_ref, w_ref, lng_ref, lnb_ref, tw_ref, sb_ref,
                   q_ref, k_ref, kb_ref, v_ref, vb_ref, ob_ref, *, w_a, w_b, qscale):
    xb = x_ref[...].astype(BF16)
    tm = xb.shape[0]
    q_ref[...] = (_dot(xb, w_ref[:, 0:w_a]) * qscale).astype(BF16)
    hk = _dot(xb, w_ref[:, w_a:2 * w_a])
    k_ref[...] = hk
    kb_ref[...] = hk.astype(BF16)
    hv = _dot(xb, w_ref[:, 2 * w_a:3 * w_a])
    v_ref[...] = hv
    vb_ref[...] = hv.astype(BF16)
    gu = _gelu(_dot(xb, w_ref[:, 3 * w_a:3 * w_a + w_b]))
    gv = _gelu(_dot(xb, w_ref[:, 3 * w_a + w_b:3 * w_a + 2 * w_b]))
    vn = _layer_norm(gv, lng_ref[...], lnb_ref[...]).astype(BF16)
    n_groups = tw_ref.shape[0]
    cg = w_b // n_groups
    for c in range(tm // CHUNK):
        rows = slice(c * CHUNK, (c + 1) * CHUNK)
        for g in range(n_groups):
            cols = slice(g * cg, (g + 1) * cg)
            mixed = _dot(tw_ref[g], vn[rows, cols]) + sb_ref[g]
            ob_ref[rows, cols] = (gu[rows, cols] * mixed).astype(BF16)


def _mix_in(x, w, lng, lnb, tw, sb, *, w_a, w_b, qscale, tm):
    n, d = x.shape
    kern = functools.partial(_mix_in_kernel, w_a=w_a, w_b=w_b, qscale=qscale)
    row = lambda width: pl.BlockSpec((tm, width), lambda i: (i, 0))
    shp = lambda width, dt: jax.ShapeDtypeStruct((n, width), dt)
    return pl.pallas_call(
        kern,
        grid=(n // tm,),
        in_specs=[row(d), _resident(w.shape), _resident(lng.shape), _resident(lnb.shape),
                  _resident(tw.shape), _resident(sb.shape)],
        out_specs=[row(w_a), row(w_a), row(w_a), row(w_a), row(w_a), row(w_b)],
        out_shape=[shp(w_a, BF16), shp(w_a, F32), shp(w_a, BF16), shp(w_a, F32), shp(w_a, BF16),
                   shp(w_b, BF16)],
        compiler_params=_params(("parallel",), 48),
        name="mix_in_sgu",
    )(x, w, lng, lnb, tw, sb)


def _mix_in_decode_kernel(x_ref, w_ref, lng_ref, lnb_ref, w00_ref, b0_ref,
                          q_ref, k_ref, v_ref, ob_ref, vn_ref, *, w_a, w_b, qscale):
    xb = x_ref[...].astype(BF16)
    q_ref[...] = _dot(xb, w_ref[:, 0:w_a]) * qscale
    k_ref[...] = _dot(xb, w_ref[:, w_a:2 * w_a])
    v_ref[...] = _dot(xb, w_ref[:, 2 * w_a:3 * w_a])
    gu = _gelu(_dot(xb, w_ref[:, 3 * w_a:3 * w_a + w_b]))
    gv = _gelu(_dot(xb, w_ref[:, 3 * w_a + w_b:3 * w_a + 2 * w_b]))
    vn = _layer_norm(gv, lng_ref[...], lnb_ref[...])
    vn_ref[...] = vn
    ob_ref[...] = (gu * (w00_ref[...] * vn + b0_ref[...])).astype(BF16)


def _mix_in_decode(x, w, lng, lnb, w00, b0, *, w_a, w_b, qscale):
    n, d = x.shape
    kern = functools.partial(_mix_in_decode_kernel, w_a=w_a, w_b=w_b, qscale=qscale)
    full = lambda shape: pl.BlockSpec(shape, lambda i: (0,) * len(shape))
    return pl.pallas_call(
        kern,
        grid=(1,),
        in_specs=[full(x.shape), full(w.shape), full(lng.shape), full(lnb.shape),
                  full(w00.shape), full(b0.shape)],
        out_specs=[full((n, w_a)), full((n, w_a)), full((n, w_a)), full((n, w_b)), full((n, w_b))],
        out_shape=[jax.ShapeDtypeStruct((n, w_a), F32), jax.ShapeDtypeStruct((n, w_a), F32),
                   jax.ShapeDtypeStruct((n, w_a), F32), jax.ShapeDtypeStruct((n, w_b), BF16),
                   jax.ShapeDtypeStruct((n, w_b), F32)],
        compiler_params=_params(("arbitrary",), 32),
        name="mix_in_decode",
    )(x, w, lng, lnb, w00, b0)


def _attn_kernel(lam_ref, g_ref, q_ref, k_ref, v_ref, b0_ref, b1_ref, o_ref,
                 qq_ref, m_ref, l_ref, acc_ref, *, t, lambda_init):
    qi = pl.program_id(2)
    q = q_ref[...]
    lane = lax.broadcasted_iota(jnp.int32, q.shape, 1)
    zero = jnp.zeros_like(q)
    qq_ref[0:t, :] = jnp.where(lane < DH_A, q, zero)
    qq_ref[t:2 * t, :] = jnp.where(lane < DH_A, zero, q)
    m_ref[...] = jnp.full_like(m_ref, NEG_INF)
    l_ref[...] = jnp.zeros_like(l_ref)
    acc_ref[...] = jnp.zeros_like(acc_ref)

    def step(j, bias, causal):
        off = pl.multiple_of(j * t, t)
        s = _dot_nt(qq_ref[...], k_ref[pl.ds(off, t), :])
        if bias is not None:
            s = s + bias
        if causal:
            r = lax.broadcasted_iota(jnp.int32, (2 * t, t), 0)
            c = lax.broadcasted_iota(jnp.int32, (2 * t, t), 1)
            s = jnp.where(jnp.where(r >= t, r - t, r) >= c, s, NEG_INF)
        m_old = m_ref[...]
        m_new = jnp.maximum(m_old, jnp.max(s, axis=-1, keepdims=True))
        alpha = jnp.exp2(m_old - m_new)
        p = jnp.exp2(s - m_new)
        l_ref[...] = alpha * l_ref[...] + jnp.sum(p, axis=-1, keepdims=True)
        acc_ref[...] = alpha * acc_ref[...] + _dot(p.astype(BF16), v_ref[pl.ds(off, t), :])
        m_ref[...] = m_new

    def far(j, carry):
        step(j, None, False)
        return carry

    lax.fori_loop(0, jnp.maximum(qi - 1, 0), far, 0)

    @pl.when(qi >= 1)
    def _():
        step(qi - 1, b1_ref[...], False)

    step(qi, b0_ref[...], True)

    o = acc_ref[...] / l_ref[...]
    lam = _diff_lambda(lam_ref, lambda_init)
    o = o[0:t] - lam * o[t:2 * t]
    o_ref[...] = _head_rmsnorm(o, g_ref[...], lambda_init).astype(BF16)


def _attention(lam4, subln_g, q, k, v, b0, b1, *, t, lambda_init):
    bsz, s, w_a = q.shape
    n_heads = w_a // E_A
    kern = functools.partial(_attn_kernel, t=t, lambda_init=lambda_init)
    const = lambda shape: pl.BlockSpec(shape, lambda b, h, i: (0,) * len(shape))
    return pl.pallas_call(
        kern,
        grid=(bsz, n_heads, s // t),
        in_specs=[
            const(lam4.shape), const(subln_g.shape),
            pl.BlockSpec((None, t, E_A), lambda b, h, i: (b, i, h)),
            pl.BlockSpec((None, s, E_A), lambda b, h, i: (b, 0, h)),
            pl.BlockSpec((None, s, E_A), lambda b, h, i: (b, 0, h)),
            pl.BlockSpec((None, 2 * t, t), lambda b, h, i: (h, 0, 0)),
            pl.BlockSpec((None, 2 * t, t), lambda b, h, i: (h, 0, 0)),
        ],
        out_specs=pl.BlockSpec((None, t, E_A), lambda b, h, i: (b, i, h)),
        out_shape=jax.ShapeDtypeStruct((bsz, s, w_a), BF16),
        scratch_shapes=[pltpu.VMEM((2 * t, E_A), BF16), pltpu.VMEM((2 * t, 1), F32),
                        pltpu.VMEM((2 * t, 1), F32), pltpu.VMEM((2 * t, E_A), F32)],
        compiler_params=_params(("parallel", "parallel", "arbitrary"), 32),
        name="diff_attn_prompt",
    )(lam4, subln_g, q, k, v, b0, b1)


def _decode_attn_kernel(pt_ref, lam_ref, g_ref, q_ref, kn_ref, vn_ref, bias_ref, *rest,
                        pages, lambda_init):
    del pt_ref
    k_refs = rest[:pages]
    v_refs = rest[pages:2 * pages]
    o_ref, kb_ref, vb_ref, m_ref, l_ref, acc_ref = rest[2 * pages:]
    j = pl.program_id(1)
    last = pl.num_programs(1) - 1
    rows = acc_ref.shape[0]
    w_a = acc_ref.shape[1]

    @pl.when(j == 0)
    def _():
        m_ref[...] = jnp.full_like(m_ref, NEG_INF)
        l_ref[...] = jnp.zeros_like(l_ref)
        acc_ref[...] = jnp.zeros_like(acc_ref)

    r_id = lax.broadcasted_iota(jnp.int32, (rows, w_a), 0)
    c_id = lax.broadcasted_iota(jnp.int32, (rows, w_a), 1)
    own = (c_id // DH_A) == r_id
    qblk = jnp.where(own, q_ref[...], 0.0)

    for i in range(pages):
        kb_ref[i * CHUNK:(i + 1) * CHUNK, :] = k_refs[i][...].astype(BF16)
        vb_ref[i * CHUNK:(i + 1) * CHUNK, :] = v_refs[i][...].astype(BF16)

    s = _dot_nt(qblk.astype(BF16), kb_ref[...])
    near = jnp.where(j == last, 1.0, 0.0)
    n_keys = pages * CHUNK
    s_tail = s[:, n_keys - CHUNK:] + near * bias_ref[:, 0:CHUNK]
    s = jnp.concatenate([s[:, :n_keys - CHUNK], s_tail], axis=1) if pages > 1 else s_tail

    m_old = m_ref[...]
    m_new = jnp.maximum(m_old, jnp.max(s, axis=-1, keepdims=True))
    alpha = jnp.exp2(m_old - m_new)
    p = jnp.exp2(s - m_new)
    l_ref[...] = alpha * l_ref[...] + jnp.sum(p, axis=-1, keepdims=True)
    acc_ref[...] = alpha * acc_ref[...] + _dot(p.astype(BF16), vb_ref[...])
    m_ref[...] = m_new

    @pl.when(j == last)
    def _():
        s_new = jnp.sum(qblk * kn_ref[...], axis=-1, keepdims=True) + bias_ref[:, CHUNK:CHUNK + 1]
        m_old = m_ref[...]
        m_new = jnp.maximum(m_old, s_new)
        alpha = jnp.exp2(m_old - m_new)
        p_new = jnp.exp2(s_new - m_new)
        l_fin = alpha * l_ref[...] + p_new
        acc = alpha * acc_ref[...] + p_new * vn_ref[...]
        o_all = jnp.where(((c_id // E_A) == (r_id // 2)), acc / l_fin, 0.0)
        lam = _diff_lambda(lam_ref, lambda_init)
        sign = jnp.where((r_id % 2) == 0, 1.0, -lam)
        o = jnp.sum(o_all * sign, axis=0, keepdims=True)
        g = g_ref[...]
        for h in range(w_a // E_A):
            cols = slice(h * E_A, (h + 1) * E_A)
            o_ref[:, cols] = _head_rmsnorm(o[:, cols], g, lambda_init).astype(BF16)


def _decode_attention(page_table, lam4, subln_g, q, k_new, v_new, bias, cache_k, cache_v, *,
                      pages, lambda_init):
    n_seq, _, w_a = q.shape
    n_pages = page_table.shape[1]
    rows = 2 * (w_a // E_A)
    page_rows = cache_k.shape[1]
    kern = functools.partial(_decode_attn_kernel, pages=pages, lambda_init=lambda_init)
    const = lambda shape: pl.BlockSpec(shape, lambda b, j, pt: (0,) * len(shape))
    tok = pl.BlockSpec((None, 1, w_a), lambda b, j, pt: (b, 0, 0))

    def page_spec(i):
        return pl.BlockSpec((None, page_rows, w_a),
                            lambda b, j, pt: (pt[b * n_pages + j * pages + i], 0, 0))

    grid_spec = pltpu.PrefetchScalarGridSpec(
        num_scalar_prefetch=1,
        grid=(n_seq, n_pages // pages),
        in_specs=[const(lam4.shape), const(subln_g.shape), tok, tok, tok, const(bias.shape)]
        + [page_spec(i) for i in range(pages)] + [page_spec(i) for i in range(pages)],
        out_specs=pl.BlockSpec((None, 1, w_a), lambda b, j, pt: (b, 0, 0)),
        scratch_shapes=[pltpu.VMEM((pages * page_rows, w_a), BF16),
                        pltpu.VMEM((pages * page_rows, w_a), BF16),
                        pltpu.VMEM((rows, 1), F32), pltpu.VMEM((rows, 1), F32),
                        pltpu.VMEM((rows, w_a), F32)],
    )
    return pl.pallas_call(
        kern,
        grid_spec=grid_spec,
        out_shape=jax.ShapeDtypeStruct((n_seq, 1, w_a), BF16),
        compiler_params=_params(("parallel", "arbitrary"), 40),
        name="diff_attn_decode",
    )(page_table.reshape(-1), lam4, subln_g, q, k_new, v_new, bias,
      *([cache_k] * pages), *([cache_v] * pages))


def _mix_out_kernel(x_ref, oa_ref, ob_ref, woa_ref, wob_ref, g_ref, b_ref, wq_ref,
                    x2_ref, qx_ref, *, alpha, qscale):
    y = _dot(oa_ref[...], woa_ref[...]) + _dot(ob_ref[...], wob_ref[...])
    x2 = _layer_norm(alpha * x_ref[...] + y, g_ref[...], b_ref[...])
    x2_ref[...] = x2
    qx_ref[...] = (_dot(x2.astype(BF16), wq_ref[...]) * qscale).astype(BF16)


def _mix_out(x, oa, ob, woa, wob, g, b, wq, *, alpha, qscale, tm):
    n, d = x.shape
    kern = functools.partial(_mix_out_kernel, alpha=alpha, qscale=qscale)
    row = lambda width: pl.BlockSpec((tm, width), lambda i: (i, 0))
    return pl.pallas_call(
        kern,
        grid=(n // tm,),
        in_specs=[row(d), row(oa.shape[1]), row(ob.shape[1]), _resident(woa.shape),
                  _resident(wob.shape), _resident(g.shape), _resident(b.shape), _resident(wq.shape)],
        out_specs=[row(d), row(d)],
        out_shape=[jax.ShapeDtypeStruct((n, d), F32), jax.ShapeDtypeStruct((n, d), BF16)],
        compiler_params=_params(("parallel",), 40),
        name="mix_out_ln_q",
    )(x, oa, ob, woa, wob, g, b, wq)


def _cross_kernel(q_ref, mk_ref, mv_ref, o_ref, *, n_heads):
    dh = q_ref.shape[1] // n_heads
    for h in range(n_heads):
        cols = slice(h * dh, (h + 1) * dh)
        s = _dot_nt(q_ref[:, cols], mk_ref[:, cols].astype(BF16))
        p = jnp.exp2(s - jnp.max(s, axis=-1, keepdims=True))
        l = jnp.sum(p, axis=-1, keepdims=True)
        o = _dot(p.astype(BF16), mv_ref[:, cols].astype(BF16)) / l
        o_ref[:, cols] = o.astype(BF16)


def _cross_attention(qx, mk, mv, *, n_heads, tm):
    bsz, s, d = qx.shape
    n_mem = mk.shape[1]
    kern = functools.partial(_cross_kernel, n_heads=n_heads)
    return pl.pallas_call(
        kern,
        grid=(bsz, s // tm),
        in_specs=[pl.BlockSpec((None, tm, d), lambda b, i: (b, i, 0)),
                  pl.BlockSpec((None, n_mem, d), lambda b, i: (b, 0, 0)),
                  pl.BlockSpec((None, n_mem, d), lambda b, i: (b, 0, 0))],
        out_specs=pl.BlockSpec((None, tm, d), lambda b, i: (b, i, 0)),
        out_shape=jax.ShapeDtypeStruct((bsz, s, d), BF16),
        compiler_params=_params(("parallel", "parallel"), 32),
        name="cross_attn_prompt",
    )(qx, mk, mv)


def _cross_decode_kernel(q_ref, mk_ref, mv_ref, o_ref, *, n_heads):
    d = q_ref.shape[1]
    dh = d // n_heads
    rows = 8
    r_id = lax.broadcasted_iota(jnp.int32, (rows, d), 0)
    c_id = lax.broadcasted_iota(jnp.int32, (rows, d), 1)
    own = (c_id // dh) == r_id
    qblk = jnp.where(own, q_ref[...].astype(F32), 0.0).astype(BF16)
    s = _dot_nt(qblk, mk_ref[...].astype(BF16))
    p = jnp.exp2(s - jnp.max(s, axis=-1, keepdims=True))
    l = jnp.sum(p, axis=-1, keepdims=True)
    o_all = _dot(p.astype(BF16), mv_ref[...].astype(BF16)) / l
    o_ref[...] = jnp.sum(jnp.where(own, o_all, 0.0), axis=0, keepdims=True).astype(BF16)


def _cross_attention_decode(qx, mk, mv, *, n_heads):
    n_seq, _, d = qx.shape
    n_mem = mk.shape[1]
    kern = functools.partial(_cross_decode_kernel, n_heads=n_heads)
    return pl.pallas_call(
        kern,
        grid=(n_seq,),
        in_specs=[pl.BlockSpec((None, 1, d), lambda b: (b, 0, 0)),
                  pl.BlockSpec((None, n_mem, d), lambda b: (b, 0, 0)),
                  pl.BlockSpec((None, n_mem, d), lambda b: (b, 0, 0))],
        out_specs=pl.BlockSpec((None, 1, d), lambda b: (b, 0, 0)),
        out_shape=jax.ShapeDtypeStruct((n_seq, 1, d), BF16),
        compiler_params=_params(("parallel",), 32),
        name="cross_attn_decode",
    )(qx, mk, mv)


def _proj_ln_kernel(x_ref, o_ref, w_ref, g_ref, b_ref, y_ref, *, alpha):
    y = _dot(o_ref[...], w_ref[...])
    y_ref[...] = _layer_norm(alpha * x_ref[...] + y, g_ref[...], b_ref[...])


def _proj_ln(x, o, w, g, b, *, alpha, tm):
    n, d = x.shape
    kern = functools.partial(_proj_ln_kernel, alpha=alpha)
    row = pl.BlockSpec((tm, d), lambda i: (i, 0))
    return pl.pallas_call(
        kern,
        grid=(n // tm,),
        in_specs=[row, row, _resident(w.shape), _resident(g.shape), _resident(b.shape)],
        out_specs=row,
        out_shape=jax.ShapeDtypeStruct((n, d), F32),
        compiler_params=_params(("parallel",), 32),
        name="proj_ln",
    )(x, o, w, g, b)


def _mem_kv_kernel(m_ref, w_ref, k_ref, v_ref):
    d = k_ref.shape[1]
    mb = m_ref[...].astype(BF16)
    k_ref[...] = _dot(mb, w_ref[:, 0:d])
    v_ref[...] = _dot(mb, w_ref[:, d:2 * d])


def _mem_kv(mem, w, *, tm):
    n, d = mem.shape
    row = pl.BlockSpec((tm, d), lambda i: (i, 0))
    return pl.pallas_call(
        _mem_kv_kernel,
        grid=(n // tm,),
        in_specs=[row, _resident(w.shape)],
        out_specs=[row, row],
        out_shape=[jax.ShapeDtypeStruct((n, d), F32)] * 2,
        compiler_params=_params(("parallel",), 32),
        name="mem_kv",
    )(mem, w)


def _shifted_bias(table, dist):
    n = jnp.maximum(dist, 0)
    max_exact = NB // 2
    nf = jnp.maximum(n, 1).astype(F32)
    large = max_exact + (jnp.log(nf / max_exact) / math.log(MAX_DIST / max_exact)
                         * (NB - max_exact)).astype(jnp.int32)
    large = jnp.minimum(large, NB - 1)
    bucket = jnp.where(n < max_exact, n, large)
    tab = table.astype(F32)
    return (tab[bucket] - tab[NB - 1]) * LOG2E


def _split_ffn(w_in, w_out, fc):
    d, two_ff = w_in.shape
    d_ff = two_ff // 2
    n = d_ff // fc
    wa = w_in[:, :d_ff].reshape(d, n, fc).transpose(1, 0, 2).astype(BF16)
    wb = w_in[:, d_ff:].reshape(d, n, fc).transpose(1, 0, 2).astype(BF16)
    wo = w_out.reshape(n, fc, d).astype(BF16)
    return wa, wb, wo


def kernel(x_prompt, x_sample, mem_prompt, cache_k, cache_v, cache_mem_k, cache_mem_v, page_table, rel_bias, ln_g, ln_b, ffn1_w_in, ffn1_w_out, w_mix_in, w_mix_out, lambda_q1, lambda_k1, lambda_q2, lambda_k2, subln_g, sgu_ln_g, sgu_ln_b, sgu_w, sgu_b, xq_w, xkv_w, xo_w, ffn2_w_in, ffn2_w_out):
    bsz, seq, d = x_prompt.shape
    n_dec = x_sample.shape[0]
    depth = ln_g.shape[0]
    assert depth == 1 and x_sample.shape[1] == 1
    w_b = sgu_ln_g.shape[1]
    w_a = (w_mix_in.shape[2] - 2 * w_b) // 3
    n_heads = w_a // E_A
    n_groups = sgu_w.shape[1]
    n_mem, h_m, dh_m = cache_mem_k.shape[2:]
    page_rows = cache_k.shape[2]
    alpha = (2 * depth) ** 0.25
    lambda_init = 0.8 - 0.6 * math.exp(-0.3 * 0)
    l = 0
    t_attn = MXU_EDGE
    tm = 512
    assert seq % tm == 0 and seq % t_attn == 0 and t_attn >= MAX_DIST and tm % CHUNK == 0
    assert page_rows == CHUNK and MAX_DIST <= page_rows

    row2 = lambda a: a.reshape(1, -1)
    g_ln = [row2(ln_g[l, i]) for i in range(4)]
    b_ln = [row2(ln_b[l, i]) for i in range(4)]
    ffn1 = _split_ffn(ffn1_w_in[l], ffn1_w_out[l], MXU_EDGE)
    ffn2 = _split_ffn(ffn2_w_in[l], ffn2_w_out[l], MXU_EDGE)
    q_fold = jnp.concatenate([jnp.full((w_a,), DH_A ** -0.5, F32),
                              jnp.ones((w_mix_in.shape[2] - w_a,), F32)])
    w_mix = (w_mix_in[l] * q_fold).astype(BF16)
    woa = w_mix_out[l, :w_a].astype(BF16)
    wob = w_mix_out[l, w_a:].astype(BF16)
    wq_x = xq_w[l].astype(BF16)
    wo_x = xo_w[l].astype(BF16)
    wkv = xkv_w[l].astype(BF16)
    lam4 = jnp.stack([lambda_q1[l], lambda_k1[l], lambda_q2[l], lambda_k2[l]]).astype(F32)
    g_sub = row2(subln_g[l])
    lng_s, lnb_s = row2(sgu_ln_g[l]), row2(sgu_ln_b[l])
    tril_w = jnp.tril(sgu_w[l]).astype(BF16)
    sgu_bias = sgu_b[l].reshape(n_groups, CHUNK, 1)
    cg = w_b // n_groups
    w00 = jnp.repeat(sgu_w[l, :, 0, 0], cg).reshape(1, w_b)
    b00 = jnp.repeat(sgu_b[l, :, 0], cg).reshape(1, w_b)
    xq_scale = dh_m ** -0.5 * LOG2E

    r = jnp.arange(t_attn)[:, None]
    c = jnp.arange(t_attn)[None, :]
    stack2 = lambda b: jnp.tile(jnp.transpose(b, (2, 0, 1)), (1, 2, 1))
    bias0 = stack2(_shifted_bias(rel_bias, r - c))
    bias1 = stack2(_shifted_bias(rel_bias, r - c + t_attn))
    dec_dist = jnp.concatenate([page_rows - jnp.arange(page_rows), jnp.zeros((LANES,), jnp.int32)])
    dec_bias = jnp.repeat(_shifted_bias(rel_bias, dec_dist).T, 2, axis=0)

    n_tok = bsz * seq
    xp = x_prompt.reshape(n_tok, d)
    mk_p, mv_p = _mem_kv(mem_prompt.reshape(bsz * n_mem, d), wkv, tm=n_mem)
    x1 = _ffn_ln(xp, *ffn1, g_ln[0], b_ln[0], alpha=alpha, tm=tm)
    qb, k_p, kb, v_p, vb, ob = _mix_in(x1, w_mix, lng_s, lnb_s, tril_w, sgu_bias,
                                       w_a=w_a, w_b=w_b, qscale=LOG2E, tm=tm)
    oa = _attention(lam4, g_sub, qb.reshape(bsz, seq, w_a), kb.reshape(bsz, seq, w_a),
                    vb.reshape(bsz, seq, w_a), bias0, bias1, t=t_attn, lambda_init=lambda_init)
    x2, qx = _mix_out(x1, oa.reshape(n_tok, w_a), ob, woa, wob, g_ln[1], b_ln[1], wq_x,
                      alpha=alpha, qscale=xq_scale, tm=tm)
    ox = _cross_attention(qx.reshape(bsz, seq, d), mk_p.reshape(bsz, n_mem, d),
                          mv_p.reshape(bsz, n_mem, d), n_heads=h_m, tm=tm)
    x3 = _proj_ln(x2, ox.reshape(n_tok, d), wo_x, g_ln[2], b_ln[2], alpha=alpha, tm=tm)
    y_p = _ffn_ln(x3, *ffn2, g_ln[3], b_ln[3], alpha=alpha, tm=tm)

    xs = x_sample.reshape(n_dec, d)
    s1 = _ffn_ln(xs, *ffn1, g_ln[0], b_ln[0], alpha=alpha, tm=n_dec)
    q_s, k_s, v_s, ob_s, vn_s = _mix_in_decode(s1, w_mix, lng_s, lnb_s, w00, b00,
                                               w_a=w_a, w_b=w_b, qscale=LOG2E)
    oa_s = _decode_attention(page_table, lam4, g_sub, q_s.reshape(n_dec, 1, w_a),
                             k_s.reshape(n_dec, 1, w_a), v_s.reshape(n_dec, 1, w_a), dec_bias,
                             cache_k[l].reshape(-1, page_rows, w_a),
                             cache_v[l].reshape(-1, page_rows, w_a),
                             pages=8, lambda_init=lambda_init)
    s2, qx_s = _mix_out(s1, oa_s.reshape(n_dec, w_a), ob_s, woa, wob, g_ln[1], b_ln[1], wq_x,
                        alpha=alpha, qscale=xq_scale, tm=n_dec)
    ox_s = _cross_attention_decode(qx_s.reshape(n_dec, 1, d),
                                   cache_mem_k[l].reshape(n_dec, n_mem, d),
                                   cache_mem_v[l].reshape(n_dec, n_mem, d), n_heads=h_m)
    s3 = _proj_ln(s2, ox_s.reshape(n_dec, d), wo_x, g_ln[2], b_ln[2], alpha=alpha, tm=n_dec)
    y_s = _ffn_ln(s3, *ffn2, g_ln[3], b_ln[3], alpha=alpha, tm=n_dec)

    return (y_p.reshape(bsz, seq, d), y_s.reshape(n_dec, 1, d),
            k_p.reshape(1, bsz, seq, n_heads, E_A), v_p.reshape(1, bsz, seq, n_heads, E_A),
            mk_p.reshape(1, bsz, n_mem, h_m, dh_m), mv_p.reshape(1, bsz, n_mem, h_m, dh_m),
            k_s.reshape(1, n_dec, 1, n_heads, E_A), v_s.reshape(1, n_dec, 1, n_heads, E_A),
            vn_s.reshape(1, n_dec, 1, w_b))
```

```python
import functools
import math

import jax
import jax.numpy as jnp
from jax import lax
from jax.experimental import pallas as pl
from jax.experimental.pallas import tpu as pltpu

F32 = jnp.float32
BF16 = jnp.bfloat16

LN_EPS = 1e-5
NEG_INF = -1e30
LOG2E = 1.4426950408889634
DH_A = 64
E_A = 2 * DH_A
CHUNK = 128
NB = 32
MAX_DIST = 128

LANES = 128
MXU_EDGE = 256
MIB = 1024 * 1024


def _params(semantics, vmem_mib):
    return pltpu.CompilerParams(dimension_semantics=semantics, vmem_limit_bytes=vmem_mib * MIB)


def _resident(shape):
    nd = len(shape)
    return pl.BlockSpec(shape, lambda *_: (0,) * nd, pipeline_mode=pl.Buffered(1))


def _layer_norm(x, g, b):
    mu = jnp.mean(x, -1, keepdims=True)
    xc = x - mu
    var = jnp.mean(xc * xc, -1, keepdims=True)
    return xc * lax.rsqrt(var + LN_EPS) * g + b


def _gelu(x):
    return 0.5 * x * (1.0 + lax.erf(x * math.sqrt(0.5)))


def _dot(a, b):
    return jnp.dot(a, b, preferred_element_type=F32)


def _dot_nt(a, b):
    return lax.dot_general(a, b, (((1,), (1,)), ((), ())), preferred_element_type=F32)


def _diff_lambda(lam_ref, lambda_init):
    lv = lam_ref[...]
    a = jnp.sum(lv[0:1] * lv[1:2], axis=-1, keepdims=True)
    b = jnp.sum(lv[2:3] * lv[3:4], axis=-1, keepdims=True)
    return jnp.exp(a) - jnp.exp(b) + lambda_init


def _head_rmsnorm(o, g, lambda_init):
    return o * lax.rsqrt(jnp.mean(o * o, -1, keepdims=True) + LN_EPS) * g * (1.0 - lambda_init)


def _ffn_ln_kernel(x_ref, wa_ref, wb_ref, wo_ref, g_ref, b_ref, o_ref, acc_ref, *, alpha):
    x = x_ref[...]
    xb = x.astype(BF16)
    acc_ref[...] = jnp.zeros_like(acc_ref)

    def body(j, carry):
        ha = _dot(xb, wa_ref[j])
        hb = _dot(xb, wb_ref[j])
        act = (ha * jax.nn.sigmoid(ha) * hb).astype(BF16)
        acc_ref[...] += _dot(act, wo_ref[j])
        return carry

    lax.fori_loop(0, wa_ref.shape[0], body, 0)
    o_ref[...] = _layer_norm(alpha * x + 0.5 * acc_ref[...], g_ref[...], b_ref[...])


def _ffn_ln(x, wa, wb, wo, g, b, *, alpha, tm):
    n, d = x.shape
    kern = functools.partial(_ffn_ln_kernel, alpha=alpha)
    return pl.pallas_call(
        kern,
        grid=(n // tm,),
        in_specs=[
            pl.BlockSpec((tm, d), lambda i: (i, 0)),
            _resident(wa.shape), _resident(wb.shape), _resident(wo.shape),
            _resident(g.shape), _resident(b.shape),
        ],
        out_specs=pl.BlockSpec((tm, d), lambda i: (i, 0)),
        out_shape=jax.ShapeDtypeStruct((n, d), F32),
        scratch_shapes=[pltpu.VMEM((tm, d), F32)],
        compiler_params=_params(("parallel",), 48),
        name="ffn_ln",
    )(x, wa, wb, wo, g, b)


def _mix_in_kernel(x_ref, w_ref, lng_ref, lnb_ref, tw_ref, sb_ref,
                   q_ref, k_ref, kb_ref, v_ref, vb_ref, ob_ref, *, w_a, w_b, qscale):
    xb = x_ref[...].astype(BF16)
    tm = xb.shape[0]
    q_ref[...] = (_dot(xb, w_ref[:, 0:w_a]) * qscale).astype(BF16)
    hk = _dot(xb, w_ref[:, w_a:2 * w_a])
    kb_ref[...] = hk.astype(BF16)
    hv = _dot(xb, w_ref[:, 2 * w_a:3 * w_a])
    vb_ref[...] = hv.astype(BF16)
    for h in range(w_a // E_A):
        k_ref[:, h, :] = hk[:, h * E_A:(h + 1) * E_A]
        v_ref[:, h, :] = hv[:, h * E_A:(h + 1) * E_A]
    gu = _gelu(_dot(xb, w_ref[:, 3 * w_a:3 * w_a + w_b]))
    gv = _gelu(_dot(xb, w_ref[:, 3 * w_a + w_b:3 * w_a + 2 * w_b]))
    vn = _layer_norm(gv, lng_ref[...], lnb_ref[...]).astype(BF16)
    n_groups = tw_ref.shape[0]
    cg = w_b // n_groups
    for c in range(tm // CHUNK):
        rows = slice(c * CHUNK, (c + 1) * CHUNK)
        for g in range(n_groups):
            cols = slice(g * cg, (g + 1) * cg)
            mixed = _dot(tw_ref[g], vn[rows, cols]) + sb_ref[g]
            ob_ref[rows, cols] = (gu[rows, cols] * mixed).astype(BF16)


def _mix_in(x, w, lng, lnb, tw, sb, *, w_a, w_b, qscale, tm):
    n, d = x.shape
    kern = functools.partial(_mix_in_kernel, w_a=w_a, w_b=w_b, qscale=qscale)
    row = lambda width: pl.BlockSpec((tm, width), lambda i: (i, 0))
    shp = lambda width, dt: jax.ShapeDtypeStruct((n, width), dt)
    n_heads = w_a // E_A
    row_h = pl.BlockSpec((tm, n_heads, E_A), lambda i: (i, 0, 0))
    shp_h = jax.ShapeDtypeStruct((n, n_heads, E_A), F32)
    return pl.pallas_call(
        kern,
        grid=(n // tm,),
        in_specs=[row(d), _resident(w.shape), _resident(lng.shape), _resident(lnb.shape),
                  _resident(tw.shape), _resident(sb.shape)],
        out_specs=[row(w_a), row_h, row(w_a), row_h, row(w_a), row(w_b)],
        out_shape=[shp(w_a, BF16), shp_h, shp(w_a, BF16), shp_h, shp(w_a, BF16), shp(w_b, BF16)],
        compiler_params=_params(("parallel",), 48),
        name="mix_in_sgu",
    )(x, w, lng, lnb, tw, sb)


def _mix_in_decode_kernel(x_ref, w_ref, lng_ref, lnb_ref, w00_ref, b0_ref,
                          q_ref, k_ref, v_ref, ob_ref, vn_ref, *, w_a, w_b, qscale):
    xb = x_ref[...].astype(BF16)
    q_ref[...] = _dot(xb, w_ref[:, 0:w_a]) * qscale
    k_ref[...] = _dot(xb, w_ref[:, w_a:2 * w_a])
    v_ref[...] = _dot(xb, w_ref[:, 2 * w_a:3 * w_a])
    gu = _gelu(_dot(xb, w_ref[:, 3 * w_a:3 * w_a + w_b]))
    gv = _gelu(_dot(xb, w_ref[:, 3 * w_a + w_b:3 * w_a + 2 * w_b]))
    vn = _layer_norm(gv, lng_ref[...], lnb_ref[...])
    vn_ref[...] = vn
    ob_ref[...] = (gu * (w00_ref[...] * vn + b0_ref[...])).astype(BF16)


def _mix_in_decode(x, w, lng, lnb, w00, b0, *, w_a, w_b, qscale):
    n, d = x.shape
    kern = functools.partial(_mix_in_decode_kernel, w_a=w_a, w_b=w_b, qscale=qscale)
    full = lambda shape: pl.BlockSpec(shape, lambda i: (0,) * len(shape))
    return pl.pallas_call(
        kern,
        grid=(1,),
        in_specs=[full(x.shape), full(w.shape), full(lng.shape), full(lnb.shape),
                  full(w00.shape), full(b0.shape)],
        out_specs=[full((n, w_a)), full((n, w_a)), full((n, w_a)), full((n, w_b)), full((n, w_b))],
        out_shape=[jax.ShapeDtypeStruct((n, w_a), F32), jax.ShapeDtypeStruct((n, w_a), F32),
                   jax.ShapeDtypeStruct((n, w_a), F32), jax.ShapeDtypeStruct((n, w_b), BF16),
                   jax.ShapeDtypeStruct((n, w_b), F32)],
        compiler_params=_params(("arbitrary",), 32),
        name="mix_in_decode",
    )(x, w, lng, lnb, w00, b0)


def _attn_kernel(lam_ref, g_ref, q_ref, k_ref, v_ref, b0_ref, b1_ref, o_ref,
                 qq_ref, m_ref, acc_ref, *, t, n_heads, lambda_init):
    qi = pl.program_id(1)
    lane = lax.broadcasted_iota(jnp.int32, (t, E_A), 1)
    for h in range(n_heads):
        q = q_ref[:, h * E_A:(h + 1) * E_A]
        zero = jnp.zeros_like(q)
        qq_ref[h, 0:t, :] = jnp.where(lane < DH_A, q, zero)
        qq_ref[h, t:2 * t, :] = jnp.where(lane < DH_A, zero, q)
    m_ref[...] = jnp.full_like(m_ref, NEG_INF)
    acc_ref[...] = jnp.zeros_like(acc_ref)
    ones = jnp.ones((t, LANES), BF16)
    n_blk = t // LANES

    def step(j, bias_ref, causal):
        off = pl.multiple_of(j * t, t)
        for h in range(n_heads):
            cols = slice(h * E_A, (h + 1) * E_A)
            s = _dot_nt(qq_ref[h], k_ref[pl.ds(off, t), cols])
            if bias_ref is not None:
                b = bias_ref[h]
                s = s + jnp.concatenate([b, b], axis=0)
            if causal:
                r = lax.broadcasted_iota(jnp.int32, (2 * t, t), 0)
                c = lax.broadcasted_iota(jnp.int32, (2 * t, t), 1)
                s = jnp.where(jnp.where(r >= t, r - t, r) >= c, s, NEG_INF)
            blocks = [s[:, i * LANES:(i + 1) * LANES] for i in range(n_blk)]
            lane_max = functools.reduce(jnp.maximum, blocks)
            row_max = jnp.max(lane_max, axis=-1, keepdims=True)
            m_old = m_ref[h]
            m_new = jnp.maximum(m_old, jnp.broadcast_to(row_max, m_old.shape))
            alpha = jnp.exp2(m_old - m_new)
            p = jnp.concatenate([jnp.exp2(blk - m_new).astype(BF16) for blk in blocks], axis=1)
            va = jnp.concatenate([v_ref[pl.ds(off, t), cols], ones], axis=1)
            acc_ref[h] = jnp.concatenate([alpha, alpha], axis=1) * acc_ref[h] + _dot(p, va)
            m_ref[h] = m_new

    def far(j, carry):
        step(j, None, False)
        return carry

    lax.fori_loop(0, jnp.maximum(qi - 1, 0), far, 0)

    @pl.when(qi >= 1)
    def _():
        step(qi - 1, b1_ref, False)

    step(qi, b0_ref, True)

    lam = _diff_lambda(lam_ref, lambda_init)
    for h in range(n_heads):
        acc = acc_ref[h]
        o = acc[:, 0:E_A] / acc[:, E_A:2 * E_A]
        o = o[0:t] - lam * o[t:2 * t]
        o_ref[:, h * E_A:(h + 1) * E_A] = _head_rmsnorm(o, g_ref[...], lambda_init).astype(BF16)


def _attention(lam4, subln_g, q, k, v, b0, b1, *, t, lambda_init):
    bsz, s, w_a = q.shape
    n_heads = w_a // E_A
    kern = functools.partial(_attn_kernel, t=t, n_heads=n_heads, lambda_init=lambda_init)
    const = lambda shape: pl.BlockSpec(shape, lambda b, i: (0,) * len(shape))
    seq_spec = pl.BlockSpec((None, s, w_a), lambda b, i: (b, 0, 0), pipeline_mode=pl.Buffered(1))
    return pl.pallas_call(
        kern,
        grid=(bsz, s // t),
        in_specs=[
            const(lam4.shape), const(subln_g.shape),
            pl.BlockSpec((None, t, w_a), lambda b, i: (b, i, 0)),
            seq_spec, seq_spec,
            _resident(b0.shape), _resident(b1.shape),
        ],
        out_specs=pl.BlockSpec((None, t, w_a), lambda b, i: (b, i, 0)),
        out_shape=jax.ShapeDtypeStruct((bsz, s, w_a), BF16),
        scratch_shapes=[pltpu.VMEM((n_heads, 2 * t, E_A), BF16),
                        pltpu.VMEM((n_heads, 2 * t, LANES), F32),
                        pltpu.VMEM((n_heads, 2 * t, E_A + LANES), F32)],
        compiler_params=_params(("parallel", "arbitrary"), 52),
        name="diff_attn_prompt",
    )(lam4, subln_g, q, k, v, b0, b1)


def _decode_attn_kernel(pt_ref, lam_ref, g_ref, q_ref, kn_ref, vn_ref, bias_ref, new_bias_ref, *rest,
                        pages, n_heads, lambda_init):
    del pt_ref
    k_refs = rest[:pages]
    v_refs = rest[pages:2 * pages]
    o_ref, kb_ref, vb_ref, m_ref, l_ref, acc_ref = rest[2 * pages:]
    j = pl.program_id(1)
    last = pl.num_programs(1) - 1
    rows = 2 * n_heads
    page_len = k_refs[0].shape[0]

    @pl.when(j == 0)
    def _():
        m_ref[...] = jnp.full_like(m_ref, NEG_INF)
        l_ref[...] = jnp.zeros_like(l_ref)
        acc_ref[...] = jnp.zeros_like(acc_ref)

    r_id = lax.broadcasted_iota(jnp.int32, (rows, E_A), 0)
    l_id = lax.broadcasted_iota(jnp.int32, (rows, E_A), 1)

    def per_row_head(tok_ref):
        out = jnp.zeros((rows, E_A), F32)
        for h in range(n_heads):
            piece = jnp.broadcast_to(tok_ref[:, h * E_A:(h + 1) * E_A], (rows, E_A))
            out = jnp.where((r_id // 2) == h, piece, out)
        return out

    qm = jnp.where((l_id // DH_A) == (r_id % 2), per_row_head(q_ref), 0.0)

    for i in range(pages):
        kb_ref[i * page_len:(i + 1) * page_len, :] = k_refs[i][...].astype(BF16)
        vb_ref[i * page_len:(i + 1) * page_len, :] = v_refs[i][...].astype(BF16)

    n_keys = pages * page_len
    s = _dot_nt(qm.astype(BF16), kb_ref[...])
    near = jnp.where(j == last, 1.0, 0.0)
    s_tail = s[:, n_keys - page_len:] + near * bias_ref[...]
    s = jnp.concatenate([s[:, :n_keys - page_len], s_tail], axis=1) if pages > 1 else s_tail
    row_s = lax.broadcasted_iota(jnp.int32, s.shape, 0)
    col_s = lax.broadcasted_iota(jnp.int32, s.shape, 1)
    s = jnp.where((col_s % n_heads) == (row_s // 2), s, NEG_INF)

    m_old = m_ref[...]
    m_new = jnp.maximum(m_old, jnp.max(s, axis=-1, keepdims=True))
    alpha = jnp.exp2(m_old - m_new)
    p = jnp.exp2(s - m_new)
    l_ref[...] = alpha * l_ref[...] + jnp.sum(p, axis=-1, keepdims=True)
    acc_ref[...] = alpha * acc_ref[...] + _dot(p.astype(BF16), vb_ref[...])
    m_ref[...] = m_new

    @pl.when(j == last)
    def _():
        s_new = (jnp.sum(qm * per_row_head(kn_ref), axis=-1, keepdims=True) + new_bias_ref[...])
        m_old = m_ref[...]
        m_new = jnp.maximum(m_old, s_new)
        alpha = jnp.exp2(m_old - m_new)
        p_new = jnp.exp2(s_new - m_new)
        l_fin = alpha * l_ref[...] + p_new
        o = (alpha * acc_ref[...] + p_new * per_row_head(vn_ref)) / l_fin
        lam = _diff_lambda(lam_ref, lambda_init)
        g = g_ref[...]
        for h in range(n_heads):
            oh = o[2 * h:2 * h + 1] - lam * o[2 * h + 1:2 * h + 2]
            o_ref[:, h * E_A:(h + 1) * E_A] = _head_rmsnorm(oh, g, lambda_init).astype(BF16)


def _decode_attention(page_table, lam4, subln_g, q, k_new, v_new, bias, new_bias, cache_k, cache_v, *,
                      layer, pages, lambda_init):
    n_seq, _, w_a = q.shape
    n_pages = page_table.shape[1]
    n_heads = w_a // E_A
    rows = 2 * n_heads
    page_len = cache_k.shape[2]
    kern = functools.partial(_decode_attn_kernel, pages=pages, n_heads=n_heads,
                             lambda_init=lambda_init)
    const = lambda shape: pl.BlockSpec(shape, lambda b, j, pt: (0,) * len(shape))
    tok = pl.BlockSpec((None, 1, w_a), lambda b, j, pt: (b, 0, 0))

    def page_spec(i):
        return pl.BlockSpec((None, None, page_len, E_A),
                            lambda b, j, pt: (layer, pt[b * n_pages + j * pages + i], 0, 0))

    grid_spec = pltpu.PrefetchScalarGridSpec(
        num_scalar_prefetch=1,
        grid=(n_seq, n_pages // pages),
        in_specs=[const(lam4.shape), const(subln_g.shape), tok, tok, tok, const(bias.shape),
                  const(new_bias.shape)]
        + [page_spec(i) for i in range(pages)] + [page_spec(i) for i in range(pages)],
        out_specs=pl.BlockSpec((None, 1, w_a), lambda b, j, pt: (b, 0, 0)),
        scratch_shapes=[pltpu.VMEM((pages * page_len, E_A), BF16),
                        pltpu.VMEM((pages * page_len, E_A), BF16),
                        pltpu.VMEM((rows, 1), F32), pltpu.VMEM((rows, 1), F32),
                        pltpu.VMEM((rows, E_A), F32)],
    )
    return pl.pallas_call(
        kern,
        grid_spec=grid_spec,
        out_shape=jax.ShapeDtypeStruct((n_seq, 1, w_a), BF16),
        compiler_params=_params(("parallel", "arbitrary"), 40),
        name="diff_attn_decode",
    )(page_table.reshape(-1), lam4, subln_g, q, k_new, v_new, bias, new_bias,
      *([cache_k] * pages), *([cache_v] * pages))


def _mix_out_kernel(x_ref, oa_ref, ob_ref, woa_ref, wob_ref, g_ref, b_ref, wq_ref,
                    x2_ref, qx_ref, *, alpha, qscale):
    y = _dot(oa_ref[...], woa_ref[...]) + _dot(ob_ref[...], wob_ref[...])
    x2 = _layer_norm(alpha * x_ref[...] + y, g_ref[...], b_ref[...])
    x2_ref[...] = x2
    qx_ref[...] = (_dot(x2.astype(BF16), wq_ref[...]) * qscale).astype(BF16)


def _mix_out(x, oa, ob, woa, wob, g, b, wq, *, alpha, qscale, tm):
    n, d = x.shape
    kern = functools.partial(_mix_out_kernel, alpha=alpha, qscale=qscale)
    row = lambda width: pl.BlockSpec((tm, width), lambda i: (i, 0))
    return pl.pallas_call(
        kern,
        grid=(n // tm,),
        in_specs=[row(d), row(oa.shape[1]), row(ob.shape[1]), _resident(woa.shape),
                  _resident(wob.shape), _resident(g.shape), _resident(b.shape), _resident(wq.shape)],
        out_specs=[row(d), row(d)],
        out_shape=[jax.ShapeDtypeStruct((n, d), F32), jax.ShapeDtypeStruct((n, d), BF16)],
        compiler_params=_params(("parallel",), 40),
        name="mix_out_ln_q",
    )(x, oa, ob, woa, wob, g, b, wq)


def _cross_kernel(q_ref, mk_ref, mv_ref, o_ref, *, n_heads):
    dh = q_ref.shape[1] // n_heads
    for h in range(n_heads):
        cols = slice(h * dh, (h + 1) * dh)
        s = _dot_nt(q_ref[:, cols], mk_ref[:, cols].astype(BF16))
        p = jnp.exp2(s - jnp.max(s, axis=-1, keepdims=True))
        l = jnp.sum(p, axis=-1, keepdims=True)
        o = _dot(p.astype(BF16), mv_ref[:, cols].astype(BF16)) / l
        o_ref[:, cols] = o.astype(BF16)


def _cross_attention(qx, mk, mv, *, n_heads, tm):
    bsz, s, d = qx.shape
    n_mem = mk.shape[1]
    kern = functools.partial(_cross_kernel, n_heads=n_heads)
    return pl.pallas_call(
        kern,
        grid=(bsz, s // tm),
        in_specs=[pl.BlockSpec((None, tm, d), lambda b, i: (b, i, 0)),
                  pl.BlockSpec((None, n_mem, d), lambda b, i: (b, 0, 0)),
                  pl.BlockSpec((None, n_mem, d), lambda b, i: (b, 0, 0))],
        out_specs=pl.BlockSpec((None, tm, d), lambda b, i: (b, i, 0)),
        out_shape=jax.ShapeDtypeStruct((bsz, s, d), BF16),
        compiler_params=_params(("parallel", "parallel"), 32),
        name="cross_attn_prompt",
    )(qx, mk, mv)


def _cross_decode_kernel(q_ref, mk_ref, mv_ref, o_ref, *, n_heads):
    dh = mk_ref.shape[1]
    sublanes = 8
    q = q_ref[...].astype(F32)
    r_id = lax.broadcasted_iota(jnp.int32, (sublanes, dh), 0)
    qm = jnp.zeros((sublanes, dh), F32)
    for h in range(n_heads):
        piece = jnp.broadcast_to(q[:, h * dh:(h + 1) * dh], (sublanes, dh))
        qm = jnp.where((r_id % n_heads) == h, piece, qm)
    s = _dot_nt(qm.astype(BF16), mk_ref[...].astype(BF16))
    row_s = lax.broadcasted_iota(jnp.int32, s.shape, 0)
    col_s = lax.broadcasted_iota(jnp.int32, s.shape, 1)
    s = jnp.where((col_s % n_heads) == (row_s % n_heads), s, NEG_INF)
    p = jnp.exp2(s - jnp.max(s, axis=-1, keepdims=True))
    l = jnp.sum(p, axis=-1, keepdims=True)
    o = _dot(p.astype(BF16), mv_ref[...].astype(BF16)) / l
    for h in range(n_heads):
        o_ref[:, h * dh:(h + 1) * dh] = o[h:h + 1].astype(BF16)


def _cross_attention_decode(qx, mk, mv, *, layer, n_heads):
    n_seq, _, d = qx.shape
    mem_len, dh = mk.shape[2:]
    kern = functools.partial(_cross_decode_kernel, n_heads=n_heads)
    mem_spec = pl.BlockSpec((None, None, mem_len, dh), lambda b: (layer, b, 0, 0))
    return pl.pallas_call(
        kern,
        grid=(n_seq,),
        in_specs=[pl.BlockSpec((None, 1, d), lambda b: (b, 0, 0)), mem_spec, mem_spec],
        out_specs=pl.BlockSpec((None, 1, d), lambda b: (b, 0, 0)),
        out_shape=jax.ShapeDtypeStruct((n_seq, 1, d), BF16),
        compiler_params=_params(("parallel",), 32),
        name="cross_attn_decode",
    )(qx, mk, mv)


def _proj_ln_kernel(x_ref, o_ref, w_ref, g_ref, b_ref, y_ref, *, alpha):
    y = _dot(o_ref[...], w_ref[...])
    y_ref[...] = _layer_norm(alpha * x_ref[...] + y, g_ref[...], b_ref[...])


def _proj_ln(x, o, w, g, b, *, alpha, tm):
    n, d = x.shape
    kern = functools.partial(_proj_ln_kernel, alpha=alpha)
    row = pl.BlockSpec((tm, d), lambda i: (i, 0))
    return pl.pallas_call(
        kern,
        grid=(n // tm,),
        in_specs=[row, row, _resident(w.shape), _resident(g.shape), _resident(b.shape)],
        out_specs=row,
        out_shape=jax.ShapeDtypeStruct((n, d), F32),
        compiler_params=_params(("parallel",), 32),
        name="proj_ln",
    )(x, o, w, g, b)


def _mem_kv_kernel(m_ref, w_ref, k_ref, v_ref):
    d = k_ref.shape[1]
    mb = m_ref[...].astype(BF16)
    k_ref[...] = _dot(mb, w_ref[:, 0:d])
    v_ref[...] = _dot(mb, w_ref[:, d:2 * d])


def _mem_kv(mem, w, *, tm):
    n, d = mem.shape
    row = pl.BlockSpec((tm, d), lambda i: (i, 0))
    return pl.pallas_call(
        _mem_kv_kernel,
        grid=(n // tm,),
        in_specs=[row, _resident(w.shape)],
        out_specs=[row, row],
        out_shape=[jax.ShapeDtypeStruct((n, d), F32)] * 2,
        compiler_params=_params(("parallel",), 32),
        name="mem_kv",
    )(mem, w)


def _shifted_bias(table, dist):
    n = jnp.maximum(dist, 0)
    max_exact = NB // 2
    nf = jnp.maximum(n, 1).astype(F32)
    large = max_exact + (jnp.log(nf / max_exact) / math.log(MAX_DIST / max_exact)
                         * (NB - max_exact)).astype(jnp.int32)
    large = jnp.minimum(large, NB - 1)
    bucket = jnp.where(n < max_exact, n, large)
    tab = (table.astype(F32) - table[NB - 1].astype(F32)) * LOG2E
    hit = bucket[..., None, None] == jnp.arange(NB)[:, None]
    return jnp.sum(jnp.where(hit, tab, 0.0), axis=-2)


def _split_ffn(w_in, w_out, fc):
    d, two_ff = w_in.shape
    d_ff = two_ff // 2
    n = d_ff // fc
    wa = w_in[:, :d_ff].reshape(d, n, fc).transpose(1, 0, 2).astype(BF16)
    wb = w_in[:, d_ff:].reshape(d, n, fc).transpose(1, 0, 2).astype(BF16)
    wo = w_out.reshape(n, fc, d).astype(BF16)
    return wa, wb, wo


def kernel(x_prompt, x_sample, mem_prompt, cache_k, cache_v, cache_mem_k, cache_mem_v, page_table, rel_bias, ln_g, ln_b, ffn1_w_in, ffn1_w_out, w_mix_in, w_mix_out, lambda_q1, lambda_k1, lambda_q2, lambda_k2, subln_g, sgu_ln_g, sgu_ln_b, sgu_w, sgu_b, xq_w, xkv_w, xo_w, ffn2_w_in, ffn2_w_out):
    bsz, seq, d = x_prompt.shape
    n_dec = x_sample.shape[0]
    depth = ln_g.shape[0]
    assert depth == 1 and x_sample.shape[1] == 1
    w_b = sgu_ln_g.shape[1]
    w_a = (w_mix_in.shape[2] - 2 * w_b) // 3
    n_heads = w_a // E_A
    n_groups = sgu_w.shape[1]
    n_mem, h_m, dh_m = cache_mem_k.shape[2:]
    page_rows = cache_k.shape[2]
    alpha = (2 * depth) ** 0.25
    lambda_init = 0.8 - 0.6 * math.exp(-0.3 * 0)
    l = 0
    t_attn = 512
    tm = 512
    assert seq % tm == 0 and seq % t_attn == 0 and t_attn >= MAX_DIST and tm % CHUNK == 0
    assert page_rows == CHUNK and MAX_DIST <= page_rows

    row2 = lambda a: a.reshape(1, -1)
    g_ln = [row2(ln_g[l, i]) for i in range(4)]
    b_ln = [row2(ln_b[l, i]) for i in range(4)]
    ffn1 = _split_ffn(ffn1_w_in[l], ffn1_w_out[l], MXU_EDGE)
    ffn2 = _split_ffn(ffn2_w_in[l], ffn2_w_out[l], MXU_EDGE)
    q_fold = jnp.concatenate([jnp.full((w_a,), DH_A ** -0.5, F32),
                              jnp.ones((w_mix_in.shape[2] - w_a,), F32)])
    w_mix = (w_mix_in[l] * q_fold).astype(BF16)
    woa = w_mix_out[l, :w_a].astype(BF16)
    wob = w_mix_out[l, w_a:].astype(BF16)
    wq_x = xq_w[l].astype(BF16)
    wo_x = xo_w[l].astype(BF16)
    wkv = xkv_w[l].astype(BF16)
    lam4 = jnp.stack([lambda_q1[l], lambda_k1[l], lambda_q2[l], lambda_k2[l]]).astype(F32)
    g_sub = row2(subln_g[l])
    lng_s, lnb_s = row2(sgu_ln_g[l]), row2(sgu_ln_b[l])
    tril_w = jnp.tril(sgu_w[l]).astype(BF16)
    sgu_bias = sgu_b[l].reshape(n_groups, CHUNK, 1)
    cg = w_b // n_groups
    w00 = jnp.repeat(sgu_w[l, :, 0, 0], cg).reshape(1, w_b)
    b00 = jnp.repeat(sgu_b[l, :, 0], cg).reshape(1, w_b)
    xq_scale = dh_m ** -0.5 * LOG2E

    r = jnp.arange(t_attn)[:, None]
    c = jnp.arange(t_attn)[None, :]
    bias0 = jnp.transpose(_shifted_bias(rel_bias, r - c), (2, 0, 1))
    bias1 = jnp.transpose(_shifted_bias(rel_bias, r - c + t_attn), (2, 0, 1))
    dec_near = _shifted_bias(rel_bias, page_rows - jnp.arange(page_rows))
    dec_bias = jnp.repeat(jnp.repeat(dec_near.T, 2, axis=0), n_heads, axis=1)
    new_bias = jnp.repeat(_shifted_bias(rel_bias, jnp.zeros((1,), jnp.int32)).T, 2, axis=0)

    n_tok = bsz * seq
    xp = x_prompt.reshape(n_tok, d)
    mk_p, mv_p = _mem_kv(mem_prompt.reshape(bsz * n_mem, d), wkv, tm=n_mem)
    x1 = _ffn_ln(xp, *ffn1, g_ln[0], b_ln[0], alpha=alpha, tm=tm)
    qb, k_p, kb, v_p, vb, ob = _mix_in(x1, w_mix, lng_s, lnb_s, tril_w, sgu_bias,
                                       w_a=w_a, w_b=w_b, qscale=LOG2E, tm=tm)
    oa = _attention(lam4, g_sub, qb.reshape(bsz, seq, w_a), kb.reshape(bsz, seq, w_a),
                    vb.reshape(bsz, seq, w_a), bias0, bias1, t=t_attn, lambda_init=lambda_init)
    x2, qx = _mix_out(x1, oa.reshape(n_tok, w_a), ob, woa, wob, g_ln[1], b_ln[1], wq_x,
                      alpha=alpha, qscale=xq_scale, tm=tm)
    ox = _cross_attention(qx.reshape(bsz, seq, d), mk_p.reshape(bsz, n_mem, d),
                          mv_p.reshape(bsz, n_mem, d), n_heads=h_m, tm=tm)
    x3 = _proj_ln(x2, ox.reshape(n_tok, d), wo_x, g_ln[2], b_ln[2], alpha=alpha, tm=tm)
    y_p = _ffn_ln(x3, *ffn2, g_ln[3], b_ln[3], alpha=alpha, tm=tm)

    xs = x_sample.reshape(n_dec, d)
    s1 = _ffn_ln(xs, *ffn1, g_ln[0], b_ln[0], alpha=alpha, tm=n_dec)
    q_s, k_s, v_s, ob_s, vn_s = _mix_in_decode(s1, w_mix, lng_s, lnb_s, w00, b00,
                                               w_a=w_a, w_b=w_b, qscale=LOG2E)
    oa_s = _decode_attention(page_table, lam4, g_sub, q_s.reshape(n_dec, 1, w_a),
                             k_s.reshape(n_dec, 1, w_a), v_s.reshape(n_dec, 1, w_a), dec_bias, new_bias,
                             cache_k.reshape(depth, -1, page_rows * n_heads, E_A),
                             cache_v.reshape(depth, -1, page_rows * n_heads, E_A),
                             layer=l, pages=8, lambda_init=lambda_init)
    s2, qx_s = _mix_out(s1, oa_s.reshape(n_dec, w_a), ob_s, woa, wob, g_ln[1], b_ln[1], wq_x,
                        alpha=alpha, qscale=xq_scale, tm=n_dec)
    ox_s = _cross_attention_decode(qx_s.reshape(n_dec, 1, d),
                                   cache_mem_k.reshape(depth, n_dec, n_mem * h_m, dh_m),
                                   cache_mem_v.reshape(depth, n_dec, n_mem * h_m, dh_m),
                                   layer=l, n_heads=h_m)
    s3 = _proj_ln(s2, ox_s.reshape(n_dec, d), wo_x, g_ln[2], b_ln[2], alpha=alpha, tm=n_dec)
    y_s = _ffn_ln(s3, *ffn2, g_ln[3], b_ln[3], alpha=alpha, tm=n_dec)

    return (y_p.reshape(bsz, seq, d), y_s.reshape(n_dec, 1, d),
            k_p.reshape(1, bsz, seq, n_heads, E_A), v_p.reshape(1, bsz, seq, n_heads, E_A),
            mk_p.reshape(1, bsz, n_mem, h_m, dh_m), mv_p.reshape(1, bsz, n_mem, h_m, dh_m),
            k_s.reshape(1, n_dec, 1, n_heads, E_A), v_s.reshape(1, n_dec, 1, n_heads, E_A),
            vn_s.reshape(1, n_dec, 1, w_b))
```

```python
import functools
import math

import jax
import jax.numpy as jnp
from jax import lax
from jax.experimental import pallas as pl
from jax.experimental.pallas import tpu as pltpu

F32 = jnp.float32
BF16 = jnp.bfloat16

LN_EPS = 1e-5
NEG_INF = -1e30
LOG2E = 1.4426950408889634
DH_A = 64
E_A = 2 * DH_A
CHUNK = 128
NB = 32
MAX_DIST = 128

LANES = 128
MXU_EDGE = 256
MIB = 1024 * 1024


def _params(semantics, vmem_mib):
    return pltpu.CompilerParams(dimension_semantics=semantics, vmem_limit_bytes=vmem_mib * MIB)


def _resident(shape):
    nd = len(shape)
    return pl.BlockSpec(shape, lambda *_: (0,) * nd, pipeline_mode=pl.Buffered(1))


def _layer_norm(x, g, b):
    mu = jnp.mean(x, -1, keepdims=True)
    xc = x - mu
    var = jnp.mean(xc * xc, -1, keepdims=True)
    return xc * lax.rsqrt(var + LN_EPS) * g + b


def _gelu(x):
    return 0.5 * x * (1.0 + lax.erf(x * math.sqrt(0.5)))


def _dot(a, b):
    return jnp.dot(a, b, preferred_element_type=F32)


def _dot_nt(a, b):
    return lax.dot_general(a, b, (((1,), (1,)), ((), ())), preferred_element_type=F32)


def _diff_lambda(lam_ref, lambda_init):
    lv = lam_ref[...]
    a = jnp.sum(lv[0:1] * lv[1:2], axis=-1, keepdims=True)
    b = jnp.sum(lv[2:3] * lv[3:4], axis=-1, keepdims=True)
    return jnp.exp(a) - jnp.exp(b) + lambda_init


def _head_rmsnorm(o, g, lambda_init):
    return o * lax.rsqrt(jnp.mean(o * o, -1, keepdims=True) + LN_EPS) * g * (1.0 - lambda_init)


def _ffn_ln_kernel(x_ref, wa_ref, wb_ref, wo_ref, g_ref, b_ref, o_ref, act_ref, *, alpha, fc):
    x = x_ref[...]
    xb = x.astype(BF16)
    d_ff = wa_ref.shape[1]
    for j in range(d_ff // fc):
        cols = slice(j * fc, (j + 1) * fc)
        ha = _dot(xb, wa_ref[:, cols])
        hb = _dot(xb, wb_ref[:, cols])
        act_ref[:, cols] = (ha * jax.nn.sigmoid(ha) * hb).astype(BF16)
    y = _dot(act_ref[...], wo_ref[...])
    o_ref[...] = _layer_norm(alpha * x + 0.5 * y, g_ref[...], b_ref[...])


def _ffn_ln(x, wa, wb, wo, g, b, *, alpha, tm):
    n, d = x.shape
    d_ff = wa.shape[1]
    assert d_ff % MXU_EDGE == 0
    kern = functools.partial(_ffn_ln_kernel, alpha=alpha, fc=MXU_EDGE)
    return pl.pallas_call(
        kern,
        grid=(n // tm,),
        in_specs=[
            pl.BlockSpec((tm, d), lambda i: (i, 0)),
            _resident(wa.shape), _resident(wb.shape), _resident(wo.shape),
            _resident(g.shape), _resident(b.shape),
        ],
        out_specs=pl.BlockSpec((tm, d), lambda i: (i, 0)),
        out_shape=jax.ShapeDtypeStruct((n, d), F32),
        scratch_shapes=[pltpu.VMEM((tm, d_ff), BF16)],
        compiler_params=_params(("parallel",), 48),
        name="ffn_ln",
    )(x, wa, wb, wo, g, b)


def _mix_in_kernel(x_ref, w_ref, lng_ref, lnb_ref, tw_ref, sb_ref,
                   q_ref, k_ref, kb_ref, v_ref, vb_ref, ob_ref, *, w_a, w_b, qscale):
    xb = x_ref[...].astype(BF16)
    tm = xb.shape[0]
    q_ref[...] = (_dot(xb, w_ref[:, 0:w_a]) * qscale).astype(BF16)
    hk = _dot(xb, w_ref[:, w_a:2 * w_a])
    kb_ref[...] = hk.astype(BF16)
    hv = _dot(xb, w_ref[:, 2 * w_a:3 * w_a])
    vb_ref[...] = hv.astype(BF16)
    for h in range(w_a // E_A):
        k_ref[:, h, :] = hk[:, h * E_A:(h + 1) * E_A]
        v_ref[:, h, :] = hv[:, h * E_A:(h + 1) * E_A]
    gu = _gelu(_dot(xb, w_ref[:, 3 * w_a:3 * w_a + w_b]))
    gv = _gelu(_dot(xb, w_ref[:, 3 * w_a + w_b:3 * w_a + 2 * w_b]))
    vn = _layer_norm(gv, lng_ref[...], lnb_ref[...]).astype(BF16)
    n_groups = tw_ref.shape[0]
    cg = w_b // n_groups
    for c in range(tm // CHUNK):
        rows = slice(c * CHUNK, (c + 1) * CHUNK)
        for g in range(n_groups):
            cols = slice(g * cg, (g + 1) * cg)
            mixed = _dot(tw_ref[g], vn[rows, cols]) + sb_ref[g]
            ob_ref[rows, cols] = (gu[rows, cols] * mixed).astype(BF16)


def _mix_in(x, w, lng, lnb, tw, sb, *, w_a, w_b, qscale, tm):
    n, d = x.shape
    kern = functools.partial(_mix_in_kernel, w_a=w_a, w_b=w_b, qscale=qscale)
    row = lambda width: pl.BlockSpec((tm, width), lambda i: (i, 0))
    shp = lambda width, dt: jax.ShapeDtypeStruct((n, width), dt)
    n_heads = w_a // E_A
    row_h = pl.BlockSpec((tm, n_heads, E_A), lambda i: (i, 0, 0))
    shp_h = jax.ShapeDtypeStruct((n, n_heads, E_A), F32)
    return pl.pallas_call(
        kern,
        grid=(n // tm,),
        in_specs=[row(d), _resident(w.shape), _resident(lng.shape), _resident(lnb.shape),
                  _resident(tw.shape), _resident(sb.shape)],
        out_specs=[row(w_a), row_h, row(w_a), row_h, row(w_a), row(w_b)],
        out_shape=[shp(w_a, BF16), shp_h, shp(w_a, BF16), shp_h, shp(w_a, BF16), shp(w_b, BF16)],
        compiler_params=_params(("parallel",), 48),
        name="mix_in_sgu",
    )(x, w, lng, lnb, tw, sb)


def _mix_in_decode_kernel(x_ref, w_ref, lng_ref, lnb_ref, w00_ref, b0_ref,
                          q_ref, k_ref, v_ref, ob_ref, vn_ref, *, w_a, w_b, qscale):
    xb = x_ref[...].astype(BF16)
    q_ref[...] = _dot(xb, w_ref[:, 0:w_a]) * qscale
    k_ref[...] = _dot(xb, w_ref[:, w_a:2 * w_a])
    v_ref[...] = _dot(xb, w_ref[:, 2 * w_a:3 * w_a])
    gu = _gelu(_dot(xb, w_ref[:, 3 * w_a:3 * w_a + w_b]))
    gv = _gelu(_dot(xb, w_ref[:, 3 * w_a + w_b:3 * w_a + 2 * w_b]))
    vn = _layer_norm(gv, lng_ref[...], lnb_ref[...])
    vn_ref[...] = vn
    ob_ref[...] = (gu * (w00_ref[...] * vn + b0_ref[...])).astype(BF16)


def _mix_in_decode(x, w, lng, lnb, w00, b0, *, w_a, w_b, qscale):
    n, d = x.shape
    kern = functools.partial(_mix_in_decode_kernel, w_a=w_a, w_b=w_b, qscale=qscale)
    full = lambda shape: pl.BlockSpec(shape, lambda i: (0,) * len(shape))
    return pl.pallas_call(
        kern,
        grid=(1,),
        in_specs=[full(x.shape), full(w.shape), full(lng.shape), full(lnb.shape),
                  full(w00.shape), full(b0.shape)],
        out_specs=[full((n, w_a)), full((n, w_a)), full((n, w_a)), full((n, w_b)), full((n, w_b))],
        out_shape=[jax.ShapeDtypeStruct((n, w_a), F32), jax.ShapeDtypeStruct((n, w_a), F32),
                   jax.ShapeDtypeStruct((n, w_a), F32), jax.ShapeDtypeStruct((n, w_b), BF16),
                   jax.ShapeDtypeStruct((n, w_b), F32)],
        compiler_params=_params(("arbitrary",), 32),
        name="mix_in_decode",
    )(x, w, lng, lnb, w00, b0)


def _attn_kernel(lam_ref, g_ref, q_ref, k_ref, v_ref, b0_ref, b1_ref, o_ref,
                 qq_ref, m_ref, acc_ref, *, t, n_heads, lambda_init):
    qi = pl.program_id(1)
    lane = lax.broadcasted_iota(jnp.int32, (t, E_A), 1)
    for h in range(n_heads):
        q = q_ref[:, h * E_A:(h + 1) * E_A]
        zero = jnp.zeros_like(q)
        qq_ref[h, 0:t, :] = jnp.where(lane < DH_A, q, zero)
        qq_ref[h, t:2 * t, :] = jnp.where(lane < DH_A, zero, q)
    m_ref[...] = jnp.full_like(m_ref, NEG_INF)
    acc_ref[...] = jnp.zeros_like(acc_ref)
    ones = jnp.ones((t, LANES), BF16)
    n_blk = t // LANES

    def step(j, bias_ref, causal):
        off = pl.multiple_of(j * t, t)
        for h in range(n_heads):
            cols = slice(h * E_A, (h + 1) * E_A)
            s = _dot_nt(qq_ref[h], k_ref[pl.ds(off, t), cols])
            if bias_ref is not None:
                b = bias_ref[h]
                s = s + jnp.concatenate([b, b], axis=0)
            if causal:
                r = lax.broadcasted_iota(jnp.int32, (2 * t, t), 0)
                c = lax.broadcasted_iota(jnp.int32, (2 * t, t), 1)
                s = jnp.where(jnp.where(r >= t, r - t, r) >= c, s, NEG_INF)
            blocks = [s[:, i * LANES:(i + 1) * LANES] for i in range(n_blk)]
            lane_max = functools.reduce(jnp.maximum, blocks)
            row_max = jnp.max(lane_max, axis=-1, keepdims=True)
            m_old = m_ref[h]
            m_new = jnp.maximum(m_old, jnp.broadcast_to(row_max, m_old.shape))
            alpha = jnp.exp2(m_old - m_new)
            p = jnp.concatenate([jnp.exp2(blk - m_new).astype(BF16) for blk in blocks], axis=1)
            va = jnp.concatenate([v_ref[pl.ds(off, t), cols], ones], axis=1)
            acc_ref[h] = jnp.concatenate([alpha, alpha], axis=1) * acc_ref[h] + _dot(p, va)
            m_ref[h] = m_new

    def far(j, carry):
        step(j, None, False)
        return carry

    lax.fori_loop(0, jnp.maximum(qi - 1, 0), far, 0)

    @pl.when(qi >= 1)
    def _():
        step(qi - 1, b1_ref, False)

    step(qi, b0_ref, True)

    lam = _diff_lambda(lam_ref, lambda_init)
    for h in range(n_heads):
        acc = acc_ref[h]
        o = acc[:, 0:E_A] / acc[:, E_A:2 * E_A]
        o = o[0:t] - lam * o[t:2 * t]
        o_ref[:, h * E_A:(h + 1) * E_A] = _head_rmsnorm(o, g_ref[...], lambda_init).astype(BF16)


def _attention(lam4, subln_g, q, k, v, b0, b1, *, t, lambda_init):
    bsz, s, w_a = q.shape
    n_heads = w_a // E_A
    kern = functools.partial(_attn_kernel, t=t, n_heads=n_heads, lambda_init=lambda_init)
    const = lambda shape: pl.BlockSpec(shape, lambda b, i: (0,) * len(shape))
    seq_spec = pl.BlockSpec((None, s, w_a), lambda b, i: (b, 0, 0), pipeline_mode=pl.Buffered(1))
    return pl.pallas_call(
        kern,
        grid=(bsz, s // t),
        in_specs=[
            const(lam4.shape), const(subln_g.shape),
            pl.BlockSpec((None, t, w_a), lambda b, i: (b, i, 0)),
            seq_spec, seq_spec,
            _resident(b0.shape), _resident(b1.shape),
        ],
        out_specs=pl.BlockSpec((None, t, w_a), lambda b, i: (b, i, 0)),
        out_shape=jax.ShapeDtypeStruct((bsz, s, w_a), BF16),
        scratch_shapes=[pltpu.VMEM((n_heads, 2 * t, E_A), BF16),
                        pltpu.VMEM((n_heads, 2 * t, LANES), F32),
                        pltpu.VMEM((n_heads, 2 * t, E_A + LANES), F32)],
        compiler_params=_params(("parallel", "arbitrary"), 52),
        name="diff_attn_prompt",
    )(lam4, subln_g, q, k, v, b0, b1)


def _decode_attn_kernel(pt_ref, lam_ref, g_ref, q_ref, kn_ref, vn_ref, bias_ref, new_bias_ref, *rest,
                        pages, n_heads, lambda_init):
    del pt_ref
    k_refs = rest[:pages]
    v_refs = rest[pages:2 * pages]
    o_ref, kb_ref, vb_ref, m_ref, l_ref, acc_ref = rest[2 * pages:]
    j = pl.program_id(1)
    last = pl.num_programs(1) - 1
    rows = 2 * n_heads
    page_len = k_refs[0].shape[0]

    @pl.when(j == 0)
    def _():
        m_ref[...] = jnp.full_like(m_ref, NEG_INF)
        l_ref[...] = jnp.zeros_like(l_ref)
        acc_ref[...] = jnp.zeros_like(acc_ref)

    r_id = lax.broadcasted_iota(jnp.int32, (rows, E_A), 0)
    l_id = lax.broadcasted_iota(jnp.int32, (rows, E_A), 1)

    def per_row_head(tok_ref):
        out = jnp.zeros((rows, E_A), F32)
        for h in range(n_heads):
            piece = jnp.broadcast_to(tok_ref[:, h * E_A:(h + 1) * E_A], (rows, E_A))
            out = jnp.where((r_id // 2) == h, piece, out)
        return out

    qm = jnp.where((l_id // DH_A) == (r_id % 2), per_row_head(q_ref), 0.0)

    for i in range(pages):
        kb_ref[i * page_len:(i + 1) * page_len, :] = k_refs[i][...].astype(BF16)
        vb_ref[i * page_len:(i + 1) * page_len, :] = v_refs[i][...].astype(BF16)

    n_keys = pages * page_len
    s = _dot_nt(qm.astype(BF16), kb_ref[...])
    near = jnp.where(j == last, 1.0, 0.0)
    s_tail = s[:, n_keys - page_len:] + near * bias_ref[...]
    s = jnp.concatenate([s[:, :n_keys - page_len], s_tail], axis=1) if pages > 1 else s_tail
    row_s = lax.broadcasted_iota(jnp.int32, s.shape, 0)
    col_s = lax.broadcasted_iota(jnp.int32, s.shape, 1)
    s = jnp.where((col_s % n_heads) == (row_s // 2), s, NEG_INF)

    m_old = m_ref[...]
    m_new = jnp.maximum(m_old, jnp.max(s, axis=-1, keepdims=True))
    alpha = jnp.exp2(m_old - m_new)
    p = jnp.exp2(s - m_new)
    l_ref[...] = alpha * l_ref[...] + jnp.sum(p, axis=-1, keepdims=True)
    acc_ref[...] = alpha * acc_ref[...] + _dot(p.astype(BF16), vb_ref[...])
    m_ref[...] = m_new

    @pl.when(j == last)
    def _():
        s_new = (jnp.sum(qm * per_row_head(kn_ref), axis=-1, keepdims=True) + new_bias_ref[...])
        m_old = m_ref[...]
        m_new = jnp.maximum(m_old, s_new)
        alpha = jnp.exp2(m_old - m_new)
        p_new = jnp.exp2(s_new - m_new)
        l_fin = alpha * l_ref[...] + p_new
        o = (alpha * acc_ref[...] + p_new * per_row_head(vn_ref)) / l_fin
        lam = _diff_lambda(lam_ref, lambda_init)
        g = g_ref[...]
        for h in range(n_heads):
            oh = o[2 * h:2 * h + 1] - lam * o[2 * h + 1:2 * h + 2]
            o_ref[:, h * E_A:(h + 1) * E_A] = _head_rmsnorm(oh, g, lambda_init).astype(BF16)


def _decode_attention(page_table, lam4, subln_g, q, k_new, v_new, bias, new_bias, cache_k, cache_v, *,
                      layer, pages, lambda_init):
    n_seq, _, w_a = q.shape
    n_pages = page_table.shape[1]
    n_heads = w_a // E_A
    rows = 2 * n_heads
    page_len = cache_k.shape[2]
    kern = functools.partial(_decode_attn_kernel, pages=pages, n_heads=n_heads,
                             lambda_init=lambda_init)
    const = lambda shape: pl.BlockSpec(shape, lambda b, j, pt: (0,) * len(shape))
    tok = pl.BlockSpec((None, 1, w_a), lambda b, j, pt: (b, 0, 0))

    def page_spec(i):
        return pl.BlockSpec((None, None, page_len, E_A),
                            lambda b, j, pt: (layer, pt[b * n_pages + j * pages + i], 0, 0))

    grid_spec = pltpu.PrefetchScalarGridSpec(
        num_scalar_prefetch=1,
        grid=(n_seq, n_pages // pages),
        in_specs=[const(lam4.shape), const(subln_g.shape), tok, tok, tok, const(bias.shape),
                  const(new_bias.shape)]
        + [page_spec(i) for i in range(pages)] + [page_spec(i) for i in range(pages)],
        out_specs=pl.BlockSpec((None, 1, w_a), lambda b, j, pt: (b, 0, 0)),
        scratch_shapes=[pltpu.VMEM((pages * page_len, E_A), BF16),
                        pltpu.VMEM((pages * page_len, E_A), BF16),
                        pltpu.VMEM((rows, 1), F32), pltpu.VMEM((rows, 1), F32),
                        pltpu.VMEM((rows, E_A), F32)],
    )
    return pl.pallas_call(
        kern,
        grid_spec=grid_spec,
        out_shape=jax.ShapeDtypeStruct((n_seq, 1, w_a), BF16),
        compiler_params=_params(("parallel", "arbitrary"), 40),
        name="diff_attn_decode",
    )(page_table.reshape(-1), lam4, subln_g, q, k_new, v_new, bias, new_bias,
      *([cache_k] * pages), *([cache_v] * pages))


def _mix_out_kernel(x_ref, oa_ref, ob_ref, woa_ref, wob_ref, g_ref, b_ref, wq_ref,
                    x2_ref, qx_ref, *, alpha, qscale):
    y = _dot(oa_ref[...], woa_ref[...]) + _dot(ob_ref[...], wob_ref[...])
    x2 = _layer_norm(alpha * x_ref[...] + y, g_ref[...], b_ref[...])
    x2_ref[...] = x2
    qx_ref[...] = (_dot(x2.astype(BF16), wq_ref[...]) * qscale).astype(BF16)


def _mix_out(x, oa, ob, woa, wob, g, b, wq, *, alpha, qscale, tm):
    n, d = x.shape
    kern = functools.partial(_mix_out_kernel, alpha=alpha, qscale=qscale)
    row = lambda width: pl.BlockSpec((tm, width), lambda i: (i, 0))
    return pl.pallas_call(
        kern,
        grid=(n // tm,),
        in_specs=[row(d), row(oa.shape[1]), row(ob.shape[1]), _resident(woa.shape),
                  _resident(wob.shape), _resident(g.shape), _resident(b.shape), _resident(wq.shape)],
        out_specs=[row(d), row(d)],
        out_shape=[jax.ShapeDtypeStruct((n, d), F32), jax.ShapeDtypeStruct((n, d), BF16)],
        compiler_params=_params(("parallel",), 40),
        name="mix_out_ln_q",
    )(x, oa, ob, woa, wob, g, b, wq)


def _post_mix_kernel(x_ref, oa_ref, ob_ref, mk_ref, mv_ref, woa_ref, wob_ref, g1_ref, b1_ref,
                     wq_ref, wo_ref, g2_ref, b2_ref, y_ref, o_scr, *, alpha, qscale, n_heads):
    y = _dot(oa_ref[...], woa_ref[...]) + _dot(ob_ref[...], wob_ref[...])
    x2 = _layer_norm(alpha * x_ref[...] + y, g1_ref[...], b1_ref[...])
    qx = (_dot(x2.astype(BF16), wq_ref[...]) * qscale).astype(BF16)
    dh = qx.shape[1] // n_heads
    for h in range(n_heads):
        cols = slice(h * dh, (h + 1) * dh)
        s = _dot_nt(qx[:, cols], mk_ref[:, cols].astype(BF16))
        p = jnp.exp2(s - jnp.max(s, axis=-1, keepdims=True))
        l = jnp.sum(p, axis=-1, keepdims=True)
        o_scr[:, cols] = (_dot(p.astype(BF16), mv_ref[:, cols].astype(BF16)) / l).astype(BF16)
    y2 = _dot(o_scr[...], wo_ref[...])
    y_ref[...] = _layer_norm(alpha * x2 + y2, g2_ref[...], b2_ref[...])


def _post_mix(x, oa, ob, mk, mv, woa, wob, g1, b1, wq, wo, g2, b2, *, alpha, qscale, n_heads, tm):
    bsz, s, d = x.shape
    n_mem = mk.shape[1]
    kern = functools.partial(_post_mix_kernel, alpha=alpha, qscale=qscale, n_heads=n_heads)
    row = lambda width: pl.BlockSpec((None, tm, width), lambda b, i: (b, i, 0))
    mem = pl.BlockSpec((None, n_mem, d), lambda b, i: (b, 0, 0))
    weights = [woa, wob, g1, b1, wq, wo, g2, b2]
    return pl.pallas_call(
        kern,
        grid=(bsz, s // tm),
        in_specs=[row(d), row(oa.shape[2]), row(ob.shape[2]), mem, mem]
        + [_resident(w.shape) for w in weights],
        out_specs=row(d),
        out_shape=jax.ShapeDtypeStruct((bsz, s, d), F32),
        scratch_shapes=[pltpu.VMEM((tm, d), BF16)],
        compiler_params=_params(("parallel", "parallel"), 48),
        name="post_mix",
    )(x, oa, ob, mk, mv, *weights)


def _cross_decode_kernel(q_ref, mk_ref, mv_ref, o_ref, *, n_heads):
    dh = mk_ref.shape[1]
    sublanes = 8
    q = q_ref[...].astype(F32)
    r_id = lax.broadcasted_iota(jnp.int32, (sublanes, dh), 0)
    qm = jnp.zeros((sublanes, dh), F32)
    for h in range(n_heads):
        piece = jnp.broadcast_to(q[:, h * dh:(h + 1) * dh], (sublanes, dh))
        qm = jnp.where((r_id % n_heads) == h, piece, qm)
    s = _dot_nt(qm.astype(BF16), mk_ref[...].astype(BF16))
    row_s = lax.broadcasted_iota(jnp.int32, s.shape, 0)
    col_s = lax.broadcasted_iota(jnp.int32, s.shape, 1)
    s = jnp.where((col_s % n_heads) == (row_s % n_heads), s, NEG_INF)
    p = jnp.exp2(s - jnp.max(s, axis=-1, keepdims=True))
    l = jnp.sum(p, axis=-1, keepdims=True)
    o = _dot(p.astype(BF16), mv_ref[...].astype(BF16)) / l
    for h in range(n_heads):
        o_ref[:, h * dh:(h + 1) * dh] = o[h:h + 1].astype(BF16)


def _cross_attention_decode(qx, mk, mv, *, layer, n_heads):
    n_seq, _, d = qx.shape
    mem_len, dh = mk.shape[2:]
    kern = functools.partial(_cross_decode_kernel, n_heads=n_heads)
    mem_spec = pl.BlockSpec((None, None, mem_len, dh), lambda b: (layer, b, 0, 0))
    return pl.pallas_call(
        kern,
        grid=(n_seq,),
        in_specs=[pl.BlockSpec((None, 1, d), lambda b: (b, 0, 0)), mem_spec, mem_spec],
        out_specs=pl.BlockSpec((None, 1, d), lambda b: (b, 0, 0)),
        out_shape=jax.ShapeDtypeStruct((n_seq, 1, d), BF16),
        compiler_params=_params(("parallel",), 32),
        name="cross_attn_decode",
    )(qx, mk, mv)


def _proj_ln_kernel(x_ref, o_ref, w_ref, g_ref, b_ref, y_ref, *, alpha):
    y = _dot(o_ref[...], w_ref[...])
    y_ref[...] = _layer_norm(alpha * x_ref[...] + y, g_ref[...], b_ref[...])


def _proj_ln(x, o, w, g, b, *, alpha, tm):
    n, d = x.shape
    kern = functools.partial(_proj_ln_kernel, alpha=alpha)
    row = pl.BlockSpec((tm, d), lambda i: (i, 0))
    return pl.pallas_call(
        kern,
        grid=(n // tm,),
        in_specs=[row, row, _resident(w.shape), _resident(g.shape), _resident(b.shape)],
        out_specs=row,
        out_shape=jax.ShapeDtypeStruct((n, d), F32),
        compiler_params=_params(("parallel",), 32),
        name="proj_ln",
    )(x, o, w, g, b)


def _mem_kv_kernel(m_ref, w_ref, k_ref, v_ref):
    d = k_ref.shape[1]
    mb = m_ref[...].astype(BF16)
    k_ref[...] = _dot(mb, w_ref[:, 0:d])
    v_ref[...] = _dot(mb, w_ref[:, d:2 * d])


def _mem_kv(mem, w, *, tm):
    n, d = mem.shape
    row = pl.BlockSpec((tm, d), lambda i: (i, 0))
    return pl.pallas_call(
        _mem_kv_kernel,
        grid=(n // tm,),
        in_specs=[row, _resident(w.shape)],
        out_specs=[row, row],
        out_shape=[jax.ShapeDtypeStruct((n, d), F32)] * 2,
        compiler_params=_params(("parallel",), 32),
        name="mem_kv",
    )(mem, w)


def _shifted_bias(table, dist):
    n = jnp.maximum(dist, 0)
    max_exact = NB // 2
    nf = jnp.maximum(n, 1).astype(F32)
    large = max_exact + (jnp.log(nf / max_exact) / math.log(MAX_DIST / max_exact)
                         * (NB - max_exact)).astype(jnp.int32)
    large = jnp.minimum(large, NB - 1)
    bucket = jnp.where(n < max_exact, n, large)
    tab = (table.astype(F32) - table[NB - 1].astype(F32)) * LOG2E
    hit = bucket[..., None, None] == jnp.arange(NB)[:, None]
    return jnp.sum(jnp.where(hit, tab, 0.0), axis=-2)


def _split_ffn(w_in, w_out):
    d_ff = w_in.shape[1] // 2
    return w_in[:, :d_ff].astype(BF16), w_in[:, d_ff:].astype(BF16), w_out.astype(BF16)


def _toeplitz(f_pos, f_neg, t):
    period = 2 * t
    v = jnp.concatenate([f_neg, jnp.zeros_like(f_pos[:1]), f_pos[:0:-1]], axis=0)
    tiled = jnp.tile(v, (t, 1))[: t * (period - 1)]
    skew = tiled.reshape(t, period - 1, -1)
    return jnp.transpose(skew[:, :t], (2, 0, 1))


def kernel(x_prompt, x_sample, mem_prompt, cache_k, cache_v, cache_mem_k, cache_mem_v, page_table, rel_bias, ln_g, ln_b, ffn1_w_in, ffn1_w_out, w_mix_in, w_mix_out, lambda_q1, lambda_k1, lambda_q2, lambda_k2, subln_g, sgu_ln_g, sgu_ln_b, sgu_w, sgu_b, xq_w, xkv_w, xo_w, ffn2_w_in, ffn2_w_out):
    bsz, seq, d = x_prompt.shape
    n_dec = x_sample.shape[0]
    depth = ln_g.shape[0]
    assert depth == 1 and x_sample.shape[1] == 1
    w_b = sgu_ln_g.shape[1]
    w_a = (w_mix_in.shape[2] - 2 * w_b) // 3
    n_heads = w_a // E_A
    n_groups = sgu_w.shape[1]
    n_mem, h_m, dh_m = cache_mem_k.shape[2:]
    page_rows = cache_k.shape[2]
    alpha = (2 * depth) ** 0.25
    lambda_init = 0.8 - 0.6 * math.exp(-0.3 * 0)
    l = 0
    t_attn = 512
    tm = 512
    assert seq % tm == 0 and seq % t_attn == 0 and t_attn >= MAX_DIST and tm % CHUNK == 0
    assert page_rows == CHUNK and MAX_DIST <= page_rows

    row2 = lambda a: a.reshape(1, -1)
    g_ln = [row2(ln_g[l, i]) for i in range(4)]
    b_ln = [row2(ln_b[l, i]) for i in range(4)]
    ffn1 = _split_ffn(ffn1_w_in[l], ffn1_w_out[l])
    ffn2 = _split_ffn(ffn2_w_in[l], ffn2_w_out[l])
    q_fold = jnp.concatenate([jnp.full((w_a,), DH_A ** -0.5, F32),
                              jnp.ones((w_mix_in.shape[2] - w_a,), F32)])
    w_mix = (w_mix_in[l] * q_fold).astype(BF16)
    woa = w_mix_out[l, :w_a].astype(BF16)
    wob = w_mix_out[l, w_a:].astype(BF16)
    wq_x = xq_w[l].astype(BF16)
    wo_x = xo_w[l].astype(BF16)
    wkv = xkv_w[l].astype(BF16)
    lam4 = jnp.stack([lambda_q1[l], lambda_k1[l], lambda_q2[l], lambda_k2[l]]).astype(F32)
    g_sub = row2(subln_g[l])
    lng_s, lnb_s = row2(sgu_ln_g[l]), row2(sgu_ln_b[l])
    tril_w = jnp.tril(sgu_w[l]).astype(BF16)
    sgu_bias = sgu_b[l].reshape(n_groups, CHUNK, 1)
    cg = w_b // n_groups
    w00 = jnp.repeat(sgu_w[l, :, 0, 0], cg).reshape(1, w_b)
    b00 = jnp.repeat(sgu_b[l, :, 0], cg).reshape(1, w_b)
    xq_scale = dh_m ** -0.5 * LOG2E

    ar = jnp.arange(t_attn)
    bias0 = _toeplitz(_shifted_bias(rel_bias, ar), _shifted_bias(rel_bias, 0 * ar), t_attn)
    bias1 = _toeplitz(_shifted_bias(rel_bias, t_attn + ar), _shifted_bias(rel_bias, t_attn - ar),
                      t_attn)
    dec_near = _shifted_bias(rel_bias, page_rows - jnp.arange(page_rows))
    dec_bias = jnp.repeat(jnp.repeat(dec_near.T, 2, axis=0), n_heads, axis=1)
    new_bias = jnp.repeat(_shifted_bias(rel_bias, jnp.zeros((1,), jnp.int32)).T, 2, axis=0)

    n_tok = bsz * seq
    xp = x_prompt.reshape(n_tok, d)
    mk_p, mv_p = _mem_kv(mem_prompt.reshape(bsz * n_mem, d), wkv, tm=n_mem)
    x1 = _ffn_ln(xp, *ffn1, g_ln[0], b_ln[0], alpha=alpha, tm=tm)
    qb, k_p, kb, v_p, vb, ob = _mix_in(x1, w_mix, lng_s, lnb_s, tril_w, sgu_bias,
                                       w_a=w_a, w_b=w_b, qscale=LOG2E, tm=tm)
    oa = _attention(lam4, g_sub, qb.reshape(bsz, seq, w_a), kb.reshape(bsz, seq, w_a),
                    vb.reshape(bsz, seq, w_a), bias0, bias1, t=t_attn, lambda_init=lambda_init)
    x3 = _post_mix(x1.reshape(bsz, seq, d), oa, ob.reshape(bsz, seq, w_b),
                   mk_p.reshape(bsz, n_mem, d), mv_p.reshape(bsz, n_mem, d),
                   woa, wob, g_ln[1], b_ln[1], wq_x, wo_x, g_ln[2], b_ln[2],
                   alpha=alpha, qscale=xq_scale, n_heads=h_m, tm=tm)
    y_p = _ffn_ln(x3.reshape(n_tok, d), *ffn2, g_ln[3], b_ln[3], alpha=alpha, tm=tm)

    xs = x_sample.reshape(n_dec, d)
    s1 = _ffn_ln(xs, *ffn1, g_ln[0], b_ln[0], alpha=alpha, tm=n_dec)
    q_s, k_s, v_s, ob_s, vn_s = _mix_in_decode(s1, w_mix, lng_s, lnb_s, w00, b00,
                                               w_a=w_a, w_b=w_b, qscale=LOG2E)
    oa_s = _decode_attention(page_table, lam4, g_sub, q_s.reshape(n_dec, 1, w_a),
                             k_s.reshape(n_dec, 1, w_a), v_s.reshape(n_dec, 1, w_a), dec_bias, new_bias,
                             cache_k.reshape(depth, -1, page_rows * n_heads, E_A),
                             cache_v.reshape(depth, -1, page_rows * n_heads, E_A),
                             layer=l, pages=16, lambda_init=lambda_init)
    s2, qx_s = _mix_out(s1, oa_s.reshape(n_dec, w_a), ob_s, woa, wob, g_ln[1], b_ln[1], wq_x,
                        alpha=alpha, qscale=xq_scale, tm=n_dec)
    ox_s = _cross_attention_decode(qx_s.reshape(n_dec, 1, d),
                                   cache_mem_k.reshape(depth, n_dec, n_mem * h_m, dh_m),
                                   cache_mem_v.reshape(depth, n_dec, n_mem * h_m, dh_m),
                                   layer=l, n_heads=h_m)
    s3 = _proj_ln(s2, ox_s.reshape(n_dec, d), wo_x, g_ln[2], b_ln[2], alpha=alpha, tm=n_dec)
    y_s = _ffn_ln(s3, *ffn2, g_ln[3], b_ln[3], alpha=alpha, tm=n_dec)

    return (y_p.reshape(bsz, seq, d), y_s.reshape(n_dec, 1, d),
            k_p.reshape(1, bsz, seq, n_heads, E_A), v_p.reshape(1, bsz, seq, n_heads, E_A),
            mk_p.reshape(1, bsz, n_mem, h_m, dh_m), mv_p.reshape(1, bsz, n_mem, h_m, dh_m),
            k_s.reshape(1, n_dec, 1, n_heads, E_A), v_s.reshape(1, n_dec, 1, n_heads, E_A),
            vn_s.reshape(1, n_dec, 1, w_b))
```

```python
import functools
import math

import jax
import jax.numpy as jnp
from jax import lax
from jax.experimental import pallas as pl
from jax.experimental.pallas import tpu as pltpu

F32 = jnp.float32
BF16 = jnp.bfloat16

LN_EPS = 1e-5
NEG_INF = -1e30
LOG2E = 1.4426950408889634
DH_A = 64
E_A = 2 * DH_A
CHUNK = 128
NB = 32
MAX_DIST = 128

LANES = 128
MXU_EDGE = 256
MIB = 1024 * 1024
SUB_ROWS = 256


def _params(semantics, vmem_mib):
    return pltpu.CompilerParams(dimension_semantics=semantics, vmem_limit_bytes=vmem_mib * MIB)


def _resident(shape):
    nd = len(shape)
    return pl.BlockSpec(shape, lambda *_: (0,) * nd, pipeline_mode=pl.Buffered(1))


def _layer_norm(x, g, b):
    mu = jnp.mean(x, -1, keepdims=True)
    xc = x - mu
    var = jnp.mean(xc * xc, -1, keepdims=True)
    return xc * lax.rsqrt(var + LN_EPS) * g + b


def _gelu(x):
    return 0.5 * x * (1.0 + lax.erf(x * math.sqrt(0.5)))


def _dot(a, b):
    return jnp.dot(a, b, preferred_element_type=F32)


def _dot_nt(a, b):
    return lax.dot_general(a, b, (((1,), (1,)), ((), ())), preferred_element_type=F32)


def _diff_lambda(lam_ref, lambda_init):
    lv = lam_ref[...]
    a = jnp.sum(lv[0:1] * lv[1:2], axis=-1, keepdims=True)
    b = jnp.sum(lv[2:3] * lv[3:4], axis=-1, keepdims=True)
    return jnp.exp(a) - jnp.exp(b) + lambda_init


def _head_rmsnorm(o, g, lambda_init):
    return o * lax.rsqrt(jnp.mean(o * o, -1, keepdims=True) + LN_EPS) * g * (1.0 - lambda_init)


def _ffn_ln_kernel(x_ref, wa_ref, wb_ref, wo_ref, g_ref, b_ref, o_ref, act_ref, *, alpha, fc):
    tm = x_ref.shape[0]
    sub = min(tm, SUB_ROWS)
    subs = [slice(r0, r0 + sub) for r0 in range(0, tm, sub)]
    xbs = [x_ref[rs, :].astype(BF16) for rs in subs]
    d_ff = wa_ref.shape[1]
    for j in range(d_ff // fc):
        cols = slice(j * fc, (j + 1) * fc)
        for rs, xb in zip(subs, xbs):
            ha = _dot(xb, wa_ref[:, cols])
            hb = _dot(xb, wb_ref[:, cols])
            act_ref[rs, cols] = (ha * jax.nn.sigmoid(ha) * hb).astype(BF16)
    for rs in subs:
        y = _dot(act_ref[rs, :], wo_ref[...])
        o_ref[rs, :] = _layer_norm(alpha * x_ref[rs, :] + 0.5 * y, g_ref[...], b_ref[...])


def _ffn_ln(x, wa, wb, wo, g, b, *, alpha, tm):
    n, d = x.shape
    d_ff = wa.shape[1]
    assert d_ff % MXU_EDGE == 0
    kern = functools.partial(_ffn_ln_kernel, alpha=alpha, fc=MXU_EDGE)
    return pl.pallas_call(
        kern,
        grid=(n // tm,),
        in_specs=[
            pl.BlockSpec((tm, d), lambda i: (i, 0)),
            _resident(wa.shape), _resident(wb.shape), _resident(wo.shape),
            _resident(g.shape), _resident(b.shape),
        ],
        out_specs=pl.BlockSpec((tm, d), lambda i: (i, 0)),
        out_shape=jax.ShapeDtypeStruct((n, d), F32),
        scratch_shapes=[pltpu.VMEM((tm, d_ff), BF16)],
        compiler_params=_params(("parallel",), 48),
        name="ffn_ln",
    )(x, wa, wb, wo, g, b)


def _mix_in_kernel(x_ref, w_ref, lng_ref, lnb_ref, tw_ref, sb_ref,
                   q_ref, k_ref, kb_ref, v_ref, vb_ref, ob_ref, *, w_a, w_b, qscale):
    tm = x_ref.shape[0]
    n_groups = tw_ref.shape[0]
    cg = w_b // n_groups
    for r0 in range(0, tm, SUB_ROWS):
        rs = slice(r0, r0 + SUB_ROWS)
        xb = x_ref[rs, :].astype(BF16)
        gv = _gelu(_dot(xb, w_ref[:, 3 * w_a + w_b:3 * w_a + 2 * w_b]))
        gu = _gelu(_dot(xb, w_ref[:, 3 * w_a:3 * w_a + w_b]))
        vn = _layer_norm(gv, lng_ref[...], lnb_ref[...]).astype(BF16)
        q_ref[rs, :] = (_dot(xb, w_ref[:, 0:w_a]) * qscale).astype(BF16)
        hk = _dot(xb, w_ref[:, w_a:2 * w_a])
        kb_ref[rs, :] = hk.astype(BF16)
        hv = _dot(xb, w_ref[:, 2 * w_a:3 * w_a])
        vb_ref[rs, :] = hv.astype(BF16)
        for h in range(w_a // E_A):
            k_ref[rs, h, :] = hk[:, h * E_A:(h + 1) * E_A]
            v_ref[rs, h, :] = hv[:, h * E_A:(h + 1) * E_A]
        for c in range(SUB_ROWS // CHUNK):
            rows = slice(c * CHUNK, (c + 1) * CHUNK)
            out_rows = slice(r0 + c * CHUNK, r0 + (c + 1) * CHUNK)
            for g in range(n_groups):
                cols = slice(g * cg, (g + 1) * cg)
                mixed = _dot(tw_ref[g], vn[rows, cols]) + sb_ref[g]
                ob_ref[out_rows, cols] = (gu[rows, cols] * mixed).astype(BF16)


def _mix_in(x, w, lng, lnb, tw, sb, *, w_a, w_b, qscale, tm):
    n, d = x.shape
    kern = functools.partial(_mix_in_kernel, w_a=w_a, w_b=w_b, qscale=qscale)
    row = lambda width: pl.BlockSpec((tm, width), lambda i: (i, 0))
    shp = lambda width, dt: jax.ShapeDtypeStruct((n, width), dt)
    n_heads = w_a // E_A
    row_h = pl.BlockSpec((tm, n_heads, E_A), lambda i: (i, 0, 0))
    shp_h = jax.ShapeDtypeStruct((n, n_heads, E_A), F32)
    return pl.pallas_call(
        kern,
        grid=(n // tm,),
        in_specs=[row(d), _resident(w.shape), _resident(lng.shape), _resident(lnb.shape),
                  _resident(tw.shape), _resident(sb.shape)],
        out_specs=[row(w_a), row_h, row(w_a), row_h, row(w_a), row(w_b)],
        out_shape=[shp(w_a, BF16), shp_h, shp(w_a, BF16), shp_h, shp(w_a, BF16), shp(w_b, BF16)],
        compiler_params=_params(("parallel",), 48),
        name="mix_in_sgu",
    )(x, w, lng, lnb, tw, sb)


def _mix_in_decode_kernel(x_ref, w_ref, lng_ref, lnb_ref, w00_ref, b0_ref,
                          q_ref, k_ref, v_ref, ob_ref, vn_ref, *, w_a, w_b, qscale):
    xb = x_ref[...].astype(BF16)
    q_ref[...] = _dot(xb, w_ref[:, 0:w_a]) * qscale
    k_ref[...] = _dot(xb, w_ref[:, w_a:2 * w_a])
    v_ref[...] = _dot(xb, w_ref[:, 2 * w_a:3 * w_a])
    gu = _gelu(_dot(xb, w_ref[:, 3 * w_a:3 * w_a + w_b]))
    gv = _gelu(_dot(xb, w_ref[:, 3 * w_a + w_b:3 * w_a + 2 * w_b]))
    vn = _layer_norm(gv, lng_ref[...], lnb_ref[...])
    vn_ref[...] = vn
    ob_ref[...] = (gu * (w00_ref[...] * vn + b0_ref[...])).astype(BF16)


def _mix_in_decode(x, w, lng, lnb, w00, b0, *, w_a, w_b, qscale):
    n, d = x.shape
    kern = functools.partial(_mix_in_decode_kernel, w_a=w_a, w_b=w_b, qscale=qscale)
    full = lambda shape: pl.BlockSpec(shape, lambda i: (0,) * len(shape))
    return pl.pallas_call(
        kern,
        grid=(1,),
        in_specs=[full(x.shape), full(w.shape), full(lng.shape), full(lnb.shape),
                  full(w00.shape), full(b0.shape)],
        out_specs=[full((n, w_a)), full((n, w_a)), full((n, w_a)), full((n, w_b)), full((n, w_b))],
        out_shape=[jax.ShapeDtypeStruct((n, w_a), F32), jax.ShapeDtypeStruct((n, w_a), F32),
                   jax.ShapeDtypeStruct((n, w_a), F32), jax.ShapeDtypeStruct((n, w_b), BF16),
                   jax.ShapeDtypeStruct((n, w_b), F32)],
        compiler_params=_params(("arbitrary",), 32),
        name="mix_in_decode",
    )(x, w, lng, lnb, w00, b0)


def _attn_kernel(lam_ref, g_ref, q_ref, k_ref, v_ref, b0_ref, b1_ref, o_ref,
                 qq_ref, m_ref, acc_ref, p_last, *, t, n_heads, lambda_init):
    qi = pl.program_id(1)
    lane = lax.broadcasted_iota(jnp.int32, (t, E_A), 1)
    for h in range(n_heads):
        q = q_ref[:, h * E_A:(h + 1) * E_A]
        zero = jnp.zeros_like(q)
        qq_ref[2 * h] = jnp.where(lane < DH_A, q, zero)
        qq_ref[2 * h + 1] = jnp.where(lane < DH_A, zero, q)
    m_ref[...] = jnp.full_like(m_ref, NEG_INF)
    acc_ref[...] = jnp.zeros_like(acc_ref)
    ones = jnp.ones((t, LANES), BF16)
    n_blk = t // LANES

    last_u = 2 * n_heads - 1
    last_cols = slice((n_heads - 1) * E_A, n_heads * E_A)
    p_last[...] = jnp.zeros_like(p_last)

    def flush_last(j_done):
        off = pl.multiple_of(j_done * t, t)
        va = jnp.concatenate([v_ref[pl.ds(off, t), last_cols], ones], axis=1)
        acc_ref[last_u] += _dot(p_last[...], va)

    def step(j, bias_ref, causal):
        flush_last(jnp.maximum(j - 1, 0))
        off = pl.multiple_of(j * t, t)
        if causal:
            visible = (lax.broadcasted_iota(jnp.int32, (t, t), 0)
                       >= lax.broadcasted_iota(jnp.int32, (t, t), 1))
        for h in range(n_heads):
            cols = slice(h * E_A, (h + 1) * E_A)
            kt = k_ref[pl.ds(off, t), cols]
            va = jnp.concatenate([v_ref[pl.ds(off, t), cols], ones], axis=1)
            for u in (2 * h, 2 * h + 1):
                s = _dot_nt(qq_ref[u], kt)
                if bias_ref is not None:
                    s = s + bias_ref[h]
                if causal:
                    s = jnp.where(visible, s, NEG_INF)
                blocks = [s[:, i * LANES:(i + 1) * LANES] for i in range(n_blk)]
                lane_max = functools.reduce(jnp.maximum, blocks)
                row_max = jnp.max(lane_max, axis=-1, keepdims=True)
                m_old = m_ref[u]
                m_new = jnp.maximum(m_old, jnp.broadcast_to(row_max, m_old.shape))
                alpha = jnp.exp2(m_old - m_new)
                p = jnp.concatenate([jnp.exp2(blk - m_new).astype(BF16) for blk in blocks], axis=1)
                alpha2 = jnp.concatenate([alpha, alpha], axis=1)
                if u == last_u:
                    acc_ref[u] = alpha2 * acc_ref[u]
                    p_last[...] = p
                else:
                    acc_ref[u] = alpha2 * acc_ref[u] + _dot(p, va)
                m_ref[u] = m_new

    def far(j, carry):
        step(j, None, False)
        return carry

    lax.fori_loop(0, jnp.maximum(qi - 1, 0), far, 0)

    @pl.when(qi >= 1)
    def _():
        step(qi - 1, b1_ref, False)

    step(qi, b0_ref, True)
    flush_last(qi)

    lam = _diff_lambda(lam_ref, lambda_init)
    for h in range(n_heads):
        a0 = acc_ref[2 * h]
        a1 = acc_ref[2 * h + 1]
        o = a0[:, 0:E_A] / a0[:, E_A:2 * E_A] - lam * (a1[:, 0:E_A] / a1[:, E_A:2 * E_A])
        o_ref[:, h * E_A:(h + 1) * E_A] = _head_rmsnorm(o, g_ref[...], lambda_init).astype(BF16)


def _attention(lam4, subln_g, q, k, v, b0, b1, *, t, lambda_init):
    bsz, s, w_a = q.shape
    n_heads = w_a // E_A
    kern = functools.partial(_attn_kernel, t=t, n_heads=n_heads, lambda_init=lambda_init)
    const = lambda shape: pl.BlockSpec(shape, lambda b, i: (0,) * len(shape))
    seq_spec = pl.BlockSpec((None, s, w_a), lambda b, i: (b, 0, 0), pipeline_mode=pl.Buffered(1))
    return pl.pallas_call(
        kern,
        grid=(bsz, s // t),
        in_specs=[
            const(lam4.shape), const(subln_g.shape),
            pl.BlockSpec((None, t, w_a), lambda b, i: (b, i, 0)),
            seq_spec, seq_spec,
            _resident(b0.shape), _resident(b1.shape),
        ],
        out_specs=pl.BlockSpec((None, t, w_a), lambda b, i: (b, i, 0)),
        out_shape=jax.ShapeDtypeStruct((bsz, s, w_a), BF16),
        scratch_shapes=[pltpu.VMEM((2 * n_heads, t, E_A), BF16),
                        pltpu.VMEM((2 * n_heads, t, LANES), F32),
                        pltpu.VMEM((2 * n_heads, t, E_A + LANES), F32),
                        pltpu.VMEM((t, t), BF16)],
        compiler_params=_params(("parallel", "arbitrary"), 52),
        name="diff_attn_prompt",
    )(lam4, subln_g, q, k, v, b0, b1)


def _decode_attn_kernel(pt_ref, lam_ref, g_ref, q_ref, kn_ref, vn_ref, bias_ref, new_bias_ref, *rest,
                        pages, n_heads, lambda_init):
    del pt_ref
    k_refs = rest[:pages]
    v_refs = rest[pages:2 * pages]
    o_ref, kb_ref, vb_ref, m_ref, l_ref, acc_ref = rest[2 * pages:]
    j = pl.program_id(1)
    last = pl.num_programs(1) - 1
    rows = 2 * n_heads
    page_len = k_refs[0].shape[0]

    @pl.when(j == 0)
    def _():
        m_ref[...] = jnp.full_like(m_ref, NEG_INF)
        l_ref[...] = jnp.zeros_like(l_ref)
        acc_ref[...] = jnp.zeros_like(acc_ref)

    r_id = lax.broadcasted_iota(jnp.int32, (rows, E_A), 0)
    l_id = lax.broadcasted_iota(jnp.int32, (rows, E_A), 1)

    def per_row_head(tok_ref):
        out = jnp.zeros((rows, E_A), F32)
        for h in range(n_heads):
            piece = jnp.broadcast_to(tok_ref[:, h * E_A:(h + 1) * E_A], (rows, E_A))
            out = jnp.where((r_id // 2) == h, piece, out)
        return out

    qm = jnp.where((l_id // DH_A) == (r_id % 2), per_row_head(q_ref), 0.0)

    for i in range(pages):
        kb_ref[i * page_len:(i + 1) * page_len, :] = k_refs[i][...].astype(BF16)
        vb_ref[i * page_len:(i + 1) * page_len, :] = v_refs[i][...].astype(BF16)

    n_keys = pages * page_len
    s = _dot_nt(qm.astype(BF16), kb_ref[...])
    near = jnp.where(j == last, 1.0, 0.0)
    s_tail = s[:, n_keys - page_len:] + near * bias_ref[...]
    s = jnp.concatenate([s[:, :n_keys - page_len], s_tail], axis=1) if pages > 1 else s_tail
    row_s = lax.broadcasted_iota(jnp.int32, s.shape, 0)
    col_s = lax.broadcasted_iota(jnp.int32, s.shape, 1)
    s = jnp.where((col_s % n_heads) == (row_s // 2), s, NEG_INF)

    m_old = m_ref[...]
    m_new = jnp.maximum(m_old, jnp.max(s, axis=-1, keepdims=True))
    alpha = jnp.exp2(m_old - m_new)
    p = jnp.exp2(s - m_new)
    l_ref[...] = alpha * l_ref[...] + jnp.sum(p, axis=-1, keepdims=True)
    acc_ref[...] = alpha * acc_ref[...] + _dot(p.astype(BF16), vb_ref[...])
    m_ref[...] = m_new

    @pl.when(j == last)
    def _():
        s_new = (jnp.sum(qm * per_row_head(kn_ref), axis=-1, keepdims=True) + new_bias_ref[...])
        m_old = m_ref[...]
        m_new = jnp.maximum(m_old, s_new)
        alpha = jnp.exp2(m_old - m_new)
        p_new = jnp.exp2(s_new - m_new)
        l_fin = alpha * l_ref[...] + p_new
        o = (alpha * acc_ref[...] + p_new * per_row_head(vn_ref)) / l_fin
        lam = _diff_lambda(lam_ref, lambda_init)
        g = g_ref[...]
        for h in range(n_heads):
            oh = o[2 * h:2 * h + 1] - lam * o[2 * h + 1:2 * h + 2]
            o_ref[:, h * E_A:(h + 1) * E_A] = _head_rmsnorm(oh, g, lambda_init).astype(BF16)


def _decode_attention(page_table, lam4, subln_g, q, k_new, v_new, bias, new_bias, cache_k, cache_v, *,
                      layer, pages, lambda_init):
    n_seq, _, w_a = q.shape
    n_pages = page_table.shape[1]
    n_heads = w_a // E_A
    rows = 2 * n_heads
    page_len = cache_k.shape[2]
    kern = functools.partial(_decode_attn_kernel, pages=pages, n_heads=n_heads,
                             lambda_init=lambda_init)
    const = lambda shape: pl.BlockSpec(shape, lambda b, j, pt: (0,) * len(shape))
    tok = pl.BlockSpec((None, 1, w_a), lambda b, j, pt: (b, 0, 0))

    def page_spec(i):
        return pl.BlockSpec((None, None, page_len, E_A),
                            lambda b, j, pt: (layer, pt[b * n_pages + j * pages + i], 0, 0))

    grid_spec = pltpu.PrefetchScalarGridSpec(
        num_scalar_prefetch=1,
        grid=(n_seq, n_pages // pages),
        in_specs=[const(lam4.shape), const(subln_g.shape), tok, tok, tok, const(bias.shape),
                  const(new_bias.shape)]
        + [page_spec(i) for i in range(pages)] + [page_spec(i) for i in range(pages)],
        out_specs=pl.BlockSpec((None, 1, w_a), lambda b, j, pt: (b, 0, 0)),
        scratch_shapes=[pltpu.VMEM((pages * page_len, E_A), BF16),
                        pltpu.VMEM((pages * page_len, E_A), BF16),
                        pltpu.VMEM((rows, 1), F32), pltpu.VMEM((rows, 1), F32),
                        pltpu.VMEM((rows, E_A), F32)],
    )
    return pl.pallas_call(
        kern,
        grid_spec=grid_spec,
        out_shape=jax.ShapeDtypeStruct((n_seq, 1, w_a), BF16),
        compiler_params=_params(("parallel", "arbitrary"), 40),
        name="diff_attn_decode",
    )(page_table.reshape(-1), lam4, subln_g, q, k_new, v_new, bias, new_bias,
      *([cache_k] * pages), *([cache_v] * pages))


def _mix_out_kernel(x_ref, oa_ref, ob_ref, woa_ref, wob_ref, g_ref, b_ref, wq_ref,
                    x2_ref, qx_ref, *, alpha, qscale):
    y = _dot(oa_ref[...], woa_ref[...]) + _dot(ob_ref[...], wob_ref[...])
    x2 = _layer_norm(alpha * x_ref[...] + y, g_ref[...], b_ref[...])
    x2_ref[...] = x2
    qx_ref[...] = (_dot(x2.astype(BF16), wq_ref[...]) * qscale).astype(BF16)


def _mix_out(x, oa, ob, woa, wob, g, b, wq, *, alpha, qscale, tm):
    n, d = x.shape
    kern = functools.partial(_mix_out_kernel, alpha=alpha, qscale=qscale)
    row = lambda width: pl.BlockSpec((tm, width), lambda i: (i, 0))
    return pl.pallas_call(
        kern,
        grid=(n // tm,),
        in_specs=[row(d), row(oa.shape[1]), row(ob.shape[1]), _resident(woa.shape),
                  _resident(wob.shape), _resident(g.shape), _resident(b.shape), _resident(wq.shape)],
        out_specs=[row(d), row(d)],
        out_shape=[jax.ShapeDtypeStruct((n, d), F32), jax.ShapeDtypeStruct((n, d), BF16)],
        compiler_params=_params(("parallel",), 40),
        name="mix_out_ln_q",
    )(x, oa, ob, woa, wob, g, b, wq)


def _post_mix_kernel(x_ref, oa_ref, ob_ref, mk_ref, mv_ref, woa_ref, wob_ref, g1_ref, b1_ref,
                     wq_ref, wo_ref, g2_ref, b2_ref, y_ref, o_scr, *, alpha, qscale, n_heads):
    tm, d = x_ref.shape
    dh = d // n_heads
    subs = [slice(r0, r0 + SUB_ROWS) for r0 in range(0, tm, SUB_ROWS)]
    x2 = []
    for rs in subs:
        y = _dot(oa_ref[rs, :], woa_ref[...]) + _dot(ob_ref[rs, :], wob_ref[...])
        x2.append(_layer_norm(alpha * x_ref[rs, :] + y, g1_ref[...], b1_ref[...]))
    qx = [(_dot(x.astype(BF16), wq_ref[...]) * qscale).astype(BF16) for x in x2]
    for h in range(n_heads):
        cols = slice(h * dh, (h + 1) * dh)
        mk = mk_ref[:, cols].astype(BF16)
        mv = mv_ref[:, cols].astype(BF16)
        for rs, q in zip(subs, qx):
            s = _dot_nt(q[:, cols], mk)
            p = jnp.exp2(s - jnp.max(s, axis=-1, keepdims=True))
            l = jnp.sum(p, axis=-1, keepdims=True)
            o_scr[rs, cols] = (_dot(p.astype(BF16), mv) / l).astype(BF16)
    for rs, x in zip(subs, x2):
        y2 = _dot(o_scr[rs, :], wo_ref[...])
        y_ref[rs, :] = _layer_norm(alpha * x + y2, g2_ref[...], b2_ref[...])


def _post_mix(x, oa, ob, mk, mv, woa, wob, g1, b1, wq, wo, g2, b2, *, alpha, qscale, n_heads, tm):
    bsz, s, d = x.shape
    n_mem = mk.shape[1]
    kern = functools.partial(_post_mix_kernel, alpha=alpha, qscale=qscale, n_heads=n_heads)
    row = lambda width: pl.BlockSpec((None, tm, width), lambda b, i: (b, i, 0))
    mem = pl.BlockSpec((None, n_mem, d), lambda b, i: (b, 0, 0))
    weights = [woa, wob, g1, b1, wq, wo, g2, b2]
    return pl.pallas_call(
        kern,
        grid=(bsz, s // tm),
        in_specs=[row(d), row(oa.shape[2]), row(ob.shape[2]), mem, mem]
        + [_resident(w.shape) for w in weights],
        out_specs=row(d),
        out_shape=jax.ShapeDtypeStruct((bsz, s, d), F32),
        scratch_shapes=[pltpu.VMEM((tm, d), BF16)],
        compiler_params=_params(("parallel", "parallel"), 48),
        name="post_mix",
    )(x, oa, ob, mk, mv, *weights)


def _cross_decode_kernel(q_ref, mk_ref, mv_ref, o_ref, *, n_heads):
    dh = mk_ref.shape[1]
    sublanes = 8
    q = q_ref[...].astype(F32)
    r_id = lax.broadcasted_iota(jnp.int32, (sublanes, dh), 0)
    qm = jnp.zeros((sublanes, dh), F32)
    for h in range(n_heads):
        piece = jnp.broadcast_to(q[:, h * dh:(h + 1) * dh], (sublanes, dh))
        qm = jnp.where((r_id % n_heads) == h, piece, qm)
    s = _dot_nt(qm.astype(BF16), mk_ref[...].astype(BF16))
    row_s = lax.broadcasted_iota(jnp.int32, s.shape, 0)
    col_s = lax.broadcasted_iota(jnp.int32, s.shape, 1)
    s = jnp.where((col_s % n_heads) == (row_s % n_heads), s, NEG_INF)
    p = jnp.exp2(s - jnp.max(s, axis=-1, keepdims=True))
    l = jnp.sum(p, axis=-1, keepdims=True)
    o = _dot(p.astype(BF16), mv_ref[...].astype(BF16)) / l
    for h in range(n_heads):
        o_ref[:, h * dh:(h + 1) * dh] = o[h:h + 1].astype(BF16)


def _cross_attention_decode(qx, mk, mv, *, layer, n_heads):
    n_seq, _, d = qx.shape
    mem_len, dh = mk.shape[2:]
    kern = functools.partial(_cross_decode_kernel, n_heads=n_heads)
    mem_spec = pl.BlockSpec((None, None, mem_len, dh), lambda b: (layer, b, 0, 0))
    return pl.pallas_call(
        kern,
        grid=(n_seq,),
        in_specs=[pl.BlockSpec((None, 1, d), lambda b: (b, 0, 0)), mem_spec, mem_spec],
        out_specs=pl.BlockSpec((None, 1, d), lambda b: (b, 0, 0)),
        out_shape=jax.ShapeDtypeStruct((n_seq, 1, d), BF16),
        compiler_params=_params(("parallel",), 32),
        name="cross_attn_decode",
    )(qx, mk, mv)


def _proj_ln_kernel(x_ref, o_ref, w_ref, g_ref, b_ref, y_ref, *, alpha):
    y = _dot(o_ref[...], w_ref[...])
    y_ref[...] = _layer_norm(alpha * x_ref[...] + y, g_ref[...], b_ref[...])


def _proj_ln(x, o, w, g, b, *, alpha, tm):
    n, d = x.shape
    kern = functools.partial(_proj_ln_kernel, alpha=alpha)
    row = pl.BlockSpec((tm, d), lambda i: (i, 0))
    return pl.pallas_call(
        kern,
        grid=(n // tm,),
        in_specs=[row, row, _resident(w.shape), _resident(g.shape), _resident(b.shape)],
        out_specs=row,
        out_shape=jax.ShapeDtypeStruct((n, d), F32),
        compiler_params=_params(("parallel",), 32),
        name="proj_ln",
    )(x, o, w, g, b)


def _mem_kv_kernel(m_ref, w_ref, k_ref, v_ref):
    d = k_ref.shape[1]
    mb = m_ref[...].astype(BF16)
    k_ref[...] = _dot(mb, w_ref[:, 0:d])
    v_ref[...] = _dot(mb, w_ref[:, d:2 * d])


def _mem_kv(mem, w, *, tm):
    n, d = mem.shape
    row = pl.BlockSpec((tm, d), lambda i: (i, 0))
    return pl.pallas_call(
        _mem_kv_kernel,
        grid=(n // tm,),
        in_specs=[row, _resident(w.shape)],
        out_specs=[row, row],
        out_shape=[jax.ShapeDtypeStruct((n, d), F32)] * 2,
        compiler_params=_params(("parallel",), 32),
        name="mem_kv",
    )(mem, w)


def _shifted_bias(table, dist):
    n = jnp.maximum(dist, 0)
    max_exact = NB // 2
    nf = jnp.maximum(n, 1).astype(F32)
    large = max_exact + (jnp.log(nf / max_exact) / math.log(MAX_DIST / max_exact)
                         * (NB - max_exact)).astype(jnp.int32)
    large = jnp.minimum(large, NB - 1)
    bucket = jnp.where(n < max_exact, n, large)
    tab = (table.astype(F32) - table[NB - 1].astype(F32)) * LOG2E
    hit = bucket[..., None, None] == jnp.arange(NB)[:, None]
    return jnp.sum(jnp.where(hit, tab, 0.0), axis=-2)


def _split_ffn(w_in, w_out):
    d_ff = w_in.shape[1] // 2
    return w_in[:, :d_ff].astype(BF16), w_in[:, d_ff:].astype(BF16), w_out.astype(BF16)


def _toeplitz(f_pos, f_neg, t):
    period = 2 * t
    v = jnp.concatenate([f_neg, jnp.zeros_like(f_pos[:1]), f_pos[:0:-1]], axis=0)
    tiled = jnp.tile(v, (t, 1))[: t * (period - 1)]
    skew = tiled.reshape(t, period - 1, -1)
    return jnp.transpose(skew[:, :t], (2, 0, 1))


def kernel(x_prompt, x_sample, mem_prompt, cache_k, cache_v, cache_mem_k, cache_mem_v, page_table, rel_bias, ln_g, ln_b, ffn1_w_in, ffn1_w_out, w_mix_in, w_mix_out, lambda_q1, lambda_k1, lambda_q2, lambda_k2, subln_g, sgu_ln_g, sgu_ln_b, sgu_w, sgu_b, xq_w, xkv_w, xo_w, ffn2_w_in, ffn2_w_out):
    bsz, seq, d = x_prompt.shape
    n_dec = x_sample.shape[0]
    depth = ln_g.shape[0]
    assert depth == 1 and x_sample.shape[1] == 1
    w_b = sgu_ln_g.shape[1]
    w_a = (w_mix_in.shape[2] - 2 * w_b) // 3
    n_heads = w_a // E_A
    n_groups = sgu_w.shape[1]
    n_mem, h_m, dh_m = cache_mem_k.shape[2:]
    page_rows = cache_k.shape[2]
    alpha = (2 * depth) ** 0.25
    lambda_init = 0.8 - 0.6 * math.exp(-0.3 * 0)
    l = 0
    t_attn = 512
    tm = 512
    assert seq % tm == 0 and seq % t_attn == 0 and t_attn >= MAX_DIST and tm % CHUNK == 0
    assert page_rows == CHUNK and MAX_DIST <= page_rows

    row2 = lambda a: a.reshape(1, -1)
    g_ln = [row2(ln_g[l, i]) for i in range(4)]
    b_ln = [row2(ln_b[l, i]) for i in range(4)]
    ffn1 = _split_ffn(ffn1_w_in[l], ffn1_w_out[l])
    ffn2 = _split_ffn(ffn2_w_in[l], ffn2_w_out[l])
    q_fold = jnp.concatenate([jnp.full((w_a,), DH_A ** -0.5, F32),
                              jnp.ones((w_mix_in.shape[2] - w_a,), F32)])
    w_mix = (w_mix_in[l] * q_fold).astype(BF16)
    woa = w_mix_out[l, :w_a].astype(BF16)
    wob = w_mix_out[l, w_a:].astype(BF16)
    wq_x = xq_w[l].astype(BF16)
    wo_x = xo_w[l].astype(BF16)
    wkv = xkv_w[l].astype(BF16)
    lam4 = jnp.stack([lambda_q1[l], lambda_k1[l], lambda_q2[l], lambda_k2[l]]).astype(F32)
    g_sub = row2(subln_g[l])
    lng_s, lnb_s = row2(sgu_ln_g[l]), row2(sgu_ln_b[l])
    tril_w = jnp.tril(sgu_w[l]).astype(BF16)
    sgu_bias = sgu_b[l].reshape(n_groups, CHUNK, 1)
    cg = w_b // n_groups
    w00 = jnp.repeat(sgu_w[l, :, 0, 0], cg).reshape(1, w_b)
    b00 = jnp.repeat(sgu_b[l, :, 0], cg).reshape(1, w_b)
    xq_scale = dh_m ** -0.5 * LOG2E

    ar = jnp.arange(t_attn)
    bias0 = _toeplitz(_shifted_bias(rel_bias, ar), _shifted_bias(rel_bias, 0 * ar), t_attn)
    bias1 = _toeplitz(_shifted_bias(rel_bias, t_attn + ar), _shifted_bias(rel_bias, t_attn - ar),
                      t_attn)
    dec_near = _shifted_bias(rel_bias, page_rows - jnp.arange(page_rows))
    dec_bias = jnp.repeat(jnp.repeat(dec_near.T, 2, axis=0), n_heads, axis=1)
    new_bias = jnp.repeat(_shifted_bias(rel_bias, jnp.zeros((1,), jnp.int32)).T, 2, axis=0)

    n_tok = bsz * seq
    xp = x_prompt.reshape(n_tok, d)
    mk_p, mv_p = _mem_kv(mem_prompt.reshape(bsz * n_mem, d), wkv, tm=n_mem)
    x1 = _ffn_ln(xp, *ffn1, g_ln[0], b_ln[0], alpha=alpha, tm=tm)
    qb, k_p, kb, v_p, vb, ob = _mix_in(x1, w_mix, lng_s, lnb_s, tril_w, sgu_bias,
                                       w_a=w_a, w_b=w_b, qscale=LOG2E, tm=tm)
    oa = _attention(lam4, g_sub, qb.reshape(bsz, seq, w_a), kb.reshape(bsz, seq, w_a),
                    vb.reshape(bsz, seq, w_a), bias0, bias1, t=t_attn, lambda_init=lambda_init)
    x3 = _post_mix(x1.reshape(bsz, seq, d), oa, ob.reshape(bsz, seq, w_b),
                   mk_p.reshape(bsz, n_mem, d), mv_p.reshape(bsz, n_mem, d),
                   woa, wob, g_ln[1], b_ln[1], wq_x, wo_x, g_ln[2], b_ln[2],
                   alpha=alpha, qscale=xq_scale, n_heads=h_m, tm=tm)
    y_p = _ffn_ln(x3.reshape(n_tok, d), *ffn2, g_ln[3], b_ln[3], alpha=alpha, tm=tm)

    xs = x_sample.reshape(n_dec, d)
    s1 = _ffn_ln(xs, *ffn1, g_ln[0], b_ln[0], alpha=alpha, tm=n_dec)
    q_s, k_s, v_s, ob_s, vn_s = _mix_in_decode(s1, w_mix, lng_s, lnb_s, w00, b00,
                                               w_a=w_a, w_b=w_b, qscale=LOG2E)
    oa_s = _decode_attention(page_table, lam4, g_sub, q_s.reshape(n_dec, 1, w_a),
                             k_s.reshape(n_dec, 1, w_a), v_s.reshape(n_dec, 1, w_a), dec_bias, new_bias,
                             cache_k.reshape(depth, -1, page_rows * n_heads, E_A),
                             cache_v.reshape(depth, -1, page_rows * n_heads, E_A),
                             layer=l, pages=16, lambda_init=lambda_init)
    s2, qx_s = _mix_out(s1, oa_s.reshape(n_dec, w_a), ob_s, woa, wob, g_ln[1], b_ln[1], wq_x,
                        alpha=alpha, qscale=xq_scale, tm=n_dec)
    ox_s = _cross_attention_decode(qx_s.reshape(n_dec, 1, d),
                                   cache_mem_k.reshape(depth, n_dec, n_mem * h_m, dh_m),
                                   cache_mem_v.reshape(depth, n_dec, n_mem * h_m, dh_m),
                                   layer=l, n_heads=h_m)
    s3 = _proj_ln(s2, ox_s.reshape(n_dec, d), wo_x, g_ln[2], b_ln[2], alpha=alpha, tm=n_dec)
    y_s = _ffn_ln(s3, *ffn2, g_ln[3], b_ln[3], alpha=alpha, tm=n_dec)

    return (y_p.reshape(bsz, seq, d), y_s.reshape(n_dec, 1, d),
            k_p.reshape(1, bsz, seq, n_heads, E_A), v_p.reshape(1, bsz, seq, n_heads, E_A),
            mk_p.reshape(1, bsz, n_mem, h_m, dh_m), mv_p.reshape(1, bsz, n_mem, h_m, dh_m),
            k_s.reshape(1, n_dec, 1, n_heads, E_A), v_s.reshape(1, n_dec, 1, n_heads, E_A),
            vn_s.reshape(1, n_dec, 1, w_b))
```

```python
import functools
import math

import jax
import jax.numpy as jnp
from jax import lax
from jax.experimental import pallas as pl
from jax.experimental.pallas import tpu as pltpu

F32 = jnp.float32
BF16 = jnp.bfloat16

LN_EPS = 1e-5
NEG_INF = -1e30
LOG2E = 1.4426950408889634
DH_A = 64
E_A = 2 * DH_A
CHUNK = 128
NB = 32
MAX_DIST = 128

LANES = 128
MXU_EDGE = 256
MIB = 1024 * 1024
SUB_ROWS = 256


def _params(semantics, vmem_mib):
    return pltpu.CompilerParams(dimension_semantics=semantics, vmem_limit_bytes=vmem_mib * MIB)


def _resident(shape):
    nd = len(shape)
    return pl.BlockSpec(shape, lambda *_: (0,) * nd, pipeline_mode=pl.Buffered(1))


def _layer_norm(x, g, b):
    mu = jnp.mean(x, -1, keepdims=True)
    xc = x - mu
    var = jnp.mean(xc * xc, -1, keepdims=True)
    return xc * lax.rsqrt(var + LN_EPS) * g + b


def _gelu(x):
    return 0.5 * x * (1.0 + lax.erf(x * math.sqrt(0.5)))


def _dot(a, b):
    return jnp.dot(a, b, preferred_element_type=F32)


def _dot_nt(a, b):
    return lax.dot_general(a, b, (((1,), (1,)), ((), ())), preferred_element_type=F32)


def _diff_lambda(lam_ref, lambda_init):
    lv = lam_ref[...]
    a = jnp.sum(lv[0:1] * lv[1:2], axis=-1, keepdims=True)
    b = jnp.sum(lv[2:3] * lv[3:4], axis=-1, keepdims=True)
    return jnp.exp(a) - jnp.exp(b) + lambda_init


def _head_rmsnorm(o, g, lambda_init):
    return o * lax.rsqrt(jnp.mean(o * o, -1, keepdims=True) + LN_EPS) * g * (1.0 - lambda_init)


def _ffn_ln_kernel(x_ref, wi_ref, wo_ref, g_ref, b_ref, o_ref, act_ref, xb_ref, *, alpha, fc):
    tm = x_ref.shape[0]
    sub = min(tm, SUB_ROWS)
    subs = [slice(r0, r0 + sub) for r0 in range(0, tm, sub)]
    xb_ref[...] = x_ref[...].astype(BF16)
    d_ff = wo_ref.shape[0]
    for j in range(d_ff // fc):
        cols = slice(j * fc, (j + 1) * fc)
        up_cols = slice(d_ff + j * fc, d_ff + (j + 1) * fc)
        for rs in subs:
            ha = _dot(xb_ref[rs, :], wi_ref[:, cols])
            hb = _dot(xb_ref[rs, :], wi_ref[:, up_cols])
            act_ref[rs, cols] = (ha * jax.nn.sigmoid(ha) * hb).astype(BF16)
    for rs in subs:
        y = _dot(act_ref[rs, :], wo_ref[...])
        o_ref[rs, :] = _layer_norm(alpha * x_ref[rs, :] + 0.5 * y, g_ref[...], b_ref[...])


def _ffn_ln(x, wi, wo, g, b, *, alpha, tm):
    n, d = x.shape
    d_ff = wo.shape[0]
    assert d_ff % MXU_EDGE == 0 and wi.shape[1] == 2 * d_ff
    kern = functools.partial(_ffn_ln_kernel, alpha=alpha, fc=MXU_EDGE)
    return pl.pallas_call(
        kern,
        grid=(n // tm,),
        in_specs=[
            pl.BlockSpec((tm, d), lambda i: (i, 0)),
            _resident(wi.shape), _resident(wo.shape), _resident(g.shape), _resident(b.shape),
        ],
        out_specs=pl.BlockSpec((tm, d), lambda i: (i, 0)),
        out_shape=jax.ShapeDtypeStruct((n, d), F32),
        scratch_shapes=[pltpu.VMEM((tm, d_ff), BF16), pltpu.VMEM((tm, d), BF16)],
        compiler_params=_params(("parallel",), 52),
        name="ffn_ln",
    )(x, wi, wo, g, b)


def _mix_in_kernel(x_ref, w_ref, lng_ref, lnb_ref, tw_ref, sb_ref,
                   q_ref, k_ref, kb_ref, v_ref, vb_ref, ob_ref, xb_ref, *, w_a, w_b, qscale):
    tm = x_ref.shape[0]
    n_groups = tw_ref.shape[0]
    cg = w_b // n_groups
    starts = list(range(0, tm, SUB_ROWS))
    gated = {}

    def gate_inputs(r0):
        rs = slice(r0, r0 + SUB_ROWS)
        xb_ref[rs, :] = x_ref[rs, :].astype(BF16)
        gv = _gelu(_dot(xb_ref[rs, :], w_ref[:, 3 * w_a + w_b:3 * w_a + 2 * w_b]))
        gu = _gelu(_dot(xb_ref[rs, :], w_ref[:, 3 * w_a:3 * w_a + w_b]))
        gated[r0] = (gu, _layer_norm(gv, lng_ref[...], lnb_ref[...]).astype(BF16))

    def qkv(r0):
        rs = slice(r0, r0 + SUB_ROWS)
        q_ref[rs, :] = (_dot(xb_ref[rs, :], w_ref[:, 0:w_a]) * qscale).astype(BF16)
        hk = _dot(xb_ref[rs, :], w_ref[:, w_a:2 * w_a])
        kb_ref[rs, :] = hk.astype(BF16)
        hv = _dot(xb_ref[rs, :], w_ref[:, 2 * w_a:3 * w_a])
        vb_ref[rs, :] = hv.astype(BF16)
        for h in range(w_a // E_A):
            k_ref[rs, h, :] = hk[:, h * E_A:(h + 1) * E_A]
            v_ref[rs, h, :] = hv[:, h * E_A:(h + 1) * E_A]

    def gate(r0):
        gu, vn = gated.pop(r0)
        for c in range(SUB_ROWS // CHUNK):
            rows = slice(c * CHUNK, (c + 1) * CHUNK)
            out_rows = slice(r0 + c * CHUNK, r0 + (c + 1) * CHUNK)
            for g in range(n_groups):
                cols = slice(g * cg, (g + 1) * cg)
                mixed = _dot(tw_ref[g], vn[rows, cols]) + sb_ref[g]
                ob_ref[out_rows, cols] = (gu[rows, cols] * mixed).astype(BF16)

    gate_inputs(starts[0])
    for cur, nxt in zip(starts, starts[1:] + [None]):
        if nxt is not None:
            gate_inputs(nxt)
        qkv(cur)
        gate(cur)


def _mix_in(x, w, lng, lnb, tw, sb, *, w_a, w_b, qscale, tm):
    n, d = x.shape
    kern = functools.partial(_mix_in_kernel, w_a=w_a, w_b=w_b, qscale=qscale)
    row = lambda width: pl.BlockSpec((tm, width), lambda i: (i, 0))
    shp = lambda width, dt: jax.ShapeDtypeStruct((n, width), dt)
    n_heads = w_a // E_A
    row_h = pl.BlockSpec((tm, n_heads, E_A), lambda i: (i, 0, 0))
    shp_h = jax.ShapeDtypeStruct((n, n_heads, E_A), F32)
    return pl.pallas_call(
        kern,
        grid=(n // tm,),
        in_specs=[row(d), _resident(w.shape), _resident(lng.shape), _resident(lnb.shape),
                  _resident(tw.shape), _resident(sb.shape)],
        out_specs=[row(w_a), row_h, row(w_a), row_h, row(w_a), row(w_b)],
        out_shape=[shp(w_a, BF16), shp_h, shp(w_a, BF16), shp_h, shp(w_a, BF16), shp(w_b, BF16)],
        scratch_shapes=[pltpu.VMEM((tm, d), BF16)],
        compiler_params=_params(("parallel",), 48),
        name="mix_in_sgu",
    )(x, w, lng, lnb, tw, sb)


def _mix_in_decode_kernel(x_ref, w_ref, lng_ref, lnb_ref, w00_ref, b0_ref,
                          q_ref, k_ref, v_ref, ob_ref, vn_ref, *, w_a, w_b, qscale):
    xb = x_ref[...].astype(BF16)
    q_ref[...] = _dot(xb, w_ref[:, 0:w_a]) * qscale
    k_ref[...] = _dot(xb, w_ref[:, w_a:2 * w_a])
    v_ref[...] = _dot(xb, w_ref[:, 2 * w_a:3 * w_a])
    gu = _gelu(_dot(xb, w_ref[:, 3 * w_a:3 * w_a + w_b]))
    gv = _gelu(_dot(xb, w_ref[:, 3 * w_a + w_b:3 * w_a + 2 * w_b]))
    vn = _layer_norm(gv, lng_ref[...], lnb_ref[...])
    vn_ref[...] = vn
    ob_ref[...] = (gu * (w00_ref[...] * vn + b0_ref[...])).astype(BF16)


def _mix_in_decode(x, w, lng, lnb, w00, b0, *, w_a, w_b, qscale):
    n, d = x.shape
    kern = functools.partial(_mix_in_decode_kernel, w_a=w_a, w_b=w_b, qscale=qscale)
    full = lambda shape: pl.BlockSpec(shape, lambda i: (0,) * len(shape))
    return pl.pallas_call(
        kern,
        grid=(1,),
        in_specs=[full(x.shape), full(w.shape), full(lng.shape), full(lnb.shape),
                  full(w00.shape), full(b0.shape)],
        out_specs=[full((n, w_a)), full((n, w_a)), full((n, w_a)), full((n, w_b)), full((n, w_b))],
        out_shape=[jax.ShapeDtypeStruct((n, w_a), F32), jax.ShapeDtypeStruct((n, w_a), F32),
                   jax.ShapeDtypeStruct((n, w_a), F32), jax.ShapeDtypeStruct((n, w_b), BF16),
                   jax.ShapeDtypeStruct((n, w_b), F32)],
        compiler_params=_params(("arbitrary",), 32),
        name="mix_in_decode",
    )(x, w, lng, lnb, w00, b0)


def _attn_kernel(lam_ref, g_ref, q_ref, k_ref, v_ref, b0_ref, b1_ref, o_ref,
                 qq_ref, m_ref, acc_ref, p_last, *, t, n_heads, lambda_init):
    qi = pl.program_id(1)
    lane = lax.broadcasted_iota(jnp.int32, (t, E_A), 1)
    for h in range(n_heads):
        q = q_ref[:, h * E_A:(h + 1) * E_A]
        zero = jnp.zeros_like(q)
        qq_ref[2 * h] = jnp.where(lane < DH_A, q, zero)
        qq_ref[2 * h + 1] = jnp.where(lane < DH_A, zero, q)
    m_ref[...] = jnp.full_like(m_ref, NEG_INF)
    acc_ref[...] = jnp.zeros_like(acc_ref)
    ones = jnp.ones((t, LANES), BF16)
    n_blk = t // LANES

    last_u = 2 * n_heads - 1
    last_cols = slice((n_heads - 1) * E_A, n_heads * E_A)
    p_last[...] = jnp.zeros_like(p_last)

    def flush_last(j_done):
        off = pl.multiple_of(j_done * t, t)
        va = jnp.concatenate([v_ref[pl.ds(off, t), last_cols], ones], axis=1)
        acc_ref[last_u] += _dot(p_last[...], va)

    def step(j, bias_ref, causal):
        flush_last(jnp.maximum(j - 1, 0))
        off = pl.multiple_of(j * t, t)
        if causal:
            visible = (lax.broadcasted_iota(jnp.int32, (t, t), 0)
                       >= lax.broadcasted_iota(jnp.int32, (t, t), 1))
        for h in range(n_heads):
            cols = slice(h * E_A, (h + 1) * E_A)
            kt = k_ref[pl.ds(off, t), cols]
            va = jnp.concatenate([v_ref[pl.ds(off, t), cols], ones], axis=1)
            for u in (2 * h, 2 * h + 1):
                s = _dot_nt(qq_ref[u], kt)
                if bias_ref is not None:
                    s = s + bias_ref[h]
                if causal:
                    s = jnp.where(visible, s, NEG_INF)
                blocks = [s[:, i * LANES:(i + 1) * LANES] for i in range(n_blk)]
                lane_max = functools.reduce(jnp.maximum, blocks)
                row_max = jnp.max(lane_max, axis=-1, keepdims=True)
                m_old = m_ref[u]
                m_new = jnp.maximum(m_old, jnp.broadcast_to(row_max, m_old.shape))
                alpha = jnp.exp2(m_old - m_new)
                p = jnp.concatenate([jnp.exp2(blk - m_new).astype(BF16) for blk in blocks], axis=1)
                alpha2 = jnp.concatenate([alpha, alpha], axis=1)
                if u == last_u:
                    acc_ref[u] = alpha2 * acc_ref[u]
                    p_last[...] = p
                else:
                    acc_ref[u] = alpha2 * acc_ref[u] + _dot(p, va)
                m_ref[u] = m_new

    def far(j, carry):
        step(j, None, False)
        return carry

    lax.fori_loop(0, jnp.maximum(qi - 1, 0), far, 0)

    @pl.when(qi >= 1)
    def _():
        step(qi - 1, b1_ref, False)

    step(qi, b0_ref, True)
    flush_last(qi)

    lam = _diff_lambda(lam_ref, lambda_init)
    for h in range(n_heads):
        a0 = acc_ref[2 * h]
        a1 = acc_ref[2 * h + 1]
        o = a0[:, 0:E_A] / a0[:, E_A:2 * E_A] - lam * (a1[:, 0:E_A] / a1[:, E_A:2 * E_A])
        o_ref[:, h * E_A:(h + 1) * E_A] = _head_rmsnorm(o, g_ref[...], lambda_init).astype(BF16)


def _attention(lam4, subln_g, q, k, v, b0, b1, *, t, lambda_init):
    bsz, s, w_a = q.shape
    n_heads = w_a // E_A
    kern = functools.partial(_attn_kernel, t=t, n_heads=n_heads, lambda_init=lambda_init)
    const = lambda shape: pl.BlockSpec(shape, lambda b, i: (0,) * len(shape))
    seq_spec = pl.BlockSpec((None, s, w_a), lambda b, i: (b, 0, 0), pipeline_mode=pl.Buffered(1))
    return pl.pallas_call(
        kern,
        grid=(bsz, s // t),
        in_specs=[
            const(lam4.shape), const(subln_g.shape),
            pl.BlockSpec((None, t, w_a), lambda b, i: (b, i, 0)),
            seq_spec, seq_spec,
            _resident(b0.shape), _resident(b1.shape),
        ],
        out_specs=pl.BlockSpec((None, t, w_a), lambda b, i: (b, i, 0)),
        out_shape=jax.ShapeDtypeStruct((bsz, s, w_a), BF16),
        scratch_shapes=[pltpu.VMEM((2 * n_heads, t, E_A), BF16),
                        pltpu.VMEM((2 * n_heads, t, LANES), F32),
                        pltpu.VMEM((2 * n_heads, t, E_A + LANES), F32),
                        pltpu.VMEM((t, t), BF16)],
        compiler_params=_params(("parallel", "arbitrary"), 52),
        name="diff_attn_prompt",
    )(lam4, subln_g, q, k, v, b0, b1)


def _decode_attn_kernel(pt_ref, lam_ref, g_ref, q_ref, kn_ref, vn_ref, bias_ref, new_bias_ref, *rest,
                        pages, n_heads, lambda_init):
    del pt_ref
    k_refs = rest[:pages]
    v_refs = rest[pages:2 * pages]
    o_ref, kb_ref, vb_ref, m_ref, l_ref, acc_ref = rest[2 * pages:]
    j = pl.program_id(1)
    last = pl.num_programs(1) - 1
    rows = 2 * n_heads
    page_len = k_refs[0].shape[0]

    @pl.when(j == 0)
    def _():
        m_ref[...] = jnp.full_like(m_ref, NEG_INF)
        l_ref[...] = jnp.zeros_like(l_ref)
        acc_ref[...] = jnp.zeros_like(acc_ref)

    r_id = lax.broadcasted_iota(jnp.int32, (rows, E_A), 0)
    l_id = lax.broadcasted_iota(jnp.int32, (rows, E_A), 1)

    def per_row_head(tok_ref):
        out = jnp.zeros((rows, E_A), F32)
        for h in range(n_heads):
            piece = jnp.broadcast_to(tok_ref[:, h * E_A:(h + 1) * E_A], (rows, E_A))
            out = jnp.where((r_id // 2) == h, piece, out)
        return out

    qm = jnp.where((l_id // DH_A) == (r_id % 2), per_row_head(q_ref), 0.0)

    for i in range(pages):
        kb_ref[i * page_len:(i + 1) * page_len, :] = k_refs[i][...].astype(BF16)
        vb_ref[i * page_len:(i + 1) * page_len, :] = v_refs[i][...].astype(BF16)

    n_keys = pages * page_len
    s = _dot_nt(qm.astype(BF16), kb_ref[...])
    near = jnp.where(j == last, 1.0, 0.0)
    s_tail = s[:, n_keys - page_len:] + near * bias_ref[...]
    s = jnp.concatenate([s[:, :n_keys - page_len], s_tail], axis=1) if pages > 1 else s_tail
    row_s = lax.broadcasted_iota(jnp.int32, s.shape, 0)
    col_s = lax.broadcasted_iota(jnp.int32, s.shape, 1)
    s = jnp.where((col_s % n_heads) == (row_s // 2), s, NEG_INF)

    m_old = m_ref[...]
    m_new = jnp.maximum(m_old, jnp.max(s, axis=-1, keepdims=True))
    alpha = jnp.exp2(m_old - m_new)
    p = jnp.exp2(s - m_new)
    l_ref[...] = alpha * l_ref[...] + jnp.sum(p, axis=-1, keepdims=True)
    acc_ref[...] = alpha * acc_ref[...] + _dot(p.astype(BF16), vb_ref[...])
    m_ref[...] = m_new

    @pl.when(j == last)
    def _():
        s_new = (jnp.sum(qm * per_row_head(kn_ref), axis=-1, keepdims=True) + new_bias_ref[...])
        m_old = m_ref[...]
        m_new = jnp.maximum(m_old, s_new)
        alpha = jnp.exp2(m_old - m_new)
        p_new = jnp.exp2(s_new - m_new)
        l_fin = alpha * l_ref[...] + p_new
        o = (alpha * acc_ref[...] + p_new * per_row_head(vn_ref)) / l_fin
        lam = _diff_lambda(lam_ref, lambda_init)
        g = g_ref[...]
        for h in range(n_heads):
            oh = o[2 * h:2 * h + 1] - lam * o[2 * h + 1:2 * h + 2]
            o_ref[:, h * E_A:(h + 1) * E_A] = _head_rmsnorm(oh, g, lambda_init).astype(BF16)


def _decode_attention(page_table, lam4, subln_g, q, k_new, v_new, bias, new_bias, cache_k, cache_v, *,
                      layer, pages, lambda_init):
    n_seq, _, w_a = q.shape
    n_pages = page_table.shape[1]
    n_heads = w_a // E_A
    rows = 2 * n_heads
    page_len = cache_k.shape[2]
    kern = functools.partial(_decode_attn_kernel, pages=pages, n_heads=n_heads,
                             lambda_init=lambda_init)
    const = lambda shape: pl.BlockSpec(shape, lambda b, j, pt: (0,) * len(shape))
    tok = pl.BlockSpec((None, 1, w_a), lambda b, j, pt: (b, 0, 0))

    def page_spec(i):
        return pl.BlockSpec((None, None, page_len, E_A),
                            lambda b, j, pt: (layer, pt[b * n_pages + j * pages + i], 0, 0))

    grid_spec = pltpu.PrefetchScalarGridSpec(
        num_scalar_prefetch=1,
        grid=(n_seq, n_pages // pages),
        in_specs=[const(lam4.shape), const(subln_g.shape), tok, tok, tok, const(bias.shape),
                  const(new_bias.shape)]
        + [page_spec(i) for i in range(pages)] + [page_spec(i) for i in range(pages)],
        out_specs=pl.BlockSpec((None, 1, w_a), lambda b, j, pt: (b, 0, 0)),
        scratch_shapes=[pltpu.VMEM((pages * page_len, E_A), BF16),
                        pltpu.VMEM((pages * page_len, E_A), BF16),
                        pltpu.VMEM((rows, 1), F32), pltpu.VMEM((rows, 1), F32),
                        pltpu.VMEM((rows, E_A), F32)],
    )
    return pl.pallas_call(
        kern,
        grid_spec=grid_spec,
        out_shape=jax.ShapeDtypeStruct((n_seq, 1, w_a), BF16),
        compiler_params=_params(("parallel", "arbitrary"), 40),
        name="diff_attn_decode",
    )(page_table.reshape(-1), lam4, subln_g, q, k_new, v_new, bias, new_bias,
      *([cache_k] * pages), *([cache_v] * pages))


def _mix_out_kernel(x_ref, oa_ref, ob_ref, woa_ref, wob_ref, g_ref, b_ref, wq_ref,
                    x2_ref, qx_ref, *, alpha, qscale):
    y = _dot(oa_ref[...], woa_ref[...]) + _dot(ob_ref[...], wob_ref[...])
    x2 = _layer_norm(alpha * x_ref[...] + y, g_ref[...], b_ref[...])
    x2_ref[...] = x2
    qx_ref[...] = (_dot(x2.astype(BF16), wq_ref[...]) * qscale).astype(BF16)


def _mix_out(x, oa, ob, woa, wob, g, b, wq, *, alpha, qscale, tm):
    n, d = x.shape
    kern = functools.partial(_mix_out_kernel, alpha=alpha, qscale=qscale)
    row = lambda width: pl.BlockSpec((tm, width), lambda i: (i, 0))
    return pl.pallas_call(
        kern,
        grid=(n // tm,),
        in_specs=[row(d), row(oa.shape[1]), row(ob.shape[1]), _resident(woa.shape),
                  _resident(wob.shape), _resident(g.shape), _resident(b.shape), _resident(wq.shape)],
        out_specs=[row(d), row(d)],
        out_shape=[jax.ShapeDtypeStruct((n, d), F32), jax.ShapeDtypeStruct((n, d), BF16)],
        compiler_params=_params(("parallel",), 40),
        name="mix_out_ln_q",
    )(x, oa, ob, woa, wob, g, b, wq)


def _post_mix_kernel(x_ref, oa_ref, ob_ref, mk_ref, mv_ref, woa_ref, wob_ref, g1_ref, b1_ref,
                     wq_ref, wo_ref, g2_ref, b2_ref, y_ref, o_scr, x2_scr, xb_scr, q_scr,
                     *, alpha, qscale, n_heads):
    tm, d = x_ref.shape
    dh = d // n_heads
    subs = [slice(r0, r0 + SUB_ROWS) for r0 in range(0, tm, SUB_ROWS)]
    for rs in subs:
        y = _dot(oa_ref[rs, :], woa_ref[...]) + _dot(ob_ref[rs, :], wob_ref[...])
        x2 = _layer_norm(alpha * x_ref[rs, :] + y, g1_ref[...], b1_ref[...])
        x2_scr[rs, :] = x2
        xb_scr[rs, :] = x2.astype(BF16)
    for rs in subs:
        q_scr[rs, :] = (_dot(xb_scr[rs, :], wq_ref[...]) * qscale).astype(BF16)
    for h in range(n_heads):
        cols = slice(h * dh, (h + 1) * dh)
        for rs in subs:
            s = _dot_nt(q_scr[rs, cols], mk_ref[:, cols])
            p = jnp.exp2(s - jnp.max(s, axis=-1, keepdims=True))
            l = jnp.sum(p, axis=-1, keepdims=True)
            o_scr[rs, cols] = (_dot(p.astype(BF16), mv_ref[:, cols]) / l).astype(BF16)
    for rs in subs:
        y2 = _dot(o_scr[rs, :], wo_ref[...])
        y_ref[rs, :] = _layer_norm(alpha * x2_scr[rs, :] + y2, g2_ref[...], b2_ref[...])


def _post_mix(x, oa, ob, mk, mv, woa, wob, g1, b1, wq, wo, g2, b2, *, alpha, qscale, n_heads, tm):
    bsz, s, d = x.shape
    n_mem = mk.shape[1]
    kern = functools.partial(_post_mix_kernel, alpha=alpha, qscale=qscale, n_heads=n_heads)
    row = lambda width: pl.BlockSpec((None, tm, width), lambda b, i: (b, i, 0))
    mem = pl.BlockSpec((None, n_mem, d), lambda b, i: (b, 0, 0))
    weights = [woa, wob, g1, b1, wq, wo, g2, b2]
    return pl.pallas_call(
        kern,
        grid=(bsz, s // tm),
        in_specs=[row(d), row(oa.shape[2]), row(ob.shape[2]), mem, mem]
        + [_resident(w.shape) for w in weights],
        out_specs=row(d),
        out_shape=jax.ShapeDtypeStruct((bsz, s, d), F32),
        scratch_shapes=[pltpu.VMEM((tm, d), BF16), pltpu.VMEM((tm, d), F32),
                        pltpu.VMEM((tm, d), BF16), pltpu.VMEM((tm, d), BF16)],
        compiler_params=_params(("parallel", "parallel"), 48),
        name="post_mix",
    )(x, oa, ob, mk, mv, *weights)


def _cross_decode_kernel(q_ref, mk_ref, mv_ref, o_ref, *, n_heads):
    dh = mk_ref.shape[1]
    sublanes = 8
    q = q_ref[...].astype(F32)
    r_id = lax.broadcasted_iota(jnp.int32, (sublanes, dh), 0)
    qm = jnp.zeros((sublanes, dh), F32)
    for h in range(n_heads):
        piece = jnp.broadcast_to(q[:, h * dh:(h + 1) * dh], (sublanes, dh))
        qm = jnp.where((r_id % n_heads) == h, piece, qm)
    s = _dot_nt(qm.astype(BF16), mk_ref[...].astype(BF16))
    row_s = lax.broadcasted_iota(jnp.int32, s.shape, 0)
    col_s = lax.broadcasted_iota(jnp.int32, s.shape, 1)
    s = jnp.where((col_s % n_heads) == (row_s % n_heads), s, NEG_INF)
    p = jnp.exp2(s - jnp.max(s, axis=-1, keepdims=True))
    l = jnp.sum(p, axis=-1, keepdims=True)
    o = _dot(p.astype(BF16), mv_ref[...].astype(BF16)) / l
    for h in range(n_heads):
        o_ref[:, h * dh:(h + 1) * dh] = o[h:h + 1].astype(BF16)


def _cross_attention_decode(qx, mk, mv, *, layer, n_heads):
    n_seq, _, d = qx.shape
    mem_len, dh = mk.shape[2:]
    kern = functools.partial(_cross_decode_kernel, n_heads=n_heads)
    mem_spec = pl.BlockSpec((None, None, mem_len, dh), lambda b: (layer, b, 0, 0))
    return pl.pallas_call(
        kern,
        grid=(n_seq,),
        in_specs=[pl.BlockSpec((None, 1, d), lambda b: (b, 0, 0)), mem_spec, mem_spec],
        out_specs=pl.BlockSpec((None, 1, d), lambda b: (b, 0, 0)),
        out_shape=jax.ShapeDtypeStruct((n_seq, 1, d), BF16),
        compiler_params=_params(("parallel",), 32),
        name="cross_attn_decode",
    )(qx, mk, mv)


def _proj_ln_kernel(x_ref, o_ref, w_ref, g_ref, b_ref, y_ref, *, alpha):
    y = _dot(o_ref[...], w_ref[...])
    y_ref[...] = _layer_norm(alpha * x_ref[...] + y, g_ref[...], b_ref[...])


def _proj_ln(x, o, w, g, b, *, alpha, tm):
    n, d = x.shape
    kern = functools.partial(_proj_ln_kernel, alpha=alpha)
    row = pl.BlockSpec((tm, d), lambda i: (i, 0))
    return pl.pallas_call(
        kern,
        grid=(n // tm,),
        in_specs=[row, row, _resident(w.shape), _resident(g.shape), _resident(b.shape)],
        out_specs=row,
        out_shape=jax.ShapeDtypeStruct((n, d), F32),
        compiler_params=_params(("parallel",), 32),
        name="proj_ln",
    )(x, o, w, g, b)


def _mem_kv_kernel(m_ref, w_ref, k_ref, v_ref, kb_ref, vb_ref):
    d = kb_ref.shape[1]
    n_heads, dh = k_ref.shape[1:]
    mb = m_ref[...].astype(BF16)
    k = _dot(mb, w_ref[:, 0:d])
    v = _dot(mb, w_ref[:, d:2 * d])
    kb_ref[...] = k.astype(BF16)
    vb_ref[...] = v.astype(BF16)
    for h in range(n_heads):
        k_ref[:, h, :] = k[:, h * dh:(h + 1) * dh]
        v_ref[:, h, :] = v[:, h * dh:(h + 1) * dh]


def _mem_kv(mem, w, *, n_heads, tm):
    n, d = mem.shape
    dh = d // n_heads
    row = pl.BlockSpec((tm, d), lambda i: (i, 0))
    row_h = pl.BlockSpec((tm, n_heads, dh), lambda i: (i, 0, 0))
    return pl.pallas_call(
        _mem_kv_kernel,
        grid=(n // tm,),
        in_specs=[row, _resident(w.shape)],
        out_specs=[row_h, row_h, row, row],
        out_shape=[jax.ShapeDtypeStruct((n, n_heads, dh), F32)] * 2
        + [jax.ShapeDtypeStruct((n, d), BF16)] * 2,
        compiler_params=_params(("parallel",), 32),
        name="mem_kv",
    )(mem, w)


def _shifted_bias(table, dist):
    n = jnp.maximum(dist, 0)
    max_exact = NB // 2
    nf = jnp.maximum(n, 1).astype(F32)
    large = max_exact + (jnp.log(nf / max_exact) / math.log(MAX_DIST / max_exact)
                         * (NB - max_exact)).astype(jnp.int32)
    large = jnp.minimum(large, NB - 1)
    bucket = jnp.where(n < max_exact, n, large)
    tab = (table.astype(F32) - table[NB - 1].astype(F32)) * LOG2E
    hit = bucket[..., None, None] == jnp.arange(NB)[:, None]
    return jnp.sum(jnp.where(hit, tab, 0.0), axis=-2)


def _place_blocks(sel, tile):
    nb, n = sel.shape[0], tile.shape[1]
    out = sel[None, :, None, :, None] * tile[:, None, :, None, :]
    return out.reshape(tile.shape[0], nb * n, nb * n)


def _toeplitz(f_pos, f_neg, t):
    period = 2 * t
    v = jnp.concatenate([f_neg, jnp.zeros_like(f_pos[:1]), f_pos[:0:-1]], axis=0)
    tiled = jnp.tile(v, (t, 1))[: t * (period - 1)]
    skew = tiled.reshape(t, period - 1, -1)
    return jnp.transpose(skew[:, :t], (2, 0, 1))


def kernel(x_prompt, x_sample, mem_prompt, cache_k, cache_v, cache_mem_k, cache_mem_v, page_table, rel_bias, ln_g, ln_b, ffn1_w_in, ffn1_w_out, w_mix_in, w_mix_out, lambda_q1, lambda_k1, lambda_q2, lambda_k2, subln_g, sgu_ln_g, sgu_ln_b, sgu_w, sgu_b, xq_w, xkv_w, xo_w, ffn2_w_in, ffn2_w_out):
    bsz, seq, d = x_prompt.shape
    n_dec = x_sample.shape[0]
    depth = ln_g.shape[0]
    assert depth == 1 and x_sample.shape[1] == 1
    w_b = sgu_ln_g.shape[1]
    w_a = (w_mix_in.shape[2] - 2 * w_b) // 3
    n_heads = w_a // E_A
    n_groups = sgu_w.shape[1]
    n_mem, h_m, dh_m = cache_mem_k.shape[2:]
    page_rows = cache_k.shape[2]
    alpha = (2 * depth) ** 0.25
    lambda_init = 0.8 - 0.6 * math.exp(-0.3 * 0)
    l = 0
    t_attn = 512
    tm = 1024
    assert seq % tm == 0 and seq % t_attn == 0 and t_attn % MAX_DIST == 0 and tm % SUB_ROWS == 0
    assert SUB_ROWS % CHUNK == 0
    assert page_rows == CHUNK and MAX_DIST <= page_rows

    row2 = lambda a: a.reshape(1, -1)
    g_ln = [row2(ln_g[l, i]) for i in range(4)]
    b_ln = [row2(ln_b[l, i]) for i in range(4)]
    ffn1 = (ffn1_w_in[l].astype(BF16), ffn1_w_out[l].astype(BF16))
    ffn2 = (ffn2_w_in[l].astype(BF16), ffn2_w_out[l].astype(BF16))
    q_fold = jnp.concatenate([jnp.full((w_a,), DH_A ** -0.5, F32),
                              jnp.ones((w_mix_in.shape[2] - w_a,), F32)])
    w_mix = (w_mix_in[l] * q_fold).astype(BF16)
    woa = w_mix_out[l, :w_a].astype(BF16)
    wob = w_mix_out[l, w_a:].astype(BF16)
    wq_x = xq_w[l].astype(BF16)
    wo_x = xo_w[l].astype(BF16)
    wkv = xkv_w[l].astype(BF16)
    lam4 = jnp.stack([lambda_q1[l], lambda_k1[l], lambda_q2[l], lambda_k2[l]]).astype(F32)
    g_sub = row2(subln_g[l])
    lng_s, lnb_s = row2(sgu_ln_g[l]), row2(sgu_ln_b[l])
    tril_w = jnp.tril(sgu_w[l]).astype(BF16)
    sgu_bias = sgu_b[l].reshape(n_groups, CHUNK, 1)
    cg = w_b // n_groups
    w00 = jnp.repeat(sgu_w[l, :, 0, 0], cg).reshape(1, w_b)
    b00 = jnp.repeat(sgu_b[l, :, 0], cg).reshape(1, w_b)
    xq_scale = dh_m ** -0.5 * LOG2E

    ar = jnp.arange(MAX_DIST)
    near = _toeplitz(_shifted_bias(rel_bias, ar), _shifted_bias(rel_bias, 0 * ar), MAX_DIST)
    edge = _toeplitz(_shifted_bias(rel_bias, MAX_DIST + ar), _shifted_bias(rel_bias, MAX_DIST - ar),
                     MAX_DIST)
    nb = t_attn // MAX_DIST
    bias0 = (_place_blocks(jnp.eye(nb, dtype=F32), near)
             + _place_blocks(jnp.eye(nb, k=-1, dtype=F32), edge))
    bias1 = _place_blocks(jnp.eye(nb, k=nb - 1, dtype=F32), edge)
    dec_near = _shifted_bias(rel_bias, page_rows - jnp.arange(page_rows))
    dec_bias = jnp.repeat(jnp.repeat(dec_near.T, 2, axis=0), n_heads, axis=1)
    new_bias = jnp.repeat(_shifted_bias(rel_bias, jnp.zeros((1,), jnp.int32)).T, 2, axis=0)

    n_tok = bsz * seq
    xp = x_prompt.reshape(n_tok, d)
    mk_p, mv_p, mkb, mvb = _mem_kv(mem_prompt.reshape(bsz * n_mem, d), wkv, n_heads=h_m, tm=n_mem)
    x1 = _ffn_ln(xp, *ffn1, g_ln[0], b_ln[0], alpha=alpha, tm=tm)
    qb, k_p, kb, v_p, vb, ob = _mix_in(x1, w_mix, lng_s, lnb_s, tril_w, sgu_bias,
                                       w_a=w_a, w_b=w_b, qscale=LOG2E, tm=tm)
    oa = _attention(lam4, g_sub, qb.reshape(bsz, seq, w_a), kb.reshape(bsz, seq, w_a),
                    vb.reshape(bsz, seq, w_a), bias0, bias1, t=t_attn, lambda_init=lambda_init)
    x3 = _post_mix(x1.reshape(bsz, seq, d), oa, ob.reshape(bsz, seq, w_b),
                   mkb.reshape(bsz, n_mem, d), mvb.reshape(bsz, n_mem, d),
                   woa, wob, g_ln[1], b_ln[1], wq_x, wo_x, g_ln[2], b_ln[2],
                   alpha=alpha, qscale=xq_scale, n_heads=h_m, tm=tm)
    y_p = _ffn_ln(x3.reshape(n_tok, d), *ffn2, g_ln[3], b_ln[3], alpha=alpha, tm=tm)

    xs = x_sample.reshape(n_dec, d)
    s1 = _ffn_ln(xs, *ffn1, g_ln[0], b_ln[0], alpha=alpha, tm=n_dec)
    q_s, k_s, v_s, ob_s, vn_s = _mix_in_decode(s1, w_mix, lng_s, lnb_s, w00, b00,
                                               w_a=w_a, w_b=w_b, qscale=LOG2E)
    oa_s = _decode_attention(page_table, lam4, g_sub, q_s.reshape(n_dec, 1, w_a),
                             k_s.reshape(n_dec, 1, w_a), v_s.reshape(n_dec, 1, w_a), dec_bias, new_bias,
                             cache_k.reshape(depth, -1, page_rows * n_heads, E_A),
                             cache_v.reshape(depth, -1, page_rows * n_heads, E_A),
                             layer=l, pages=16, lambda_init=lambda_init)
    s2, qx_s = _mix_out(s1, oa_s.reshape(n_dec, w_a), ob_s, woa, wob, g_ln[1], b_ln[1], wq_x,
                        alpha=alpha, qscale=xq_scale, tm=n_dec)
    ox_s = _cross_attention_decode(qx_s.reshape(n_dec, 1, d),
                                   cache_mem_k.reshape(depth, n_dec, n_mem * h_m, dh_m),
                                   cache_mem_v.reshape(depth, n_dec, n_mem * h_m, dh_m),
                                   layer=l, n_heads=h_m)
    s3 = _proj_ln(s2, ox_s.reshape(n_dec, d), wo_x, g_ln[2], b_ln[2], alpha=alpha, tm=n_dec)
    y_s = _ffn_ln(s3, *ffn2, g_ln[3], b_ln[3], alpha=alpha, tm=n_dec)

    return (y_p.reshape(bsz, seq, d), y_s.reshape(n_dec, 1, d),
            k_p.reshape(1, bsz, seq, n_heads, E_A), v_p.reshape(1, bsz, seq, n_heads, E_A),
            mk_p.reshape(1, bsz, n_mem, h_m, dh_m), mv_p.reshape(1, bsz, n_mem, h_m, dh_m),
            k_s.reshape(1, n_dec, 1, n_heads, E_A), v_s.reshape(1, n_dec, 1, n_heads, E_A),
            vn_s.reshape(1, n_dec, 1, w_b))
```

```python
import functools
import math

import jax
import jax.numpy as jnp
from jax import lax
from jax.experimental import pallas as pl
from jax.experimental.pallas import tpu as pltpu

F32 = jnp.float32
BF16 = jnp.bfloat16

LN_EPS = 1e-5
NEG_INF = -1e30
LOG2E = 1.4426950408889634
DH_A = 64
E_A = 2 * DH_A
CHUNK = 128
NB = 32
MAX_DIST = 128

LANES = 128
MXU_EDGE = 256
MIB = 1024 * 1024
SUB_ROWS = 256


def _params(semantics, vmem_mib):
    return pltpu.CompilerParams(dimension_semantics=semantics, vmem_limit_bytes=vmem_mib * MIB)


def _resident(shape):
    nd = len(shape)
    return pl.BlockSpec(shape, lambda *_: (0,) * nd, pipeline_mode=pl.Buffered(1))


def _layer_norm(x, g, b):
    mu = jnp.mean(x, -1, keepdims=True)
    xc = x - mu
    var = jnp.mean(xc * xc, -1, keepdims=True)
    return xc * lax.rsqrt(var + LN_EPS) * g + b


def _gelu(x):
    return 0.5 * x * (1.0 + lax.erf(x * math.sqrt(0.5)))


def _dot(a, b):
    return jnp.dot(a, b, preferred_element_type=F32)


def _dot_nt(a, b):
    return lax.dot_general(a, b, (((1,), (1,)), ((), ())), preferred_element_type=F32)


def _diff_lambda(lam_ref, lambda_init):
    lv = lam_ref[...]
    a = jnp.sum(lv[0:1] * lv[1:2], axis=-1, keepdims=True)
    b = jnp.sum(lv[2:3] * lv[3:4], axis=-1, keepdims=True)
    return jnp.exp(a) - jnp.exp(b) + lambda_init


def _head_rmsnorm(o, g, lambda_init):
    return o * lax.rsqrt(jnp.mean(o * o, -1, keepdims=True) + LN_EPS) * g * (1.0 - lambda_init)


def _ffn_ln_kernel(x_ref, wi_ref, wo_ref, g_ref, b_ref, o_ref, act_ref, xb_ref, *, alpha, fc):
    tm = x_ref.shape[0]
    sub = min(tm, SUB_ROWS)
    subs = [slice(r0, r0 + sub) for r0 in range(0, tm, sub)]
    xb_ref[...] = x_ref[...].astype(BF16)
    d_ff = wo_ref.shape[0]
    for j in range(d_ff // fc):
        cols = slice(j * fc, (j + 1) * fc)
        up_cols = slice(d_ff + j * fc, d_ff + (j + 1) * fc)
        for rs in subs:
            ha = _dot(xb_ref[rs, :], wi_ref[:, cols])
            hb = _dot(xb_ref[rs, :], wi_ref[:, up_cols])
            act_ref[rs, cols] = (ha * jax.nn.sigmoid(ha) * hb).astype(BF16)
    for rs in subs:
        y = _dot(act_ref[rs, :], wo_ref[...])
        o_ref[rs, :] = _layer_norm(alpha * x_ref[rs, :] + 0.5 * y, g_ref[...], b_ref[...])


def _ffn_ln(x, wi, wo, g, b, *, alpha, tm):
    n, d = x.shape
    d_ff = wo.shape[0]
    assert d_ff % MXU_EDGE == 0 and wi.shape[1] == 2 * d_ff
    kern = functools.partial(_ffn_ln_kernel, alpha=alpha, fc=MXU_EDGE)
    return pl.pallas_call(
        kern,
        grid=(n // tm,),
        in_specs=[
            pl.BlockSpec((tm, d), lambda i: (i, 0)),
            _resident(wi.shape), _resident(wo.shape), _resident(g.shape), _resident(b.shape),
        ],
        out_specs=pl.BlockSpec((tm, d), lambda i: (i, 0)),
        out_shape=jax.ShapeDtypeStruct((n, d), F32),
        scratch_shapes=[pltpu.VMEM((tm, d_ff), BF16), pltpu.VMEM((tm, d), BF16)],
        compiler_params=_params(("parallel",), 52),
        name="ffn_ln",
    )(x, wi, wo, g, b)


def _mix_in_kernel(x_ref, w_ref, lng_ref, lnb_ref, tw_ref, sb_ref,
                   q_ref, k_ref, kb_ref, v_ref, vb_ref, ob_ref, xb_ref, *, w_a, w_b, qscale):
    tm = x_ref.shape[0]
    n_groups = tw_ref.shape[0]
    cg = w_b // n_groups
    starts = list(range(0, tm, SUB_ROWS))
    gated = {}

    def gate_inputs(r0):
        rs = slice(r0, r0 + SUB_ROWS)
        xb_ref[rs, :] = x_ref[rs, :].astype(BF16)
        gv = _gelu(_dot(xb_ref[rs, :], w_ref[:, 3 * w_a + w_b:3 * w_a + 2 * w_b]))
        gu = _gelu(_dot(xb_ref[rs, :], w_ref[:, 3 * w_a:3 * w_a + w_b]))
        gated[r0] = (gu, _layer_norm(gv, lng_ref[...], lnb_ref[...]).astype(BF16))

    def qkv(r0):
        rs = slice(r0, r0 + SUB_ROWS)
        q_ref[rs, :] = (_dot(xb_ref[rs, :], w_ref[:, 0:w_a]) * qscale).astype(BF16)
        hk = _dot(xb_ref[rs, :], w_ref[:, w_a:2 * w_a])
        kb_ref[rs, :] = hk.astype(BF16)
        hv = _dot(xb_ref[rs, :], w_ref[:, 2 * w_a:3 * w_a])
        vb_ref[rs, :] = hv.astype(BF16)
        for h in range(w_a // E_A):
            k_ref[rs, h, :] = hk[:, h * E_A:(h + 1) * E_A]
            v_ref[rs, h, :] = hv[:, h * E_A:(h + 1) * E_A]

    def gate(r0):
        gu, vn = gated.pop(r0)
        for c in range(SUB_ROWS // CHUNK):
            rows = slice(c * CHUNK, (c + 1) * CHUNK)
            out_rows = slice(r0 + c * CHUNK, r0 + (c + 1) * CHUNK)
            for g in range(n_groups):
                cols = slice(g * cg, (g + 1) * cg)
                mixed = _dot(tw_ref[g], vn[rows, cols]) + sb_ref[g]
                ob_ref[out_rows, cols] = (gu[rows, cols] * mixed).astype(BF16)

    gate_inputs(starts[0])
    for cur, nxt in zip(starts, starts[1:] + [None]):
        if nxt is not None:
            gate_inputs(nxt)
        qkv(cur)
        gate(cur)


def _mix_in(x, w, lng, lnb, tw, sb, *, w_a, w_b, qscale, tm):
    n, d = x.shape
    kern = functools.partial(_mix_in_kernel, w_a=w_a, w_b=w_b, qscale=qscale)
    row = lambda width: pl.BlockSpec((tm, width), lambda i: (i, 0))
    shp = lambda width, dt: jax.ShapeDtypeStruct((n, width), dt)
    n_heads = w_a // E_A
    row_h = pl.BlockSpec((tm, n_heads, E_A), lambda i: (i, 0, 0))
    shp_h = jax.ShapeDtypeStruct((n, n_heads, E_A), F32)
    return pl.pallas_call(
        kern,
        grid=(n // tm,),
        in_specs=[row(d), _resident(w.shape), _resident(lng.shape), _resident(lnb.shape),
                  _resident(tw.shape), _resident(sb.shape)],
        out_specs=[row(w_a), row_h, row(w_a), row_h, row(w_a), row(w_b)],
        out_shape=[shp(w_a, BF16), shp_h, shp(w_a, BF16), shp_h, shp(w_a, BF16), shp(w_b, BF16)],
        scratch_shapes=[pltpu.VMEM((tm, d), BF16)],
        compiler_params=_params(("parallel",), 48),
        name="mix_in_sgu",
    )(x, w, lng, lnb, tw, sb)


def _mix_in_decode_kernel(x_ref, w_ref, lng_ref, lnb_ref, w00_ref, b0_ref,
                          q_ref, k_ref, v_ref, ob_ref, vn_ref, *, w_a, w_b, qscale):
    xb = x_ref[...].astype(BF16)
    q_ref[...] = _dot(xb, w_ref[:, 0:w_a]) * qscale
    k_ref[...] = _dot(xb, w_ref[:, w_a:2 * w_a])
    v_ref[...] = _dot(xb, w_ref[:, 2 * w_a:3 * w_a])
    gu = _gelu(_dot(xb, w_ref[:, 3 * w_a:3 * w_a + w_b]))
    gv = _gelu(_dot(xb, w_ref[:, 3 * w_a + w_b:3 * w_a + 2 * w_b]))
    vn = _layer_norm(gv, lng_ref[...], lnb_ref[...])
    vn_ref[...] = vn
    ob_ref[...] = (gu * (w00_ref[...] * vn + b0_ref[...])).astype(BF16)


def _mix_in_decode(x, w, lng, lnb, w00, b0, *, w_a, w_b, qscale):
    n, d = x.shape
    kern = functools.partial(_mix_in_decode_kernel, w_a=w_a, w_b=w_b, qscale=qscale)
    full = lambda shape: pl.BlockSpec(shape, lambda i: (0,) * len(shape))
    return pl.pallas_call(
        kern,
        grid=(1,),
        in_specs=[full(x.shape), full(w.shape), full(lng.shape), full(lnb.shape),
                  full(w00.shape), full(b0.shape)],
        out_specs=[full((n, w_a)), full((n, w_a)), full((n, w_a)), full((n, w_b)), full((n, w_b))],
        out_shape=[jax.ShapeDtypeStruct((n, w_a), F32), jax.ShapeDtypeStruct((n, w_a), F32),
                   jax.ShapeDtypeStruct((n, w_a), F32), jax.ShapeDtypeStruct((n, w_b), BF16),
                   jax.ShapeDtypeStruct((n, w_b), F32)],
        compiler_params=_params(("arbitrary",), 32),
        name="mix_in_decode",
    )(x, w, lng, lnb, w00, b0)


def _attn_kernel(lam_ref, g_ref, q_ref, k_ref, v_ref, b0_ref, b1_ref, o_ref,
                 qq_ref, m_ref, acc_ref, p_last, *, t, n_heads, lambda_init):
    qi = pl.program_id(1)
    lane = lax.broadcasted_iota(jnp.int32, (t, E_A), 1)
    for h in range(n_heads):
        q = q_ref[:, h * E_A:(h + 1) * E_A]
        zero = jnp.zeros_like(q)
        qq_ref[2 * h] = jnp.where(lane < DH_A, q, zero)
        qq_ref[2 * h + 1] = jnp.where(lane < DH_A, zero, q)
    m_ref[...] = jnp.full_like(m_ref, NEG_INF)
    acc_ref[...] = jnp.zeros_like(acc_ref)
    ones = jnp.ones((t, LANES), BF16)

    last_u = 2 * n_heads - 1
    last_cols = slice((n_heads - 1) * E_A, n_heads * E_A)
    p_last[...] = jnp.zeros_like(p_last)

    def flush_last(j_done):
        off = pl.multiple_of(j_done * t, t)
        va = jnp.concatenate([v_ref[pl.ds(off, t), last_cols], ones], axis=1)
        acc_ref[last_u] += _dot(p_last[...], va)

    def update(u, blocks, va):
        lane_max = functools.reduce(jnp.maximum, blocks)
        row_max = jnp.max(lane_max, axis=-1, keepdims=True)
        m_old = m_ref[u]
        m_new = jnp.maximum(m_old, jnp.broadcast_to(row_max, m_old.shape))
        alpha = jnp.exp2(m_old - m_new)
        p = jnp.concatenate([jnp.exp2(blk - m_new).astype(BF16) for blk in blocks], axis=1)
        alpha2 = jnp.concatenate([alpha, alpha], axis=1)
        if u == last_u:
            acc_ref[u] = alpha2 * acc_ref[u]
            p_last[...] = p
        else:
            acc_ref[u] = alpha2 * acc_ref[u] + _dot(p, va)
        m_ref[u] = m_new

    def split(s):
        return [s[:, i * LANES:(i + 1) * LANES] for i in range(s.shape[1] // LANES)]

    def step(j, kind):
        flush_last(jnp.maximum(j - 1, 0))
        off = pl.multiple_of(j * t, t)
        if kind == "diag":
            visible = (lax.broadcasted_iota(jnp.int32, (t, t), 0)
                       >= lax.broadcasted_iota(jnp.int32, (t, t), 1))
        for h in range(n_heads):
            cols = slice(h * E_A, (h + 1) * E_A)
            kt = k_ref[pl.ds(off, t), cols]
            va = jnp.concatenate([v_ref[pl.ds(off, t), cols], ones], axis=1)
            for u in (2 * h, 2 * h + 1):
                s = _dot_nt(qq_ref[u], kt)
                if kind == "diag":
                    s = jnp.where(visible, s + b0_ref[h], NEG_INF)
                blocks = split(s)
                if kind == "prev":
                    corner = blocks[-1]
                    blocks[-1] = jnp.concatenate(
                        [corner[0:MAX_DIST] + b1_ref[h], corner[MAX_DIST:]], axis=0)
                update(u, blocks, va)

    def far(j, carry):
        step(j, "far")
        return carry

    lax.fori_loop(0, jnp.maximum(qi - 1, 0), far, 0)

    @pl.when(qi >= 1)
    def _():
        step(qi - 1, "prev")

    step(qi, "diag")
    flush_last(qi)

    lam = _diff_lambda(lam_ref, lambda_init)
    for h in range(n_heads):
        a0 = acc_ref[2 * h]
        a1 = acc_ref[2 * h + 1]
        o = a0[:, 0:E_A] / a0[:, E_A:2 * E_A] - lam * (a1[:, 0:E_A] / a1[:, E_A:2 * E_A])
        o_ref[:, h * E_A:(h + 1) * E_A] = _head_rmsnorm(o, g_ref[...], lambda_init).astype(BF16)


def _attention(lam4, subln_g, q, k, v, b0, b1, *, t, lambda_init):
    bsz, s, w_a = q.shape
    n_heads = w_a // E_A
    kern = functools.partial(_attn_kernel, t=t, n_heads=n_heads, lambda_init=lambda_init)
    const = lambda shape: pl.BlockSpec(shape, lambda b, i: (0,) * len(shape))
    seq_spec = pl.BlockSpec((None, s, w_a), lambda b, i: (b, 0, 0))
    return pl.pallas_call(
        kern,
        grid=(bsz, s // t),
        in_specs=[
            const(lam4.shape), const(subln_g.shape),
            pl.BlockSpec((None, t, w_a), lambda b, i: (b, i, 0)),
            seq_spec, seq_spec,
            _resident(b0.shape), _resident(b1.shape),
        ],
        out_specs=pl.BlockSpec((None, t, w_a), lambda b, i: (b, i, 0)),
        out_shape=jax.ShapeDtypeStruct((bsz, s, w_a), BF16),
        scratch_shapes=[pltpu.VMEM((2 * n_heads, t, E_A), BF16),
                        pltpu.VMEM((2 * n_heads, t, LANES), F32),
                        pltpu.VMEM((2 * n_heads, t, E_A + LANES), F32),
                        pltpu.VMEM((t, t), BF16)],
        compiler_params=_params(("parallel", "arbitrary"), 52),
        name="diff_attn_prompt",
    )(lam4, subln_g, q, k, v, b0, b1)


def _decode_attn_kernel(pt_ref, lam_ref, g_ref, q_ref, kn_ref, vn_ref, bias_ref, new_bias_ref, *rest,
                        pages, n_heads, lambda_init):
    del pt_ref
    k_refs = rest[:pages]
    v_refs = rest[pages:2 * pages]
    o_ref, kb_ref, vb_ref, m_ref, l_ref, acc_ref = rest[2 * pages:]
    j = pl.program_id(1)
    last = pl.num_programs(1) - 1
    rows = 2 * n_heads
    page_len = k_refs[0].shape[0]

    @pl.when(j == 0)
    def _():
        m_ref[...] = jnp.full_like(m_ref, NEG_INF)
        l_ref[...] = jnp.zeros_like(l_ref)
        acc_ref[...] = jnp.zeros_like(acc_ref)

    r_id = lax.broadcasted_iota(jnp.int32, (rows, E_A), 0)
    l_id = lax.broadcasted_iota(jnp.int32, (rows, E_A), 1)

    def per_row_head(tok_ref):
        out = jnp.zeros((rows, E_A), F32)
        for h in range(n_heads):
            piece = jnp.broadcast_to(tok_ref[:, h * E_A:(h + 1) * E_A], (rows, E_A))
            out = jnp.where((r_id // 2) == h, piece, out)
        return out

    qm = jnp.where((l_id // DH_A) == (r_id % 2), per_row_head(q_ref), 0.0)

    for i in range(pages):
        kb_ref[i * page_len:(i + 1) * page_len, :] = k_refs[i][...].astype(BF16)
        vb_ref[i * page_len:(i + 1) * page_len, :] = v_refs[i][...].astype(BF16)

    n_keys = pages * page_len
    s = _dot_nt(qm.astype(BF16), kb_ref[...])
    near = jnp.where(j == last, 1.0, 0.0)
    s_tail = s[:, n_keys - page_len:] + near * bias_ref[...]
    s = jnp.concatenate([s[:, :n_keys - page_len], s_tail], axis=1) if pages > 1 else s_tail
    row_s = lax.broadcasted_iota(jnp.int32, s.shape, 0)
    col_s = lax.broadcasted_iota(jnp.int32, s.shape, 1)
    s = jnp.where((col_s % n_heads) == (row_s // 2), s, NEG_INF)

    m_old = m_ref[...]
    m_new = jnp.maximum(m_old, jnp.max(s, axis=-1, keepdims=True))
    alpha = jnp.exp2(m_old - m_new)
    p = jnp.exp2(s - m_new)
    l_ref[...] = alpha * l_ref[...] + jnp.sum(p, axis=-1, keepdims=True)
    acc_ref[...] = alpha * acc_ref[...] + _dot(p.astype(BF16), vb_ref[...])
    m_ref[...] = m_new

    @pl.when(j == last)
    def _():
        s_new = (jnp.sum(qm * per_row_head(kn_ref), axis=-1, keepdims=True) + new_bias_ref[...])
        m_old = m_ref[...]
        m_new = jnp.maximum(m_old, s_new)
        alpha = jnp.exp2(m_old - m_new)
        p_new = jnp.exp2(s_new - m_new)
        l_fin = alpha * l_ref[...] + p_new
        o = (alpha * acc_ref[...] + p_new * per_row_head(vn_ref)) / l_fin
        lam = _diff_lambda(lam_ref, lambda_init)
        g = g_ref[...]
        for h in range(n_heads):
            oh = o[2 * h:2 * h + 1] - lam * o[2 * h + 1:2 * h + 2]
            o_ref[:, h * E_A:(h + 1) * E_A] = _head_rmsnorm(oh, g, lambda_init).astype(BF16)


def _decode_attention(page_table, lam4, subln_g, q, k_new, v_new, bias, new_bias, cache_k, cache_v, *,
                      layer, pages, lambda_init):
    n_seq, _, w_a = q.shape
    n_pages = page_table.shape[1]
    n_heads = w_a // E_A
    rows = 2 * n_heads
    page_len = cache_k.shape[2]
    kern = functools.partial(_decode_attn_kernel, pages=pages, n_heads=n_heads,
                             lambda_init=lambda_init)
    const = lambda shape: pl.BlockSpec(shape, lambda b, j, pt: (0,) * len(shape))
    tok = pl.BlockSpec((None, 1, w_a), lambda b, j, pt: (b, 0, 0))

    def page_spec(i):
        return pl.BlockSpec((None, None, page_len, E_A),
                            lambda b, j, pt: (layer, pt[b * n_pages + j * pages + i], 0, 0))

    grid_spec = pltpu.PrefetchScalarGridSpec(
        num_scalar_prefetch=1,
        grid=(n_seq, n_pages // pages),
        in_specs=[const(lam4.shape), const(subln_g.shape), tok, tok, tok, const(bias.shape),
                  const(new_bias.shape)]
        + [page_spec(i) for i in range(pages)] + [page_spec(i) for i in range(pages)],
        out_specs=pl.BlockSpec((None, 1, w_a), lambda b, j, pt: (b, 0, 0)),
        scratch_shapes=[pltpu.VMEM((pages * page_len, E_A), BF16),
                        pltpu.VMEM((pages * page_len, E_A), BF16),
                        pltpu.VMEM((rows, 1), F32), pltpu.VMEM((rows, 1), F32),
                        pltpu.VMEM((rows, E_A), F32)],
    )
    return pl.pallas_call(
        kern,
        grid_spec=grid_spec,
        out_shape=jax.ShapeDtypeStruct((n_seq, 1, w_a), BF16),
        compiler_params=_params(("parallel", "arbitrary"), 52),
        name="diff_attn_decode",
    )(page_table.reshape(-1), lam4, subln_g, q, k_new, v_new, bias, new_bias,
      *([cache_k] * pages), *([cache_v] * pages))


def _mix_out_kernel(x_ref, oa_ref, ob_ref, woa_ref, wob_ref, g_ref, b_ref, wq_ref,
                    x2_ref, qx_ref, *, alpha, qscale):
    y = _dot(oa_ref[...], woa_ref[...]) + _dot(ob_ref[...], wob_ref[...])
    x2 = _layer_norm(alpha * x_ref[...] + y, g_ref[...], b_ref[...])
    x2_ref[...] = x2
    qx_ref[...] = (_dot(x2.astype(BF16), wq_ref[...]) * qscale).astype(BF16)


def _mix_out(x, oa, ob, woa, wob, g, b, wq, *, alpha, qscale, tm):
    n, d = x.shape
    kern = functools.partial(_mix_out_kernel, alpha=alpha, qscale=qscale)
    row = lambda width: pl.BlockSpec((tm, width), lambda i: (i, 0))
    return pl.pallas_call(
        kern,
        grid=(n // tm,),
        in_specs=[row(d), row(oa.shape[1]), row(ob.shape[1]), _resident(woa.shape),
                  _resident(wob.shape), _resident(g.shape), _resident(b.shape), _resident(wq.shape)],
        out_specs=[row(d), row(d)],
        out_shape=[jax.ShapeDtypeStruct((n, d), F32), jax.ShapeDtypeStruct((n, d), BF16)],
        compiler_params=_params(("parallel",), 40),
        name="mix_out_ln_q",
    )(x, oa, ob, woa, wob, g, b, wq)


def _post_mix_kernel(x_ref, oa_ref, ob_ref, mk_ref, mv_ref, woa_ref, wob_ref, g1_ref, b1_ref,
                     wq_ref, wo_ref, g2_ref, b2_ref, y_ref, o_scr, x2_scr, xb_scr, q_scr,
                     *, alpha, qscale, n_heads):
    tm, d = x_ref.shape
    dh = d // n_heads
    subs = [slice(r0, r0 + SUB_ROWS) for r0 in range(0, tm, SUB_ROWS)]
    for rs in subs:
        y = _dot(oa_ref[rs, :], woa_ref[...]) + _dot(ob_ref[rs, :], wob_ref[...])
        x2 = _layer_norm(alpha * x_ref[rs, :] + y, g1_ref[...], b1_ref[...])
        x2_scr[rs, :] = x2
        xb_scr[rs, :] = x2.astype(BF16)
    for rs in subs:
        q_scr[rs, :] = (_dot(xb_scr[rs, :], wq_ref[...]) * qscale).astype(BF16)
    for h in range(n_heads):
        cols = slice(h * dh, (h + 1) * dh)
        for rs in subs:
            s = _dot_nt(q_scr[rs, cols], mk_ref[:, cols])
            p = jnp.exp2(s - jnp.max(s, axis=-1, keepdims=True))
            l = jnp.sum(p, axis=-1, keepdims=True)
            o_scr[rs, cols] = (_dot(p.astype(BF16), mv_ref[:, cols]) / l).astype(BF16)
    for rs in subs:
        y2 = _dot(o_scr[rs, :], wo_ref[...])
        y_ref[rs, :] = _layer_norm(alpha * x2_scr[rs, :] + y2, g2_ref[...], b2_ref[...])


def _post_mix(x, oa, ob, mk, mv, woa, wob, g1, b1, wq, wo, g2, b2, *, alpha, qscale, n_heads, tm):
    bsz, s, d = x.shape
    n_mem = mk.shape[1]
    kern = functools.partial(_post_mix_kernel, alpha=alpha, qscale=qscale, n_heads=n_heads)
    row = lambda width: pl.BlockSpec((None, tm, width), lambda b, i: (b, i, 0))
    mem = pl.BlockSpec((None, n_mem, d), lambda b, i: (b, 0, 0))
    weights = [woa, wob, g1, b1, wq, wo, g2, b2]
    return pl.pallas_call(
        kern,
        grid=(bsz, s // tm),
        in_specs=[row(d), row(oa.shape[2]), row(ob.shape[2]), mem, mem]
        + [_resident(w.shape) for w in weights],
        out_specs=row(d),
        out_shape=jax.ShapeDtypeStruct((bsz, s, d), F32),
        scratch_shapes=[pltpu.VMEM((tm, d), BF16), pltpu.VMEM((tm, d), F32),
                        pltpu.VMEM((tm, d), BF16), pltpu.VMEM((tm, d), BF16)],
        compiler_params=_params(("parallel", "parallel"), 48),
        name="post_mix",
    )(x, oa, ob, mk, mv, *weights)


def _cross_decode_kernel(q_ref, mk_ref, mv_ref, o_ref, *, n_heads):
    dh = mk_ref.shape[1]
    sublanes = 8
    q = q_ref[...].astype(F32)
    r_id = lax.broadcasted_iota(jnp.int32, (sublanes, dh), 0)
    qm = jnp.zeros((sublanes, dh), F32)
    for h in range(n_heads):
        piece = jnp.broadcast_to(q[:, h * dh:(h + 1) * dh], (sublanes, dh))
        qm = jnp.where((r_id % n_heads) == h, piece, qm)
    s = _dot_nt(qm.astype(BF16), mk_ref[...])
    row_s = lax.broadcasted_iota(jnp.int32, s.shape, 0)
    col_s = lax.broadcasted_iota(jnp.int32, s.shape, 1)
    s = jnp.where((col_s % n_heads) == (row_s % n_heads), s, NEG_INF)
    p = jnp.exp2(s - jnp.max(s, axis=-1, keepdims=True))
    l = jnp.sum(p, axis=-1, keepdims=True)
    o = _dot(p.astype(BF16), mv_ref[...]) / l
    for h in range(n_heads):
        o_ref[:, h * dh:(h + 1) * dh] = o[h:h + 1].astype(BF16)


def _cross_attention_decode(qx, mk, mv, *, layer, n_heads):
    n_seq, _, d = qx.shape
    mem_len, dh = mk.shape[2:]
    kern = functools.partial(_cross_decode_kernel, n_heads=n_heads)
    mem_spec = pl.BlockSpec((None, None, mem_len, dh), lambda b: (layer, b, 0, 0))
    return pl.pallas_call(
        kern,
        grid=(n_seq,),
        in_specs=[pl.BlockSpec((None, 1, d), lambda b: (b, 0, 0)), mem_spec, mem_spec],
        out_specs=pl.BlockSpec((None, 1, d), lambda b: (b, 0, 0)),
        out_shape=jax.ShapeDtypeStruct((n_seq, 1, d), BF16),
        compiler_params=_params(("parallel",), 32),
        name="cross_attn_decode",
    )(qx, mk, mv)


def _proj_ln_kernel(x_ref, o_ref, w_ref, g_ref, b_ref, y_ref, *, alpha):
    y = _dot(o_ref[...], w_ref[...])
    y_ref[...] = _layer_norm(alpha * x_ref[...] + y, g_ref[...], b_ref[...])


def _proj_ln(x, o, w, g, b, *, alpha, tm):
    n, d = x.shape
    kern = functools.partial(_proj_ln_kernel, alpha=alpha)
    row = pl.BlockSpec((tm, d), lambda i: (i, 0))
    return pl.pallas_call(
        kern,
        grid=(n // tm,),
        in_specs=[row, row, _resident(w.shape), _resident(g.shape), _resident(b.shape)],
        out_specs=row,
        out_shape=jax.ShapeDtypeStruct((n, d), F32),
        compiler_params=_params(("parallel",), 32),
        name="proj_ln",
    )(x, o, w, g, b)


def _mem_kv_kernel(m_ref, w_ref, k_ref, v_ref, kb_ref, vb_ref):
    d = kb_ref.shape[1]
    n_heads, dh = k_ref.shape[1:]
    mb = m_ref[...].astype(BF16)
    k = _dot(mb, w_ref[:, 0:d])
    v = _dot(mb, w_ref[:, d:2 * d])
    kb_ref[...] = k.astype(BF16)
    vb_ref[...] = v.astype(BF16)
    for h in range(n_heads):
        k_ref[:, h, :] = k[:, h * dh:(h + 1) * dh]
        v_ref[:, h, :] = v[:, h * dh:(h + 1) * dh]


def _mem_kv(mem, w, *, n_heads, tm):
    n, d = mem.shape
    dh = d // n_heads
    row = pl.BlockSpec((tm, d), lambda i: (i, 0))
    row_h = pl.BlockSpec((tm, n_heads, dh), lambda i: (i, 0, 0))
    return pl.pallas_call(
        _mem_kv_kernel,
        grid=(n // tm,),
        in_specs=[row, _resident(w.shape)],
        out_specs=[row_h, row_h, row, row],
        out_shape=[jax.ShapeDtypeStruct((n, n_heads, dh), F32)] * 2
        + [jax.ShapeDtypeStruct((n, d), BF16)] * 2,
        compiler_params=_params(("parallel",), 32),
        name="mem_kv",
    )(mem, w)


def _shifted_bias(table, dist):
    n = jnp.maximum(dist, 0)
    max_exact = NB // 2
    nf = jnp.maximum(n, 1).astype(F32)
    large = max_exact + (jnp.log(nf / max_exact) / math.log(MAX_DIST / max_exact)
                         * (NB - max_exact)).astype(jnp.int32)
    large = jnp.minimum(large, NB - 1)
    bucket = jnp.where(n < max_exact, n, large)
    tab = (table.astype(F32) - table[NB - 1].astype(F32)) * LOG2E
    hit = bucket[..., None, None] == jnp.arange(NB)[:, None]
    return jnp.sum(jnp.where(hit, tab, 0.0), axis=-2)


def _place_blocks(sel, tile):
    nb, n = sel.shape[0], tile.shape[1]
    out = sel[None, :, None, :, None] * tile[:, None, :, None, :]
    return out.reshape(tile.shape[0], nb * n, nb * n)


def _toeplitz(f_pos, f_neg, t):
    period = 2 * t
    v = jnp.concatenate([f_neg, jnp.zeros_like(f_pos[:1]), f_pos[:0:-1]], axis=0)
    tiled = jnp.tile(v, (t, 1))[: t * (period - 1)]
    skew = tiled.reshape(t, period - 1, -1)
    return jnp.transpose(skew[:, :t], (2, 0, 1))


def kernel(x_prompt, x_sample, mem_prompt, cache_k, cache_v, cache_mem_k, cache_mem_v, page_table, rel_bias, ln_g, ln_b, ffn1_w_in, ffn1_w_out, w_mix_in, w_mix_out, lambda_q1, lambda_k1, lambda_q2, lambda_k2, subln_g, sgu_ln_g, sgu_ln_b, sgu_w, sgu_b, xq_w, xkv_w, xo_w, ffn2_w_in, ffn2_w_out):
    bsz, seq, d = x_prompt.shape
    n_dec = x_sample.shape[0]
    depth = ln_g.shape[0]
    assert depth == 1 and x_sample.shape[1] == 1
    w_b = sgu_ln_g.shape[1]
    w_a = (w_mix_in.shape[2] - 2 * w_b) // 3
    n_heads = w_a // E_A
    n_groups = sgu_w.shape[1]
    n_mem, h_m, dh_m = cache_mem_k.shape[2:]
    page_rows = cache_k.shape[2]
    alpha = (2 * depth) ** 0.25
    lambda_init = 0.8 - 0.6 * math.exp(-0.3 * 0)
    l = 0
    t_attn = 512
    tm = 1024
    assert seq % tm == 0 and seq % t_attn == 0 and t_attn % MAX_DIST == 0 and tm % SUB_ROWS == 0
    assert SUB_ROWS % CHUNK == 0
    assert page_rows == CHUNK and MAX_DIST <= page_rows

    row2 = lambda a: a.reshape(1, -1)
    g_ln = [row2(ln_g[l, i]) for i in range(4)]
    b_ln = [row2(ln_b[l, i]) for i in range(4)]
    ffn1 = (ffn1_w_in[l].astype(BF16), ffn1_w_out[l].astype(BF16))
    ffn2 = (ffn2_w_in[l].astype(BF16), ffn2_w_out[l].astype(BF16))
    q_fold = jnp.concatenate([jnp.full((w_a,), DH_A ** -0.5, F32),
                              jnp.ones((w_mix_in.shape[2] - w_a,), F32)])
    w_mix = (w_mix_in[l] * q_fold).astype(BF16)
    woa = w_mix_out[l, :w_a].astype(BF16)
    wob = w_mix_out[l, w_a:].astype(BF16)
    wq_x = xq_w[l].astype(BF16)
    wo_x = xo_w[l].astype(BF16)
    wkv = xkv_w[l].astype(BF16)
    lam4 = jnp.stack([lambda_q1[l], lambda_k1[l], lambda_q2[l], lambda_k2[l]]).astype(F32)
    g_sub = row2(subln_g[l])
    lng_s, lnb_s = row2(sgu_ln_g[l]), row2(sgu_ln_b[l])
    tril_w = jnp.tril(sgu_w[l]).astype(BF16)
    sgu_bias = sgu_b[l].reshape(n_groups, CHUNK, 1)
    cg = w_b // n_groups
    w00 = jnp.repeat(sgu_w[l, :, 0, 0], cg).reshape(1, w_b)
    b00 = jnp.repeat(sgu_b[l, :, 0], cg).reshape(1, w_b)
    xq_scale = dh_m ** -0.5 * LOG2E

    ar = jnp.arange(MAX_DIST)
    near = _toeplitz(_shifted_bias(rel_bias, ar), _shifted_bias(rel_bias, 0 * ar), MAX_DIST)
    edge = _toeplitz(_shifted_bias(rel_bias, MAX_DIST + ar), _shifted_bias(rel_bias, MAX_DIST - ar),
                     MAX_DIST)
    nb = t_attn // MAX_DIST
    bias0 = (_place_blocks(jnp.eye(nb, dtype=F32), near)
             + _place_blocks(jnp.eye(nb, k=-1, dtype=F32), edge))
    dec_near = _shifted_bias(rel_bias, page_rows - jnp.arange(page_rows))
    dec_bias = jnp.repeat(jnp.repeat(dec_near.T, 2, axis=0), n_heads, axis=1)
    new_bias = jnp.repeat(_shifted_bias(rel_bias, jnp.zeros((1,), jnp.int32)).T, 2, axis=0)

    n_tok = bsz * seq
    xp = x_prompt.reshape(n_tok, d)
    mk_p, mv_p, mkb, mvb = _mem_kv(mem_prompt.reshape(bsz * n_mem, d), wkv, n_heads=h_m, tm=n_mem)
    x1 = _ffn_ln(xp, *ffn1, g_ln[0], b_ln[0], alpha=alpha, tm=tm)
    qb, k_p, kb, v_p, vb, ob = _mix_in(x1, w_mix, lng_s, lnb_s, tril_w, sgu_bias,
                                       w_a=w_a, w_b=w_b, qscale=LOG2E, tm=tm)
    oa = _attention(lam4, g_sub, qb.reshape(bsz, seq, w_a), kb.reshape(bsz, seq, w_a),
                    vb.reshape(bsz, seq, w_a), bias0, edge, t=t_attn, lambda_init=lambda_init)
    x3 = _post_mix(x1.reshape(bsz, seq, d), oa, ob.reshape(bsz, seq, w_b),
                   mkb.reshape(bsz, n_mem, d), mvb.reshape(bsz, n_mem, d),
                   woa, wob, g_ln[1], b_ln[1], wq_x, wo_x, g_ln[2], b_ln[2],
                   alpha=alpha, qscale=xq_scale, n_heads=h_m, tm=tm)
    y_p = _ffn_ln(x3.reshape(n_tok, d), *ffn2, g_ln[3], b_ln[3], alpha=alpha, tm=tm)

    xs = x_sample.reshape(n_dec, d)
    s1 = _ffn_ln(xs, *ffn1, g_ln[0], b_ln[0], alpha=alpha, tm=n_dec)
    q_s, k_s, v_s, ob_s, vn_s = _mix_in_decode(s1, w_mix, lng_s, lnb_s, w00, b00,
                                               w_a=w_a, w_b=w_b, qscale=LOG2E)
    oa_s = _decode_attention(page_table, lam4, g_sub, q_s.reshape(n_dec, 1, w_a),
                             k_s.reshape(n_dec, 1, w_a), v_s.reshape(n_dec, 1, w_a), dec_bias, new_bias,
                             cache_k.reshape(depth, -1, page_rows * n_heads, E_A),
                             cache_v.reshape(depth, -1, page_rows * n_heads, E_A),
                             layer=l, pages=32, lambda_init=lambda_init)
    s2, qx_s = _mix_out(s1, oa_s.reshape(n_dec, w_a), ob_s, woa, wob, g_ln[1], b_ln[1], wq_x,
                        alpha=alpha, qscale=xq_scale, tm=n_dec)
    ox_s = _cross_attention_decode(qx_s.reshape(n_dec, 1, d),
                                   cache_mem_k.reshape(depth, n_dec, n_mem * h_m, dh_m).astype(BF16),
                                   cache_mem_v.reshape(depth, n_dec, n_mem * h_m, dh_m).astype(BF16),
                                   layer=l, n_heads=h_m)
    s3 = _proj_ln(s2, ox_s.reshape(n_dec, d), wo_x, g_ln[2], b_ln[2], alpha=alpha, tm=n_dec)
    y_s = _ffn_ln(s3, *ffn2, g_ln[3], b_ln[3], alpha=alpha, tm=n_dec)

    return (y_p.reshape(bsz, seq, d), y_s.reshape(n_dec, 1, d),
            k_p.reshape(1, bsz, seq, n_heads, E_A), v_p.reshape(1, bsz, seq, n_heads, E_A),
            mk_p.reshape(1, bsz, n_mem, h_m, dh_m), mv_p.reshape(1, bsz, n_mem, h_m, dh_m),
            k_s.reshape(1, n_dec, 1, n_heads, E_A), v_s.reshape(1, n_dec, 1, n_heads, E_A),
            vn_s.reshape(1, n_dec, 1, w_b))
```

```python
import functools
import math

import jax
import jax.numpy as jnp
from jax import lax
from jax.experimental import pallas as pl
from jax.experimental.pallas import tpu as pltpu

F32 = jnp.float32
BF16 = jnp.bfloat16

LN_EPS = 1e-5
NEG_INF = -1e30
LOG2E = 1.4426950408889634
DH_A = 64
E_A = 2 * DH_A
CHUNK = 128
NB = 32
MAX_DIST = 128

LANES = 128
MXU_EDGE = 256
MIB = 1024 * 1024
SUB_ROWS = 256


def _params(semantics, vmem_mib):
    return pltpu.CompilerParams(dimension_semantics=semantics, vmem_limit_bytes=vmem_mib * MIB)


def _resident(shape):
    nd = len(shape)
    return pl.BlockSpec(shape, lambda *_: (0,) * nd, pipeline_mode=pl.Buffered(1))


def _layer_norm(x, g, b):
    mu = jnp.mean(x, -1, keepdims=True)
    xc = x - mu
    var = jnp.mean(xc * xc, -1, keepdims=True)
    return xc * lax.rsqrt(var + LN_EPS) * g + b


def _gelu(x):
    return 0.5 * x * (1.0 + lax.erf(x * math.sqrt(0.5)))


def _dot(a, b):
    return jnp.dot(a, b, preferred_element_type=F32)


def _dot_nt(a, b):
    return lax.dot_general(a, b, (((1,), (1,)), ((), ())), preferred_element_type=F32)


def _diff_lambda(lam_ref, lambda_init):
    lv = lam_ref[...]
    a = jnp.sum(lv[0:1] * lv[1:2], axis=-1, keepdims=True)
    b = jnp.sum(lv[2:3] * lv[3:4], axis=-1, keepdims=True)
    return jnp.exp(a) - jnp.exp(b) + lambda_init


def _head_rmsnorm(o, g, lambda_init):
    return o * lax.rsqrt(jnp.mean(o * o, -1, keepdims=True) + LN_EPS) * g * (1.0 - lambda_init)


def _ffn_ln_kernel(x_ref, wi_ref, wo_ref, g_ref, b_ref, o_ref, act_ref, xb_ref, *, alpha, fc):
    tm = x_ref.shape[0]
    sub = min(tm, SUB_ROWS)
    subs = [slice(r0, r0 + sub) for r0 in range(0, tm, sub)]
    xb_ref[...] = x_ref[...].astype(BF16)
    d_ff = wo_ref.shape[0]
    for j in range(d_ff // fc):
        cols = slice(j * fc, (j + 1) * fc)
        up_cols = slice(d_ff + j * fc, d_ff + (j + 1) * fc)
        for rs in subs:
            ha = _dot(xb_ref[rs, :], wi_ref[:, cols])
            hb = _dot(xb_ref[rs, :], wi_ref[:, up_cols])
            act_ref[rs, cols] = (ha * jax.nn.sigmoid(ha) * hb).astype(BF16)
    for rs in subs:
        y = _dot(act_ref[rs, :], wo_ref[...])
        o_ref[rs, :] = _layer_norm(alpha * x_ref[rs, :] + 0.5 * y, g_ref[...], b_ref[...])


def _ffn_ln(x, wi, wo, g, b, *, alpha, tm):
    n, d = x.shape
    d_ff = wo.shape[0]
    assert d_ff % MXU_EDGE == 0 and wi.shape[1] == 2 * d_ff
    kern = functools.partial(_ffn_ln_kernel, alpha=alpha, fc=MXU_EDGE)
    return pl.pallas_call(
        kern,
        grid=(n // tm,),
        in_specs=[
            pl.BlockSpec((tm, d), lambda i: (i, 0)),
            _resident(wi.shape), _resident(wo.shape), _resident(g.shape), _resident(b.shape),
        ],
        out_specs=pl.BlockSpec((tm, d), lambda i: (i, 0)),
        out_shape=jax.ShapeDtypeStruct((n, d), F32),
        scratch_shapes=[pltpu.VMEM((tm, d_ff), BF16), pltpu.VMEM((tm, d), BF16)],
        compiler_params=_params(("parallel",), 52),
        name="ffn_ln",
    )(x, wi, wo, g, b)


def _mix_in_kernel(x_ref, w_ref, lng_ref, lnb_ref, tw_ref, sb_ref,
                   q_ref, k_ref, kb_ref, v_ref, vb_ref, ob_ref, xb_ref, *, w_a, w_b, qscale):
    tm = x_ref.shape[0]
    n_groups = tw_ref.shape[0]
    cg = w_b // n_groups
    starts = list(range(0, tm, SUB_ROWS))
    gated = {}

    def gate_inputs(r0):
        rs = slice(r0, r0 + SUB_ROWS)
        xb_ref[rs, :] = x_ref[rs, :].astype(BF16)
        gv = _gelu(_dot(xb_ref[rs, :], w_ref[:, 3 * w_a + w_b:3 * w_a + 2 * w_b]))
        gu = _gelu(_dot(xb_ref[rs, :], w_ref[:, 3 * w_a:3 * w_a + w_b]))
        gated[r0] = (gu, _layer_norm(gv, lng_ref[...], lnb_ref[...]).astype(BF16))

    def qkv(r0):
        rs = slice(r0, r0 + SUB_ROWS)
        q_ref[rs, :] = (_dot(xb_ref[rs, :], w_ref[:, 0:w_a]) * qscale).astype(BF16)
        hk = _dot(xb_ref[rs, :], w_ref[:, w_a:2 * w_a])
        kb_ref[rs, :] = hk.astype(BF16)
        hv = _dot(xb_ref[rs, :], w_ref[:, 2 * w_a:3 * w_a])
        vb_ref[rs, :] = hv.astype(BF16)
        k_ref[rs] = hk.reshape(SUB_ROWS, w_a // E_A, E_A)
        v_ref[rs] = hv.reshape(SUB_ROWS, w_a // E_A, E_A)

    def gate(r0):
        gu, vn = gated.pop(r0)
        for c in range(SUB_ROWS // CHUNK):
            rows = slice(c * CHUNK, (c + 1) * CHUNK)
            out_rows = slice(r0 + c * CHUNK, r0 + (c + 1) * CHUNK)
            for g in range(n_groups):
                cols = slice(g * cg, (g + 1) * cg)
                mixed = _dot(tw_ref[g], vn[rows, cols]) + sb_ref[g]
                ob_ref[out_rows, cols] = (gu[rows, cols] * mixed).astype(BF16)

    gate_inputs(starts[0])
    for cur, nxt in zip(starts, starts[1:] + [None]):
        if nxt is not None:
            gate_inputs(nxt)
        qkv(cur)
        gate(cur)


def _mix_in(x, w, lng, lnb, tw, sb, *, w_a, w_b, qscale, tm):
    n, d = x.shape
    kern = functools.partial(_mix_in_kernel, w_a=w_a, w_b=w_b, qscale=qscale)
    row = lambda width: pl.BlockSpec((tm, width), lambda i: (i, 0))
    shp = lambda width, dt: jax.ShapeDtypeStruct((n, width), dt)
    n_heads = w_a // E_A
    row_h = pl.BlockSpec((tm, n_heads, E_A), lambda i: (i, 0, 0))
    shp_h = jax.ShapeDtypeStruct((n, n_heads, E_A), F32)
    return pl.pallas_call(
        kern,
        grid=(n // tm,),
        in_specs=[row(d), _resident(w.shape), _resident(lng.shape), _resident(lnb.shape),
                  _resident(tw.shape), _resident(sb.shape)],
        out_specs=[row(w_a), row_h, row(w_a), row_h, row(w_a), row(w_b)],
        out_shape=[shp(w_a, BF16), shp_h, shp(w_a, BF16), shp_h, shp(w_a, BF16), shp(w_b, BF16)],
        scratch_shapes=[pltpu.VMEM((tm, d), BF16)],
        compiler_params=_params(("parallel",), 48),
        name="mix_in_sgu",
    )(x, w, lng, lnb, tw, sb)


def _mix_in_decode_kernel(x_ref, w_ref, lng_ref, lnb_ref, w00_ref, b0_ref,
                          q_ref, k_ref, v_ref, ob_ref, vn_ref, *, w_a, w_b, qscale):
    xb = x_ref[...].astype(BF16)
    q_ref[...] = _dot(xb, w_ref[:, 0:w_a]) * qscale
    k_ref[...] = _dot(xb, w_ref[:, w_a:2 * w_a])
    v_ref[...] = _dot(xb, w_ref[:, 2 * w_a:3 * w_a])
    gu = _gelu(_dot(xb, w_ref[:, 3 * w_a:3 * w_a + w_b]))
    gv = _gelu(_dot(xb, w_ref[:, 3 * w_a + w_b:3 * w_a + 2 * w_b]))
    vn = _layer_norm(gv, lng_ref[...], lnb_ref[...])
    vn_ref[...] = vn
    ob_ref[...] = (gu * (w00_ref[...] * vn + b0_ref[...])).astype(BF16)


def _mix_in_decode(x, w, lng, lnb, w00, b0, *, w_a, w_b, qscale):
    n, d = x.shape
    kern = functools.partial(_mix_in_decode_kernel, w_a=w_a, w_b=w_b, qscale=qscale)
    full = lambda shape: pl.BlockSpec(shape, lambda i: (0,) * len(shape))
    return pl.pallas_call(
        kern,
        grid=(1,),
        in_specs=[full(x.shape), full(w.shape), full(lng.shape), full(lnb.shape),
                  full(w00.shape), full(b0.shape)],
        out_specs=[full((n, w_a)), full((n, w_a)), full((n, w_a)), full((n, w_b)), full((n, w_b))],
        out_shape=[jax.ShapeDtypeStruct((n, w_a), F32), jax.ShapeDtypeStruct((n, w_a), F32),
                   jax.ShapeDtypeStruct((n, w_a), F32), jax.ShapeDtypeStruct((n, w_b), BF16),
                   jax.ShapeDtypeStruct((n, w_b), F32)],
        compiler_params=_params(("arbitrary",), 32),
        name="mix_in_decode",
    )(x, w, lng, lnb, w00, b0)


def _attn_kernel(lam_ref, g_ref, q_ref, k_ref, v_ref, b0_ref, b1_ref, o_ref,
                 qq_ref, m_ref, acc_ref, p_last, *, t, n_heads, lambda_init):
    qi = pl.program_id(1)
    lane = lax.broadcasted_iota(jnp.int32, (t, E_A), 1)
    for h in range(n_heads):
        q = q_ref[:, h * E_A:(h + 1) * E_A]
        zero = jnp.zeros_like(q)
        qq_ref[2 * h] = jnp.where(lane < DH_A, q, zero)
        qq_ref[2 * h + 1] = jnp.where(lane < DH_A, zero, q)
    m_ref[...] = jnp.full_like(m_ref, NEG_INF)
    acc_ref[...] = jnp.zeros_like(acc_ref)
    ones = jnp.ones((t, LANES), BF16)

    last_u = 2 * n_heads - 1
    last_cols = slice((n_heads - 1) * E_A, n_heads * E_A)
    p_last[...] = jnp.zeros_like(p_last)

    def flush_last(j_done):
        off = pl.multiple_of(j_done * t, t)
        va = jnp.concatenate([v_ref[pl.ds(off, t), last_cols], ones], axis=1)
        acc_ref[last_u] += _dot(p_last[...], va)

    def update(u, blocks, va):
        lane_max = functools.reduce(jnp.maximum, blocks)
        row_max = jnp.max(lane_max, axis=-1, keepdims=True)
        m_old = m_ref[u]
        m_new = jnp.maximum(m_old, jnp.broadcast_to(row_max, m_old.shape))
        alpha = jnp.exp2(m_old - m_new)
        p = jnp.concatenate([jnp.exp2(blk - m_new).astype(BF16) for blk in blocks], axis=1)
        alpha2 = jnp.concatenate([alpha, alpha], axis=1)
        if u == last_u:
            acc_ref[u] = alpha2 * acc_ref[u]
            p_last[...] = p
        else:
            acc_ref[u] = alpha2 * acc_ref[u] + _dot(p, va)
        m_ref[u] = m_new

    def split(s):
        return [s[:, i * LANES:(i + 1) * LANES] for i in range(s.shape[1] // LANES)]

    def step(j, kind):
        flush_last(jnp.maximum(j - 1, 0))
        off = pl.multiple_of(j * t, t)
        if kind == "diag":
            visible = (lax.broadcasted_iota(jnp.int32, (t, t), 0)
                       >= lax.broadcasted_iota(jnp.int32, (t, t), 1))
        for h in range(n_heads):
            cols = slice(h * E_A, (h + 1) * E_A)
            kt = k_ref[pl.ds(off, t), cols]
            va = jnp.concatenate([v_ref[pl.ds(off, t), cols], ones], axis=1)
            for u in (2 * h, 2 * h + 1):
                s = _dot_nt(qq_ref[u], kt)
                if kind == "diag":
                    s = jnp.where(visible, s + b0_ref[h], NEG_INF)
                blocks = split(s)
                if kind == "prev":
                    corner = blocks[-1]
                    blocks[-1] = jnp.concatenate(
                        [corner[0:MAX_DIST] + b1_ref[h], corner[MAX_DIST:]], axis=0)
                update(u, blocks, va)

    def far(j, carry):
        step(j, "far")
        return carry

    lax.fori_loop(0, jnp.maximum(qi - 1, 0), far, 0)

    @pl.when(qi >= 1)
    def _():
        step(qi - 1, "prev")

    step(qi, "diag")
    flush_last(qi)

    lam = _diff_lambda(lam_ref, lambda_init)
    for h in range(n_heads):
        a0 = acc_ref[2 * h]
        a1 = acc_ref[2 * h + 1]
        o = a0[:, 0:E_A] / a0[:, E_A:2 * E_A] - lam * (a1[:, 0:E_A] / a1[:, E_A:2 * E_A])
        o_ref[:, h * E_A:(h + 1) * E_A] = _head_rmsnorm(o, g_ref[...], lambda_init).astype(BF16)


def _attention(lam4, subln_g, q, k, v, b0, b1, *, t, lambda_init):
    bsz, s, w_a = q.shape
    n_heads = w_a // E_A
    kern = functools.partial(_attn_kernel, t=t, n_heads=n_heads, lambda_init=lambda_init)
    const = lambda shape: pl.BlockSpec(shape, lambda b, i: (0,) * len(shape))
    seq_spec = pl.BlockSpec((None, s, w_a), lambda b, i: (b, 0, 0))
    return pl.pallas_call(
        kern,
        grid=(bsz, s // t),
        in_specs=[
            const(lam4.shape), const(subln_g.shape),
            pl.BlockSpec((None, t, w_a), lambda b, i: (b, i, 0)),
            seq_spec, seq_spec,
            _resident(b0.shape), _resident(b1.shape),
        ],
        out_specs=pl.BlockSpec((None, t, w_a), lambda b, i: (b, i, 0)),
        out_shape=jax.ShapeDtypeStruct((bsz, s, w_a), BF16),
        scratch_shapes=[pltpu.VMEM((2 * n_heads, t, E_A), BF16),
                        pltpu.VMEM((2 * n_heads, t, LANES), F32),
                        pltpu.VMEM((2 * n_heads, t, E_A + LANES), F32),
                        pltpu.VMEM((t, t), BF16)],
        compiler_params=_params(("parallel", "arbitrary"), 52),
        name="diff_attn_prompt",
    )(lam4, subln_g, q, k, v, b0, b1)


def _decode_attn_kernel(pt_ref, lam_ref, g_ref, q_ref, kn_ref, vn_ref, bias_ref, new_bias_ref, *rest,
                        pages, n_heads, lambda_init):
    del pt_ref
    k_refs = rest[:pages]
    v_refs = rest[pages:2 * pages]
    o_ref, kb_ref, vb_ref, m_ref, l_ref, acc_ref = rest[2 * pages:]
    j = pl.program_id(1)
    last = pl.num_programs(1) - 1
    rows = 2 * n_heads
    page_len = k_refs[0].shape[0]

    @pl.when(j == 0)
    def _():
        m_ref[...] = jnp.full_like(m_ref, NEG_INF)
        l_ref[...] = jnp.zeros_like(l_ref)
        acc_ref[...] = jnp.zeros_like(acc_ref)

    r_id = lax.broadcasted_iota(jnp.int32, (rows, E_A), 0)
    l_id = lax.broadcasted_iota(jnp.int32, (rows, E_A), 1)

    def per_row_head(tok_ref):
        out = jnp.zeros((rows, E_A), F32)
        for h in range(n_heads):
            piece = jnp.broadcast_to(tok_ref[:, h * E_A:(h + 1) * E_A], (rows, E_A))
            out = jnp.where((r_id // 2) == h, piece, out)
        return out

    qm = jnp.where((l_id // DH_A) == (r_id % 2), per_row_head(q_ref), 0.0)

    for i in range(pages):
        kb_ref[i * page_len:(i + 1) * page_len, :] = k_refs[i][...].astype(BF16)
        vb_ref[i * page_len:(i + 1) * page_len, :] = v_refs[i][...].astype(BF16)

    n_keys = pages * page_len
    s = _dot_nt(qm.astype(BF16), kb_ref[...])
    near = jnp.where(j == last, 1.0, 0.0)
    s_tail = s[:, n_keys - page_len:] + near * bias_ref[...]
    s = jnp.concatenate([s[:, :n_keys - page_len], s_tail], axis=1) if pages > 1 else s_tail
    row_s = lax.broadcasted_iota(jnp.int32, s.shape, 0)
    col_s = lax.broadcasted_iota(jnp.int32, s.shape, 1)
    s = jnp.where((col_s % n_heads) == (row_s // 2), s, NEG_INF)

    m_old = m_ref[...]
    m_new = jnp.maximum(m_old, jnp.max(s, axis=-1, keepdims=True))
    alpha = jnp.exp2(m_old - m_new)
    p = jnp.exp2(s - m_new)
    l_ref[...] = alpha * l_ref[...] + jnp.sum(p, axis=-1, keepdims=True)
    acc_ref[...] = alpha * acc_ref[...] + _dot(p.astype(BF16), vb_ref[...])
    m_ref[...] = m_new

    @pl.when(j == last)
    def _():
        s_new = (jnp.sum(qm * per_row_head(kn_ref), axis=-1, keepdims=True) + new_bias_ref[...])
        m_old = m_ref[...]
        m_new = jnp.maximum(m_old, s_new)
        alpha = jnp.exp2(m_old - m_new)
        p_new = jnp.exp2(s_new - m_new)
        l_fin = alpha * l_ref[...] + p_new
        o = (alpha * acc_ref[...] + p_new * per_row_head(vn_ref)) / l_fin
        lam = _diff_lambda(lam_ref, lambda_init)
        g = g_ref[...]
        for h in range(n_heads):
            oh = o[2 * h:2 * h + 1] - lam * o[2 * h + 1:2 * h + 2]
            o_ref[:, h * E_A:(h + 1) * E_A] = _head_rmsnorm(oh, g, lambda_init).astype(BF16)


def _decode_attention(page_table, lam4, subln_g, q, k_new, v_new, bias, new_bias, cache_k, cache_v, *,
                      layer, pages, lambda_init):
    n_seq, _, w_a = q.shape
    n_pages = page_table.shape[1]
    n_heads = w_a // E_A
    rows = 2 * n_heads
    page_len = cache_k.shape[2]
    kern = functools.partial(_decode_attn_kernel, pages=pages, n_heads=n_heads,
                             lambda_init=lambda_init)
    const = lambda shape: pl.BlockSpec(shape, lambda b, j, pt: (0,) * len(shape))
    tok = pl.BlockSpec((None, 1, w_a), lambda b, j, pt: (b, 0, 0))

    def page_spec(i):
        return pl.BlockSpec((None, None, page_len, E_A),
                            lambda b, j, pt: (layer, pt[b * n_pages + j * pages + i], 0, 0))

    grid_spec = pltpu.PrefetchScalarGridSpec(
        num_scalar_prefetch=1,
        grid=(n_seq, n_pages // pages),
        in_specs=[const(lam4.shape), const(subln_g.shape), tok, tok, tok, const(bias.shape),
                  const(new_bias.shape)]
        + [page_spec(i) for i in range(pages)] + [page_spec(i) for i in range(pages)],
        out_specs=pl.BlockSpec((None, 1, w_a), lambda b, j, pt: (b, 0, 0)),
        scratch_shapes=[pltpu.VMEM((pages * page_len, E_A), BF16),
                        pltpu.VMEM((pages * page_len, E_A), BF16),
                        pltpu.VMEM((rows, 1), F32), pltpu.VMEM((rows, 1), F32),
                        pltpu.VMEM((rows, E_A), F32)],
    )
    return pl.pallas_call(
        kern,
        grid_spec=grid_spec,
        out_shape=jax.ShapeDtypeStruct((n_seq, 1, w_a), BF16),
        compiler_params=_params(("parallel", "arbitrary"), 52),
        name="diff_attn_decode",
    )(page_table.reshape(-1), lam4, subln_g, q, k_new, v_new, bias, new_bias,
      *([cache_k] * pages), *([cache_v] * pages))


def _mix_out_kernel(x_ref, oa_ref, ob_ref, woa_ref, wob_ref, g_ref, b_ref, wq_ref,
                    x2_ref, qx_ref, *, alpha, qscale):
    y = _dot(oa_ref[...], woa_ref[...]) + _dot(ob_ref[...], wob_ref[...])
    x2 = _layer_norm(alpha * x_ref[...] + y, g_ref[...], b_ref[...])
    x2_ref[...] = x2
    qx_ref[...] = (_dot(x2.astype(BF16), wq_ref[...]) * qscale).astype(BF16)


def _mix_out(x, oa, ob, woa, wob, g, b, wq, *, alpha, qscale, tm):
    n, d = x.shape
    kern = functools.partial(_mix_out_kernel, alpha=alpha, qscale=qscale)
    row = lambda width: pl.BlockSpec((tm, width), lambda i: (i, 0))
    return pl.pallas_call(
        kern,
        grid=(n // tm,),
        in_specs=[row(d), row(oa.shape[1]), row(ob.shape[1]), _resident(woa.shape),
                  _resident(wob.shape), _resident(g.shape), _resident(b.shape), _resident(wq.shape)],
        out_specs=[row(d), row(d)],
        out_shape=[jax.ShapeDtypeStruct((n, d), F32), jax.ShapeDtypeStruct((n, d), BF16)],
        compiler_params=_params(("parallel",), 40),
        name="mix_out_ln_q",
    )(x, oa, ob, woa, wob, g, b, wq)


def _post_mix_kernel(x_ref, oa_ref, ob_ref, mk_ref, mv_ref, woa_ref, wob_ref, g1_ref, b1_ref,
                     wq_ref, wo_ref, g2_ref, b2_ref, y_ref, o_scr, x2_scr, xb_scr, q_scr,
                     *, alpha, qscale, n_heads):
    tm, d = x_ref.shape
    dh = d // n_heads
    subs = [slice(r0, r0 + SUB_ROWS) for r0 in range(0, tm, SUB_ROWS)]
    for rs in subs:
        y = _dot(oa_ref[rs, :], woa_ref[...]) + _dot(ob_ref[rs, :], wob_ref[...])
        x2 = _layer_norm(alpha * x_ref[rs, :] + y, g1_ref[...], b1_ref[...])
        x2_scr[rs, :] = x2
        xb_scr[rs, :] = x2.astype(BF16)
    for rs in subs:
        q_scr[rs, :] = (_dot(xb_scr[rs, :], wq_ref[...]) * qscale).astype(BF16)
    for h in range(n_heads):
        cols = slice(h * dh, (h + 1) * dh)
        for rs in subs:
            s = _dot_nt(q_scr[rs, cols], mk_ref[:, cols])
            p = jnp.exp2(s - jnp.max(s, axis=-1, keepdims=True))
            l = jnp.sum(p, axis=-1, keepdims=True)
            o_scr[rs, cols] = (_dot(p.astype(BF16), mv_ref[:, cols]) / l).astype(BF16)
    for rs in subs:
        y2 = _dot(o_scr[rs, :], wo_ref[...])
        y_ref[rs, :] = _layer_norm(alpha * x2_scr[rs, :] + y2, g2_ref[...], b2_ref[...])


def _post_mix(x, oa, ob, mk, mv, woa, wob, g1, b1, wq, wo, g2, b2, *, alpha, qscale, n_heads, tm):
    bsz, s, d = x.shape
    n_mem = mk.shape[1]
    kern = functools.partial(_post_mix_kernel, alpha=alpha, qscale=qscale, n_heads=n_heads)
    row = lambda width: pl.BlockSpec((None, tm, width), lambda b, i: (b, i, 0))
    mem = pl.BlockSpec((None, n_mem, d), lambda b, i: (b, 0, 0))
    weights = [woa, wob, g1, b1, wq, wo, g2, b2]
    return pl.pallas_call(
        kern,
        grid=(bsz, s // tm),
        in_specs=[row(d), row(oa.shape[2]), row(ob.shape[2]), mem, mem]
        + [_resident(w.shape) for w in weights],
        out_specs=row(d),
        out_shape=jax.ShapeDtypeStruct((bsz, s, d), F32),
        scratch_shapes=[pltpu.VMEM((tm, d), BF16), pltpu.VMEM((tm, d), F32),
                        pltpu.VMEM((tm, d), BF16), pltpu.VMEM((tm, d), BF16)],
        compiler_params=_params(("parallel", "parallel"), 48),
        name="post_mix",
    )(x, oa, ob, mk, mv, *weights)


def _cross_decode_kernel(q_ref, mk_ref, mv_ref, o_ref, *, n_heads):
    n_mem, _, dh = mk_ref.shape
    n_keys = n_mem * n_heads
    sublanes = 8
    q = q_ref[...].astype(F32)
    r_id = lax.broadcasted_iota(jnp.int32, (sublanes, LANES), 0)
    s = None
    for c in range(dh // LANES):
        qm = jnp.zeros((sublanes, LANES), F32)
        for h in range(n_heads):
            lo = h * dh + c * LANES
            piece = jnp.broadcast_to(q[:, lo:lo + LANES], (sublanes, LANES))
            qm = jnp.where((r_id % n_heads) == h, piece, qm)
        xk = mk_ref[:, :, c * LANES:(c + 1) * LANES].reshape(n_keys, LANES).astype(BF16)
        part = _dot_nt(qm.astype(BF16), xk)
        s = part if s is None else s + part
    row_s = lax.broadcasted_iota(jnp.int32, s.shape, 0)
    col_s = lax.broadcasted_iota(jnp.int32, s.shape, 1)
    s = jnp.where((col_s % n_heads) == (row_s % n_heads), s, NEG_INF)
    p = jnp.exp2(s - jnp.max(s, axis=-1, keepdims=True))
    l = jnp.sum(p, axis=-1, keepdims=True)
    pb = p.astype(BF16)
    for c in range(dh // LANES):
        xv = mv_ref[:, :, c * LANES:(c + 1) * LANES].reshape(n_keys, LANES).astype(BF16)
        o = _dot(pb, xv) / l
        for h in range(n_heads):
            lo = h * dh + c * LANES
            o_ref[:, lo:lo + LANES] = o[h:h + 1].astype(BF16)


def _cross_attention_decode(qx, mk, mv, *, layer, n_heads):
    n_seq, _, d = qx.shape
    n_mem, _, dh = mk.shape[2:]
    kern = functools.partial(_cross_decode_kernel, n_heads=n_heads)
    mem_spec = pl.BlockSpec((None, None, n_mem, n_heads, dh), lambda b: (layer, b, 0, 0, 0))
    return pl.pallas_call(
        kern,
        grid=(n_seq,),
        in_specs=[pl.BlockSpec((None, 1, d), lambda b: (b, 0, 0)), mem_spec, mem_spec],
        out_specs=pl.BlockSpec((None, 1, d), lambda b: (b, 0, 0)),
        out_shape=jax.ShapeDtypeStruct((n_seq, 1, d), BF16),
        compiler_params=_params(("parallel",), 32),
        name="cross_attn_decode",
    )(qx, mk, mv)


def _proj_ln_kernel(x_ref, o_ref, w_ref, g_ref, b_ref, y_ref, *, alpha):
    y = _dot(o_ref[...], w_ref[...])
    y_ref[...] = _layer_norm(alpha * x_ref[...] + y, g_ref[...], b_ref[...])


def _proj_ln(x, o, w, g, b, *, alpha, tm):
    n, d = x.shape
    kern = functools.partial(_proj_ln_kernel, alpha=alpha)
    row = pl.BlockSpec((tm, d), lambda i: (i, 0))
    return pl.pallas_call(
        kern,
        grid=(n // tm,),
        in_specs=[row, row, _resident(w.shape), _resident(g.shape), _resident(b.shape)],
        out_specs=row,
        out_shape=jax.ShapeDtypeStruct((n, d), F32),
        compiler_params=_params(("parallel",), 32),
        name="proj_ln",
    )(x, o, w, g, b)


def _mem_kv_kernel(m_ref, w_ref, k_ref, v_ref, kb_ref, vb_ref):
    d = kb_ref.shape[1]
    n_heads, dh = k_ref.shape[1:]
    mb = m_ref[...].astype(BF16)
    k = _dot(mb, w_ref[:, 0:d])
    v = _dot(mb, w_ref[:, d:2 * d])
    kb_ref[...] = k.astype(BF16)
    vb_ref[...] = v.astype(BF16)
    k_ref[...] = k.reshape(k.shape[0], n_heads, dh)
    v_ref[...] = v.reshape(v.shape[0], n_heads, dh)


def _mem_kv(mem, w, *, n_heads, tm):
    n, d = mem.shape
    dh = d // n_heads
    row = pl.BlockSpec((tm, d), lambda i: (i, 0))
    row_h = pl.BlockSpec((tm, n_heads, dh), lambda i: (i, 0, 0))
    return pl.pallas_call(
        _mem_kv_kernel,
        grid=(n // tm,),
        in_specs=[row, _resident(w.shape)],
        out_specs=[row_h, row_h, row, row],
        out_shape=[jax.ShapeDtypeStruct((n, n_heads, dh), F32)] * 2
        + [jax.ShapeDtypeStruct((n, d), BF16)] * 2,
        compiler_params=_params(("parallel",), 32),
        name="mem_kv",
    )(mem, w)


def _shifted_bias(table, dist):
    n = jnp.maximum(dist, 0)
    max_exact = NB // 2
    nf = jnp.maximum(n, 1).astype(F32)
    large = max_exact + (jnp.log(nf / max_exact) / math.log(MAX_DIST / max_exact)
                         * (NB - max_exact)).astype(jnp.int32)
    large = jnp.minimum(large, NB - 1)
    bucket = jnp.where(n < max_exact, n, large)
    tab = (table.astype(F32) - table[NB - 1].astype(F32)) * LOG2E
    hit = bucket[..., None, None] == jnp.arange(NB)[:, None]
    return jnp.sum(jnp.where(hit, tab, 0.0), axis=-2)


def _place_blocks(sel, tile):
    nb, n = sel.shape[0], tile.shape[1]
    out = sel[None, :, None, :, None] * tile[:, None, :, None, :]
    return out.reshape(tile.shape[0], nb * n, nb * n)


def _toeplitz(f_pos, f_neg, t):
    period = 2 * t
    v = jnp.concatenate([f_neg, jnp.zeros_like(f_pos[:1]), f_pos[:0:-1]], axis=0)
    tiled = jnp.tile(v, (t, 1))[: t * (period - 1)]
    skew = tiled.reshape(t, period - 1, -1)
    return jnp.transpose(skew[:, :t], (2, 0, 1))


def kernel(x_prompt, x_sample, mem_prompt, cache_k, cache_v, cache_mem_k, cache_mem_v, page_table, rel_bias, ln_g, ln_b, ffn1_w_in, ffn1_w_out, w_mix_in, w_mix_out, lambda_q1, lambda_k1, lambda_q2, lambda_k2, subln_g, sgu_ln_g, sgu_ln_b, sgu_w, sgu_b, xq_w, xkv_w, xo_w, ffn2_w_in, ffn2_w_out):
    bsz, seq, d = x_prompt.shape
    n_dec = x_sample.shape[0]
    depth = ln_g.shape[0]
    assert depth == 1 and x_sample.shape[1] == 1
    w_b = sgu_ln_g.shape[1]
    w_a = (w_mix_in.shape[2] - 2 * w_b) // 3
    n_heads = w_a // E_A
    n_groups = sgu_w.shape[1]
    n_mem, h_m, dh_m = cache_mem_k.shape[2:]
    page_rows = cache_k.shape[2]
    alpha = (2 * depth) ** 0.25
    lambda_init = 0.8 - 0.6 * math.exp(-0.3 * 0)
    l = 0
    t_attn = 512
    tm = 1024
    assert seq % tm == 0 and seq % t_attn == 0 and t_attn % MAX_DIST == 0 and tm % SUB_ROWS == 0
    assert SUB_ROWS % CHUNK == 0
    assert page_rows == CHUNK and MAX_DIST <= page_rows

    row2 = lambda a: a.reshape(1, -1)
    g_ln = [row2(ln_g[l, i]) for i in range(4)]
    b_ln = [row2(ln_b[l, i]) for i in range(4)]
    ffn1 = (ffn1_w_in[l].astype(BF16), ffn1_w_out[l].astype(BF16))
    ffn2 = (ffn2_w_in[l].astype(BF16), ffn2_w_out[l].astype(BF16))
    q_fold = jnp.concatenate([jnp.full((w_a,), DH_A ** -0.5, F32),
                              jnp.ones((w_mix_in.shape[2] - w_a,), F32)])
    w_mix = (w_mix_in[l] * q_fold).astype(BF16)
    woa = w_mix_out[l, :w_a].astype(BF16)
    wob = w_mix_out[l, w_a:].astype(BF16)
    wq_x = xq_w[l].astype(BF16)
    wo_x = xo_w[l].astype(BF16)
    wkv = xkv_w[l].astype(BF16)
    lam4 = jnp.stack([lambda_q1[l], lambda_k1[l], lambda_q2[l], lambda_k2[l]]).astype(F32)
    g_sub = row2(subln_g[l])
    lng_s, lnb_s = row2(sgu_ln_g[l]), row2(sgu_ln_b[l])
    tril_w = jnp.tril(sgu_w[l]).astype(BF16)
    sgu_bias = sgu_b[l].reshape(n_groups, CHUNK, 1)
    cg = w_b // n_groups
    w00 = jnp.repeat(sgu_w[l, :, 0, 0], cg).reshape(1, w_b)
    b00 = jnp.repeat(sgu_b[l, :, 0], cg).reshape(1, w_b)
    xq_scale = dh_m ** -0.5 * LOG2E

    ar = jnp.arange(MAX_DIST)
    near = _toeplitz(_shifted_bias(rel_bias, ar), _shifted_bias(rel_bias, 0 * ar), MAX_DIST)
    edge = _toeplitz(_shifted_bias(rel_bias, MAX_DIST + ar), _shifted_bias(rel_bias, MAX_DIST - ar),
                     MAX_DIST)
    nb = t_attn // MAX_DIST
    bias0 = (_place_blocks(jnp.eye(nb, dtype=F32), near)
             + _place_blocks(jnp.eye(nb, k=-1, dtype=F32), edge))
    dec_near = _shifted_bias(rel_bias, page_rows - jnp.arange(page_rows))
    dec_bias = jnp.repeat(jnp.repeat(dec_near.T, 2, axis=0), n_heads, axis=1)
    new_bias = jnp.repeat(_shifted_bias(rel_bias, jnp.zeros((1,), jnp.int32)).T, 2, axis=0)

    n_tok = bsz * seq
    xp = x_prompt.reshape(n_tok, d)
    mk_p, mv_p, mkb, mvb = _mem_kv(mem_prompt.reshape(bsz * n_mem, d), wkv, n_heads=h_m, tm=n_mem)
    x1 = _ffn_ln(xp, *ffn1, g_ln[0], b_ln[0], alpha=alpha, tm=tm)
    qb, k_p, kb, v_p, vb, ob = _mix_in(x1, w_mix, lng_s, lnb_s, tril_w, sgu_bias,
                                       w_a=w_a, w_b=w_b, qscale=LOG2E, tm=tm)
    oa = _attention(lam4, g_sub, qb.reshape(bsz, seq, w_a), kb.reshape(bsz, seq, w_a),
                    vb.reshape(bsz, seq, w_a), bias0, edge, t=t_attn, lambda_init=lambda_init)
    x3 = _post_mix(x1.reshape(bsz, seq, d), oa, ob.reshape(bsz, seq, w_b),
                   mkb.reshape(bsz, n_mem, d), mvb.reshape(bsz, n_mem, d),
                   woa, wob, g_ln[1], b_ln[1], wq_x, wo_x, g_ln[2], b_ln[2],
                   alpha=alpha, qscale=xq_scale, n_heads=h_m, tm=tm)
    y_p = _ffn_ln(x3.reshape(n_tok, d), *ffn2, g_ln[3], b_ln[3], alpha=alpha, tm=tm)

    xs = x_sample.reshape(n_dec, d)
    s1 = _ffn_ln(xs, *ffn1, g_ln[0], b_ln[0], alpha=alpha, tm=n_dec)
    q_s, k_s, v_s, ob_s, vn_s = _mix_in_decode(s1, w_mix, lng_s, lnb_s, w00, b00,
                                               w_a=w_a, w_b=w_b, qscale=LOG2E)
    oa_s = _decode_attention(page_table, lam4, g_sub, q_s.reshape(n_dec, 1, w_a),
                             k_s.reshape(n_dec, 1, w_a), v_s.reshape(n_dec, 1, w_a), dec_bias, new_bias,
                             cache_k.reshape(depth, -1, page_rows * n_heads, E_A),
                             cache_v.reshape(depth, -1, page_rows * n_heads, E_A),
                             layer=l, pages=32, lambda_init=lambda_init)
    s2, qx_s = _mix_out(s1, oa_s.reshape(n_dec, w_a), ob_s, woa, wob, g_ln[1], b_ln[1], wq_x,
                        alpha=alpha, qscale=xq_scale, tm=n_dec)
    ox_s = _cross_attention_decode(qx_s.reshape(n_dec, 1, d),
                                   cache_mem_k, cache_mem_v,
                                   layer=l, n_heads=h_m)
    s3 = _proj_ln(s2, ox_s.reshape(n_dec, d), wo_x, g_ln[2], b_ln[2], alpha=alpha, tm=n_dec)
    y_s = _ffn_ln(s3, *ffn2, g_ln[3], b_ln[3], alpha=alpha, tm=n_dec)

    return (y_p.reshape(bsz, seq, d), y_s.reshape(n_dec, 1, d),
            k_p.reshape(1, bsz, seq, n_heads, E_A), v_p.reshape(1, bsz, seq, n_heads, E_A),
            mk_p.reshape(1, bsz, n_mem, h_m, dh_m), mv_p.reshape(1, bsz, n_mem, h_m, dh_m),
            k_s.reshape(1, n_dec, 1, n_heads, E_A), v_s.reshape(1, n_dec, 1, n_heads, E_A),
            vn_s.reshape(1, n_dec, 1, w_b))
```

```python
import functools
import math

import jax
import jax.numpy as jnp
from jax import lax
from jax.experimental import pallas as pl
from jax.experimental.pallas import tpu as pltpu

F32 = jnp.float32
BF16 = jnp.bfloat16

LN_EPS = 1e-5
NEG_INF = -1e30
LOG2E = 1.4426950408889634
DH_A = 64
E_A = 2 * DH_A
CHUNK = 128
NB = 32
MAX_DIST = 128

LANES = 128
MXU_EDGE = 256
MIB = 1024 * 1024
SUB_ROWS = 256


def _params(semantics, vmem_mib):
    return pltpu.CompilerParams(dimension_semantics=semantics, vmem_limit_bytes=vmem_mib * MIB)


def _resident(shape):
    nd = len(shape)
    return pl.BlockSpec(shape, lambda *_: (0,) * nd, pipeline_mode=pl.Buffered(1))


def _layer_norm(x, g, b):
    mu = jnp.mean(x, -1, keepdims=True)
    xc = x - mu
    var = jnp.mean(xc * xc, -1, keepdims=True)
    return xc * lax.rsqrt(var + LN_EPS) * g + b


def _gelu(x):
    return 0.5 * x * (1.0 + lax.erf(x * math.sqrt(0.5)))


def _dot(a, b):
    return jnp.dot(a, b, preferred_element_type=F32)


def _dot_nt(a, b):
    return lax.dot_general(a, b, (((1,), (1,)), ((), ())), preferred_element_type=F32)


def _diff_lambda(lam_ref, lambda_init):
    lv = lam_ref[...]
    a = jnp.sum(lv[0:1] * lv[1:2], axis=-1, keepdims=True)
    b = jnp.sum(lv[2:3] * lv[3:4], axis=-1, keepdims=True)
    return jnp.exp(a) - jnp.exp(b) + lambda_init


def _head_rmsnorm(o, g, lambda_init):
    return o * lax.rsqrt(jnp.mean(o * o, -1, keepdims=True) + LN_EPS) * g * (1.0 - lambda_init)


def _ffn_ln_kernel(x_ref, wi_ref, wo_ref, g_ref, b_ref, o_ref, act_ref, xb_ref, *, alpha, fc):
    tm = x_ref.shape[0]
    sub = min(tm, SUB_ROWS)
    subs = [slice(r0, r0 + sub) for r0 in range(0, tm, sub)]
    xb_ref[...] = x_ref[...].astype(BF16)
    d_ff = wo_ref.shape[0]
    for j in range(d_ff // fc):
        cols = slice(j * fc, (j + 1) * fc)
        up_cols = slice(d_ff + j * fc, d_ff + (j + 1) * fc)
        for rs in subs:
            ha = _dot(xb_ref[rs, :], wi_ref[:, cols])
            hb = _dot(xb_ref[rs, :], wi_ref[:, up_cols])
            act_ref[rs, cols] = (ha * jax.nn.sigmoid(ha) * hb).astype(BF16)
    for rs in subs:
        y = _dot(act_ref[rs, :], wo_ref[...])
        o_ref[rs, :] = _layer_norm(alpha * x_ref[rs, :] + 0.5 * y, g_ref[...], b_ref[...])


def _ffn_ln(x, wi, wo, g, b, *, alpha, tm):
    n, d = x.shape
    d_ff = wo.shape[0]
    assert d_ff % MXU_EDGE == 0 and wi.shape[1] == 2 * d_ff
    kern = functools.partial(_ffn_ln_kernel, alpha=alpha, fc=MXU_EDGE)
    return pl.pallas_call(
        kern,
        grid=(n // tm,),
        in_specs=[
            pl.BlockSpec((tm, d), lambda i: (i, 0)),
            _resident(wi.shape), _resident(wo.shape), _resident(g.shape), _resident(b.shape),
        ],
        out_specs=pl.BlockSpec((tm, d), lambda i: (i, 0)),
        out_shape=jax.ShapeDtypeStruct((n, d), F32),
        scratch_shapes=[pltpu.VMEM((tm, d_ff), BF16), pltpu.VMEM((tm, d), BF16)],
        compiler_params=_params(("parallel",), 52),
        name="ffn_ln",
    )(x, wi, wo, g, b)


def _mix_in_kernel(x_ref, w_ref, lng_ref, lnb_ref, tw_ref, sb_ref,
                   q_ref, k_ref, kb_ref, v_ref, vb_ref, ob_ref, xb_ref, *, w_a, w_b, qscale):
    tm = x_ref.shape[0]
    n_groups = tw_ref.shape[0]
    cg = w_b // n_groups
    starts = list(range(0, tm, SUB_ROWS))
    gated = {}

    def gate_inputs(r0):
        rs = slice(r0, r0 + SUB_ROWS)
        xb_ref[rs, :] = x_ref[rs, :].astype(BF16)
        gv = _gelu(_dot(xb_ref[rs, :], w_ref[:, 3 * w_a + w_b:3 * w_a + 2 * w_b]))
        gu = _gelu(_dot(xb_ref[rs, :], w_ref[:, 3 * w_a:3 * w_a + w_b]))
        gated[r0] = (gu, _layer_norm(gv, lng_ref[...], lnb_ref[...]).astype(BF16))

    def qkv(r0):
        rs = slice(r0, r0 + SUB_ROWS)
        q_ref[rs, :] = (_dot(xb_ref[rs, :], w_ref[:, 0:w_a]) * qscale).astype(BF16)
        hk = _dot(xb_ref[rs, :], w_ref[:, w_a:2 * w_a])
        kb_ref[rs, :] = hk.astype(BF16)
        hv = _dot(xb_ref[rs, :], w_ref[:, 2 * w_a:3 * w_a])
        vb_ref[rs, :] = hv.astype(BF16)
        k_ref[rs] = hk.reshape(SUB_ROWS, w_a // E_A, E_A)
        v_ref[rs] = hv.reshape(SUB_ROWS, w_a // E_A, E_A)

    def gate(r0):
        gu, vn = gated.pop(r0)
        for c in range(SUB_ROWS // CHUNK):
            rows = slice(c * CHUNK, (c + 1) * CHUNK)
            out_rows = slice(r0 + c * CHUNK, r0 + (c + 1) * CHUNK)
            for g in range(n_groups):
                cols = slice(g * cg, (g + 1) * cg)
                mixed = _dot(tw_ref[g], vn[rows, cols]) + sb_ref[g]
                ob_ref[out_rows, cols] = (gu[rows, cols] * mixed).astype(BF16)

    gate_inputs(starts[0])
    for cur, nxt in zip(starts, starts[1:] + [None]):
        if nxt is not None:
            gate_inputs(nxt)
        qkv(cur)
        gate(cur)


def _mix_in(x, w, lng, lnb, tw, sb, *, w_a, w_b, qscale, tm):
    n, d = x.shape
    kern = functools.partial(_mix_in_kernel, w_a=w_a, w_b=w_b, qscale=qscale)
    row = lambda width: pl.BlockSpec((tm, width), lambda i: (i, 0))
    shp = lambda width, dt: jax.ShapeDtypeStruct((n, width), dt)
    n_heads = w_a // E_A
    row_h = pl.BlockSpec((tm, n_heads, E_A), lambda i: (i, 0, 0))
    shp_h = jax.ShapeDtypeStruct((n, n_heads, E_A), F32)
    return pl.pallas_call(
        kern,
        grid=(n // tm,),
        in_specs=[row(d), _resident(w.shape), _resident(lng.shape), _resident(lnb.shape),
                  _resident(tw.shape), _resident(sb.shape)],
        out_specs=[row(w_a), row_h, row(w_a), row_h, row(w_a), row(w_b)],
        out_shape=[shp(w_a, BF16), shp_h, shp(w_a, BF16), shp_h, shp(w_a, BF16), shp(w_b, BF16)],
        scratch_shapes=[pltpu.VMEM((tm, d), BF16)],
        compiler_params=_params(("parallel",), 48),
        name="mix_in_sgu",
    )(x, w, lng, lnb, tw, sb)


def _mix_in_decode_kernel(x_ref, w_ref, lng_ref, lnb_ref, w00_ref, b0_ref,
                          q_ref, k_ref, v_ref, ob_ref, vn_ref, *, w_a, w_b, qscale):
    xb = x_ref[...].astype(BF16)
    q_ref[...] = _dot(xb, w_ref[:, 0:w_a]) * qscale
    k_ref[...] = _dot(xb, w_ref[:, w_a:2 * w_a])
    v_ref[...] = _dot(xb, w_ref[:, 2 * w_a:3 * w_a])
    gu = _gelu(_dot(xb, w_ref[:, 3 * w_a:3 * w_a + w_b]))
    gv = _gelu(_dot(xb, w_ref[:, 3 * w_a + w_b:3 * w_a + 2 * w_b]))
    vn = _layer_norm(gv, lng_ref[...], lnb_ref[...])
    vn_ref[...] = vn
    ob_ref[...] = (gu * (w00_ref[...] * vn + b0_ref[...])).astype(BF16)


def _mix_in_decode(x, w, lng, lnb, w00, b0, *, w_a, w_b, qscale):
    n, d = x.shape
    kern = functools.partial(_mix_in_decode_kernel, w_a=w_a, w_b=w_b, qscale=qscale)
    full = lambda shape: pl.BlockSpec(shape, lambda i: (0,) * len(shape))
    return pl.pallas_call(
        kern,
        grid=(1,),
        in_specs=[full(x.shape), full(w.shape), full(lng.shape), full(lnb.shape),
                  full(w00.shape), full(b0.shape)],
        out_specs=[full((n, w_a)), full((n, w_a)), full((n, w_a)), full((n, w_b)), full((n, w_b))],
        out_shape=[jax.ShapeDtypeStruct((n, w_a), F32), jax.ShapeDtypeStruct((n, w_a), F32),
                   jax.ShapeDtypeStruct((n, w_a), F32), jax.ShapeDtypeStruct((n, w_b), BF16),
                   jax.ShapeDtypeStruct((n, w_b), F32)],
        compiler_params=_params(("arbitrary",), 32),
        name="mix_in_decode",
    )(x, w, lng, lnb, w00, b0)


def _attn_kernel(lam_ref, g_ref, q_ref, k_ref, v_ref, b0_ref, b1_ref, o_ref,
                 qq_ref, m_ref, acc_ref, p_last, *, t, n_heads, lambda_init):
    qi = pl.program_id(1)
    lane = lax.broadcasted_iota(jnp.int32, (t, E_A), 1)
    for h in range(n_heads):
        q = q_ref[:, h * E_A:(h + 1) * E_A]
        zero = jnp.zeros_like(q)
        qq_ref[2 * h] = jnp.where(lane < DH_A, q, zero)
        qq_ref[2 * h + 1] = jnp.where(lane < DH_A, zero, q)
    ones = jnp.ones((t, LANES), BF16)

    last_u = 2 * n_heads - 1
    last_cols = slice((n_heads - 1) * E_A, n_heads * E_A)

    def flush_last(j_done):
        off = pl.multiple_of(j_done * t, t)
        va = jnp.concatenate([v_ref[pl.ds(off, t), last_cols], ones], axis=1)
        acc_ref[last_u] += _dot(p_last[...], va)

    def update(u, blocks, va, first):
        lane_max = functools.reduce(jnp.maximum, blocks)
        row_max = jnp.broadcast_to(jnp.max(lane_max, axis=-1, keepdims=True), lane_max.shape)
        m_new = row_max if first else jnp.maximum(m_ref[u], row_max)
        p = jnp.concatenate([jnp.exp2(blk - m_new).astype(BF16) for blk in blocks], axis=1)
        if first:
            if u == last_u:
                acc_ref[u] = jnp.zeros(acc_ref.shape[1:], F32)
            else:
                acc_ref[u] = _dot(p, va)
        else:
            alpha = jnp.exp2(m_ref[u] - m_new)
            alpha2 = jnp.concatenate([alpha, alpha], axis=1)
            if u == last_u:
                acc_ref[u] = alpha2 * acc_ref[u]
            else:
                acc_ref[u] = alpha2 * acc_ref[u] + _dot(p, va)
        if u == last_u:
            p_last[...] = p
        m_ref[u] = m_new

    def split(s):
        return [s[:, i * LANES:(i + 1) * LANES] for i in range(s.shape[1] // LANES)]

    def step(j, kind, j_pending):
        if j_pending is not None:
            flush_last(j_pending)
        off = pl.multiple_of(j * t, t)
        if kind == "diag":
            visible = (lax.broadcasted_iota(jnp.int32, (t, t), 0)
                       >= lax.broadcasted_iota(jnp.int32, (t, t), 1))
        for h in range(n_heads):
            cols = slice(h * E_A, (h + 1) * E_A)
            kt = k_ref[pl.ds(off, t), cols]
            va = jnp.concatenate([v_ref[pl.ds(off, t), cols], ones], axis=1)
            for u in (2 * h, 2 * h + 1):
                s = _dot_nt(qq_ref[u], kt)
                if kind == "diag":
                    s = jnp.where(visible, s + b0_ref[h], NEG_INF)
                blocks = split(s)
                if kind == "prev":
                    corner = blocks[-1]
                    blocks[-1] = jnp.concatenate(
                        [corner[0:MAX_DIST] + b1_ref[h], corner[MAX_DIST:]], axis=0)
                update(u, blocks, va, first=(kind == "diag"))

    step(qi, "diag", None)

    @pl.when(qi >= 1)
    def _():
        step(qi - 1, "prev", qi)

    def far(j, carry):
        step(j, "far", jnp.where(j == 0, qi - 1, j - 1))
        return carry

    lax.fori_loop(0, jnp.maximum(qi - 1, 0), far, 0)
    flush_last(jnp.maximum(qi - 2, 0))

    lam = _diff_lambda(lam_ref, lambda_init)
    for h in range(n_heads):
        a0 = acc_ref[2 * h]
        a1 = acc_ref[2 * h + 1]
        o = a0[:, 0:E_A] / a0[:, E_A:2 * E_A] - lam * (a1[:, 0:E_A] / a1[:, E_A:2 * E_A])
        o_ref[:, h * E_A:(h + 1) * E_A] = _head_rmsnorm(o, g_ref[...], lambda_init).astype(BF16)


def _attention(lam4, subln_g, q, k, v, b0, b1, *, t, lambda_init):
    bsz, s, w_a = q.shape
    n_heads = w_a // E_A
    kern = functools.partial(_attn_kernel, t=t, n_heads=n_heads, lambda_init=lambda_init)
    const = lambda shape: pl.BlockSpec(shape, lambda b, i: (0,) * len(shape))
    seq_spec = pl.BlockSpec((None, s, w_a), lambda b, i: (b, 0, 0))
    return pl.pallas_call(
        kern,
        grid=(bsz, s // t),
        in_specs=[
            const(lam4.shape), const(subln_g.shape),
            pl.BlockSpec((None, t, w_a), lambda b, i: (b, i, 0)),
            seq_spec, seq_spec,
            _resident(b0.shape), _resident(b1.shape),
        ],
        out_specs=pl.BlockSpec((None, t, w_a), lambda b, i: (b, i, 0)),
        out_shape=jax.ShapeDtypeStruct((bsz, s, w_a), BF16),
        scratch_shapes=[pltpu.VMEM((2 * n_heads, t, E_A), BF16),
                        pltpu.VMEM((2 * n_heads, t, LANES), F32),
                        pltpu.VMEM((2 * n_heads, t, E_A + LANES), F32),
                        pltpu.VMEM((t, t), BF16)],
        compiler_params=_params(("parallel", "arbitrary"), 52),
        name="diff_attn_prompt",
    )(lam4, subln_g, q, k, v, b0, b1)


def _decode_attn_kernel(pt_ref, lam_ref, g_ref, q_ref, kn_ref, vn_ref, bias_ref, new_bias_ref, *rest,
                        pages, n_heads, lambda_init):
    del pt_ref
    k_refs = rest[:pages]
    v_refs = rest[pages:2 * pages]
    o_ref, kb_ref, vb_ref, m_ref, l_ref, acc_ref = rest[2 * pages:]
    j = pl.program_id(1)
    last = pl.num_programs(1) - 1
    rows = 2 * n_heads
    page_len = k_refs[0].shape[0]

    @pl.when(j == 0)
    def _():
        m_ref[...] = jnp.full_like(m_ref, NEG_INF)
        l_ref[...] = jnp.zeros_like(l_ref)
        acc_ref[...] = jnp.zeros_like(acc_ref)

    r_id = lax.broadcasted_iota(jnp.int32, (rows, E_A), 0)
    l_id = lax.broadcasted_iota(jnp.int32, (rows, E_A), 1)

    def per_row_head(tok_ref):
        out = jnp.zeros((rows, E_A), F32)
        for h in range(n_heads):
            piece = jnp.broadcast_to(tok_ref[:, h * E_A:(h + 1) * E_A], (rows, E_A))
            out = jnp.where((r_id // 2) == h, piece, out)
        return out

    qm = jnp.where((l_id // DH_A) == (r_id % 2), per_row_head(q_ref), 0.0)

    for i in range(pages):
        kb_ref[i * page_len:(i + 1) * page_len, :] = k_refs[i][...].astype(BF16)
        vb_ref[i * page_len:(i + 1) * page_len, :] = v_refs[i][...].astype(BF16)

    n_keys = pages * page_len
    s = _dot_nt(qm.astype(BF16), kb_ref[...])
    near = jnp.where(j == last, 1.0, 0.0)
    s_tail = s[:, n_keys - page_len:] + near * bias_ref[...]
    s = jnp.concatenate([s[:, :n_keys - page_len], s_tail], axis=1) if pages > 1 else s_tail
    row_s = lax.broadcasted_iota(jnp.int32, s.shape, 0)
    col_s = lax.broadcasted_iota(jnp.int32, s.shape, 1)
    s = jnp.where((col_s % n_heads) == (row_s // 2), s, NEG_INF)

    m_old = m_ref[...]
    m_new = jnp.maximum(m_old, jnp.max(s, axis=-1, keepdims=True))
    alpha = jnp.exp2(m_old - m_new)
    p = jnp.exp2(s - m_new)
    l_ref[...] = alpha * l_ref[...] + jnp.sum(p, axis=-1, keepdims=True)
    acc_ref[...] = alpha * acc_ref[...] + _dot(p.astype(BF16), vb_ref[...])
    m_ref[...] = m_new

    @pl.when(j == last)
    def _():
        s_new = (jnp.sum(qm * per_row_head(kn_ref), axis=-1, keepdims=True) + new_bias_ref[...])
        m_old = m_ref[...]
        m_new = jnp.maximum(m_old, s_new)
        alpha = jnp.exp2(m_old - m_new)
        p_new = jnp.exp2(s_new - m_new)
        l_fin = alpha * l_ref[...] + p_new
        o = (alpha * acc_ref[...] + p_new * per_row_head(vn_ref)) / l_fin
        lam = _diff_lambda(lam_ref, lambda_init)
        g = g_ref[...]
        for h in range(n_heads):
            oh = o[2 * h:2 * h + 1] - lam * o[2 * h + 1:2 * h + 2]
            o_ref[:, h * E_A:(h + 1) * E_A] = _head_rmsnorm(oh, g, lambda_init).astype(BF16)


def _decode_attention(page_table, lam4, subln_g, q, k_new, v_new, bias, new_bias, cache_k, cache_v, *,
                      layer, pages, lambda_init):
    n_seq, _, w_a = q.shape
    n_pages = page_table.shape[1]
    n_heads = w_a // E_A
    rows = 2 * n_heads
    page_len = cache_k.shape[2]
    kern = functools.partial(_decode_attn_kernel, pages=pages, n_heads=n_heads,
                             lambda_init=lambda_init)
    const = lambda shape: pl.BlockSpec(shape, lambda b, j, pt: (0,) * len(shape))
    tok = pl.BlockSpec((None, 1, w_a), lambda b, j, pt: (b, 0, 0))

    def page_spec(i):
        return pl.BlockSpec((None, None, page_len, E_A),
                            lambda b, j, pt: (layer, pt[b * n_pages + j * pages + i], 0, 0))

    grid_spec = pltpu.PrefetchScalarGridSpec(
        num_scalar_prefetch=1,
        grid=(n_seq, n_pages // pages),
        in_specs=[const(lam4.shape), const(subln_g.shape), tok, tok, tok, const(bias.shape),
                  const(new_bias.shape)]
        + [page_spec(i) for i in range(pages)] + [page_spec(i) for i in range(pages)],
        out_specs=pl.BlockSpec((None, 1, w_a), lambda b, j, pt: (b, 0, 0)),
        scratch_shapes=[pltpu.VMEM((pages * page_len, E_A), BF16),
                        pltpu.VMEM((pages * page_len, E_A), BF16),
                        pltpu.VMEM((rows, 1), F32), pltpu.VMEM((rows, 1), F32),
                        pltpu.VMEM((rows, E_A), F32)],
    )
    return pl.pallas_call(
        kern,
        grid_spec=grid_spec,
        out_shape=jax.ShapeDtypeStruct((n_seq, 1, w_a), BF16),
        compiler_params=_params(("parallel", "arbitrary"), 52),
        name="diff_attn_decode",
    )(page_table.reshape(-1), lam4, subln_g, q, k_new, v_new, bias, new_bias,
      *([cache_k] * pages), *([cache_v] * pages))


def _mix_out_kernel(x_ref, oa_ref, ob_ref, woa_ref, wob_ref, g_ref, b_ref, wq_ref,
                    x2_ref, qx_ref, *, alpha, qscale):
    y = _dot(oa_ref[...], woa_ref[...]) + _dot(ob_ref[...], wob_ref[...])
    x2 = _layer_norm(alpha * x_ref[...] + y, g_ref[...], b_ref[...])
    x2_ref[...] = x2
    qx_ref[...] = (_dot(x2.astype(BF16), wq_ref[...]) * qscale).astype(BF16)


def _mix_out(x, oa, ob, woa, wob, g, b, wq, *, alpha, qscale, tm):
    n, d = x.shape
    kern = functools.partial(_mix_out_kernel, alpha=alpha, qscale=qscale)
    row = lambda width: pl.BlockSpec((tm, width), lambda i: (i, 0))
    return pl.pallas_call(
        kern,
        grid=(n // tm,),
        in_specs=[row(d), row(oa.shape[1]), row(ob.shape[1]), _resident(woa.shape),
                  _resident(wob.shape), _resident(g.shape), _resident(b.shape), _resident(wq.shape)],
        out_specs=[row(d), row(d)],
        out_shape=[jax.ShapeDtypeStruct((n, d), F32), jax.ShapeDtypeStruct((n, d), BF16)],
        compiler_params=_params(("parallel",), 40),
        name="mix_out_ln_q",
    )(x, oa, ob, woa, wob, g, b, wq)


def _post_mix_kernel(x_ref, oa_ref, ob_ref, mk_ref, mv_ref, woa_ref, wob_ref, g1_ref, b1_ref,
                     wq_ref, wo_ref, g2_ref, b2_ref, y_ref, o_scr, x2_scr, xb_scr, q_scr,
                     *, alpha, qscale, n_heads):
    tm, d = x_ref.shape
    dh = d // n_heads
    subs = [slice(r0, r0 + SUB_ROWS) for r0 in range(0, tm, SUB_ROWS)]
    for rs in subs:
        y = _dot(oa_ref[rs, :], woa_ref[...]) + _dot(ob_ref[rs, :], wob_ref[...])
        x2 = _layer_norm(alpha * x_ref[rs, :] + y, g1_ref[...], b1_ref[...])
        x2_scr[rs, :] = x2
        xb_scr[rs, :] = x2.astype(BF16)
    for rs in subs:
        q_scr[rs, :] = (_dot(xb_scr[rs, :], wq_ref[...]) * qscale).astype(BF16)
    for h in range(n_heads):
        cols = slice(h * dh, (h + 1) * dh)
        for rs in subs:
            s = _dot_nt(q_scr[rs, cols], mk_ref[:, cols])
            p = jnp.exp2(s - jnp.max(s, axis=-1, keepdims=True))
            l = jnp.sum(p, axis=-1, keepdims=True)
            o_scr[rs, cols] = (_dot(p.astype(BF16), mv_ref[:, cols]) / l).astype(BF16)
    for rs in subs:
        y2 = _dot(o_scr[rs, :], wo_ref[...])
        y_ref[rs, :] = _layer_norm(alpha * x2_scr[rs, :] + y2, g2_ref[...], b2_ref[...])


def _post_mix(x, oa, ob, mk, mv, woa, wob, g1, b1, wq, wo, g2, b2, *, alpha, qscale, n_heads, tm):
    bsz, s, d = x.shape
    n_mem = mk.shape[1]
    kern = functools.partial(_post_mix_kernel, alpha=alpha, qscale=qscale, n_heads=n_heads)
    row = lambda width: pl.BlockSpec((None, tm, width), lambda b, i: (b, i, 0))
    mem = pl.BlockSpec((None, n_mem, d), lambda b, i: (b, 0, 0))
    weights = [woa, wob, g1, b1, wq, wo, g2, b2]
    return pl.pallas_call(
        kern,
        grid=(bsz, s // tm),
        in_specs=[row(d), row(oa.shape[2]), row(ob.shape[2]), mem, mem]
        + [_resident(w.shape) for w in weights],
        out_specs=row(d),
        out_shape=jax.ShapeDtypeStruct((bsz, s, d), F32),
        scratch_shapes=[pltpu.VMEM((tm, d), BF16), pltpu.VMEM((tm, d), F32),
                        pltpu.VMEM((tm, d), BF16), pltpu.VMEM((tm, d), BF16)],
        compiler_params=_params(("parallel", "parallel"), 48),
        name="post_mix",
    )(x, oa, ob, mk, mv, *weights)


def _cross_decode_kernel(q_ref, mk_ref, mv_ref, o_ref, *, n_heads):
    n_mem, _, dh = mk_ref.shape
    n_keys = n_mem * n_heads
    sublanes = 8
    q = q_ref[...].astype(F32)
    r_id = lax.broadcasted_iota(jnp.int32, (sublanes, LANES), 0)
    s = None
    for c in range(dh // LANES):
        qm = jnp.zeros((sublanes, LANES), F32)
        for h in range(n_heads):
            lo = h * dh + c * LANES
            piece = jnp.broadcast_to(q[:, lo:lo + LANES], (sublanes, LANES))
            qm = jnp.where((r_id % n_heads) == h, piece, qm)
        xk = mk_ref[:, :, c * LANES:(c + 1) * LANES].reshape(n_keys, LANES).astype(BF16)
        part = _dot_nt(qm.astype(BF16), xk)
        s = part if s is None else s + part
    row_s = lax.broadcasted_iota(jnp.int32, s.shape, 0)
    col_s = lax.broadcasted_iota(jnp.int32, s.shape, 1)
    s = jnp.where((col_s % n_heads) == (row_s % n_heads), s, NEG_INF)
    p = jnp.exp2(s - jnp.max(s, axis=-1, keepdims=True))
    l = jnp.sum(p, axis=-1, keepdims=True)
    pb = p.astype(BF16)
    for c in range(dh // LANES):
        xv = mv_ref[:, :, c * LANES:(c + 1) * LANES].reshape(n_keys, LANES).astype(BF16)
        o = _dot(pb, xv) / l
        for h in range(n_heads):
            lo = h * dh + c * LANES
            o_ref[:, lo:lo + LANES] = o[h:h + 1].astype(BF16)


def _cross_attention_decode(qx, mk, mv, *, layer, n_heads):
    n_seq, _, d = qx.shape
    n_mem, _, dh = mk.shape[2:]
    kern = functools.partial(_cross_decode_kernel, n_heads=n_heads)
    mem_spec = pl.BlockSpec((None, None, n_mem, n_heads, dh), lambda b: (layer, b, 0, 0, 0))
    return pl.pallas_call(
        kern,
        grid=(n_seq,),
        in_specs=[pl.BlockSpec((None, 1, d), lambda b: (b, 0, 0)), mem_spec, mem_spec],
        out_specs=pl.BlockSpec((None, 1, d), lambda b: (b, 0, 0)),
        out_shape=jax.ShapeDtypeStruct((n_seq, 1, d), BF16),
        compiler_params=_params(("parallel",), 32),
        name="cross_attn_decode",
    )(qx, mk, mv)


def _proj_ln_kernel(x_ref, o_ref, w_ref, g_ref, b_ref, y_ref, *, alpha):
    y = _dot(o_ref[...], w_ref[...])
    y_ref[...] = _layer_norm(alpha * x_ref[...] + y, g_ref[...], b_ref[...])


def _proj_ln(x, o, w, g, b, *, alpha, tm):
    n, d = x.shape
    kern = functools.partial(_proj_ln_kernel, alpha=alpha)
    row = pl.BlockSpec((tm, d), lambda i: (i, 0))
    return pl.pallas_call(
        kern,
        grid=(n // tm,),
        in_specs=[row, row, _resident(w.shape), _resident(g.shape), _resident(b.shape)],
        out_specs=row,
        out_shape=jax.ShapeDtypeStruct((n, d), F32),
        compiler_params=_params(("parallel",), 32),
        name="proj_ln",
    )(x, o, w, g, b)


def _mem_kv_kernel(m_ref, w_ref, k_ref, v_ref, kb_ref, vb_ref):
    d = kb_ref.shape[1]
    n_heads, dh = k_ref.shape[1:]
    mb = m_ref[...].astype(BF16)
    k = _dot(mb, w_ref[:, 0:d])
    v = _dot(mb, w_ref[:, d:2 * d])
    kb_ref[...] = k.astype(BF16)
    vb_ref[...] = v.astype(BF16)
    k_ref[...] = k.reshape(k.shape[0], n_heads, dh)
    v_ref[...] = v.reshape(v.shape[0], n_heads, dh)


def _mem_kv(mem, w, *, n_heads, tm):
    n, d = mem.shape
    dh = d // n_heads
    row = pl.BlockSpec((tm, d), lambda i: (i, 0))
    row_h = pl.BlockSpec((tm, n_heads, dh), lambda i: (i, 0, 0))
    return pl.pallas_call(
        _mem_kv_kernel,
        grid=(n // tm,),
        in_specs=[row, _resident(w.shape)],
        out_specs=[row_h, row_h, row, row],
        out_shape=[jax.ShapeDtypeStruct((n, n_heads, dh), F32)] * 2
        + [jax.ShapeDtypeStruct((n, d), BF16)] * 2,
        compiler_params=_params(("parallel",), 32),
        name="mem_kv",
    )(mem, w)


def _shifted_bias(table, dist):
    n = jnp.maximum(dist, 0)
    max_exact = NB // 2
    nf = jnp.maximum(n, 1).astype(F32)
    large = max_exact + (jnp.log(nf / max_exact) / math.log(MAX_DIST / max_exact)
                         * (NB - max_exact)).astype(jnp.int32)
    large = jnp.minimum(large, NB - 1)
    bucket = jnp.where(n < max_exact, n, large)
    tab = (table.astype(F32) - table[NB - 1].astype(F32)) * LOG2E
    hit = bucket[..., None, None] == jnp.arange(NB)[:, None]
    return jnp.sum(jnp.where(hit, tab, 0.0), axis=-2)


def _place_blocks(sel, tile):
    nb, n = sel.shape[0], tile.shape[1]
    out = sel[None, :, None, :, None] * tile[:, None, :, None, :]
    return out.reshape(tile.shape[0], nb * n, nb * n)


def _toeplitz(f_pos, f_neg, t):
    period = 2 * t
    v = jnp.concatenate([f_neg, jnp.zeros_like(f_pos[:1]), f_pos[:0:-1]], axis=0)
    tiled = jnp.tile(v, (t, 1))[: t * (period - 1)]
    skew = tiled.reshape(t, period - 1, -1)
    return jnp.transpose(skew[:, :t], (2, 0, 1))


def kernel(x_prompt, x_sample, mem_prompt, cache_k, cache_v, cache_mem_k, cache_mem_v, page_table, rel_bias, ln_g, ln_b, ffn1_w_in, ffn1_w_out, w_mix_in, w_mix_out, lambda_q1, lambda_k1, lambda_q2, lambda_k2, subln_g, sgu_ln_g, sgu_ln_b, sgu_w, sgu_b, xq_w, xkv_w, xo_w, ffn2_w_in, ffn2_w_out):
    bsz, seq, d = x_prompt.shape
    n_dec = x_sample.shape[0]
    depth = ln_g.shape[0]
    assert depth == 1 and x_sample.shape[1] == 1
    w_b = sgu_ln_g.shape[1]
    w_a = (w_mix_in.shape[2] - 2 * w_b) // 3
    n_heads = w_a // E_A
    n_groups = sgu_w.shape[1]
    n_mem, h_m, dh_m = cache_mem_k.shape[2:]
    page_rows = cache_k.shape[2]
    alpha = (2 * depth) ** 0.25
    lambda_init = 0.8 - 0.6 * math.exp(-0.3 * 0)
    l = 0
    t_attn = 512
    tm = 1024
    assert seq % tm == 0 and seq % t_attn == 0 and t_attn % MAX_DIST == 0 and tm % SUB_ROWS == 0
    assert SUB_ROWS % CHUNK == 0
    assert page_rows == CHUNK and MAX_DIST <= page_rows

    row2 = lambda a: a.reshape(1, -1)
    g_ln = [row2(ln_g[l, i]) for i in range(4)]
    b_ln = [row2(ln_b[l, i]) for i in range(4)]
    ffn1 = (ffn1_w_in[l].astype(BF16), ffn1_w_out[l].astype(BF16))
    ffn2 = (ffn2_w_in[l].astype(BF16), ffn2_w_out[l].astype(BF16))
    q_fold = jnp.concatenate([jnp.full((w_a,), DH_A ** -0.5, F32),
                              jnp.ones((w_mix_in.shape[2] - w_a,), F32)])
    w_mix = (w_mix_in[l] * q_fold).astype(BF16)
    woa = w_mix_out[l, :w_a].astype(BF16)
    wob = w_mix_out[l, w_a:].astype(BF16)
    wq_x = xq_w[l].astype(BF16)
    wo_x = xo_w[l].astype(BF16)
    wkv = xkv_w[l].astype(BF16)
    lam4 = jnp.stack([lambda_q1[l], lambda_k1[l], lambda_q2[l], lambda_k2[l]]).astype(F32)
    g_sub = row2(subln_g[l])
    lng_s, lnb_s = row2(sgu_ln_g[l]), row2(sgu_ln_b[l])
    tril_w = jnp.tril(sgu_w[l]).astype(BF16)
    sgu_bias = sgu_b[l].reshape(n_groups, CHUNK, 1)
    cg = w_b // n_groups
    w00 = jnp.repeat(sgu_w[l, :, 0, 0], cg).reshape(1, w_b)
    b00 = jnp.repeat(sgu_b[l, :, 0], cg).reshape(1, w_b)
    xq_scale = dh_m ** -0.5 * LOG2E

    ar = jnp.arange(MAX_DIST)
    near = _toeplitz(_shifted_bias(rel_bias, ar), _shifted_bias(rel_bias, 0 * ar), MAX_DIST)
    edge = _toeplitz(_shifted_bias(rel_bias, MAX_DIST + ar), _shifted_bias(rel_bias, MAX_DIST - ar),
                     MAX_DIST)
    nb = t_attn // MAX_DIST
    bias0 = (_place_blocks(jnp.eye(nb, dtype=F32), near)
             + _place_blocks(jnp.eye(nb, k=-1, dtype=F32), edge))
    dec_near = _shifted_bias(rel_bias, page_rows - jnp.arange(page_rows))
    dec_bias = jnp.repeat(jnp.repeat(dec_near.T, 2, axis=0), n_heads, axis=1)
    new_bias = jnp.repeat(_shifted_bias(rel_bias, jnp.zeros((1,), jnp.int32)).T, 2, axis=0)

    n_tok = bsz * seq
    xp = x_prompt.reshape(n_tok, d)
    mk_p, mv_p, mkb, mvb = _mem_kv(mem_prompt.reshape(bsz * n_mem, d), wkv, n_heads=h_m, tm=n_mem)
    x1 = _ffn_ln(xp, *ffn1, g_ln[0], b_ln[0], alpha=alpha, tm=tm)
    qb, k_p, kb, v_p, vb, ob = _mix_in(x1, w_mix, lng_s, lnb_s, tril_w, sgu_bias,
                                       w_a=w_a, w_b=w_b, qscale=LOG2E, tm=tm)
    oa = _attention(lam4, g_sub, qb.reshape(bsz, seq, w_a), kb.reshape(bsz, seq, w_a),
                    vb.reshape(bsz, seq, w_a), bias0, edge, t=t_attn, lambda_init=lambda_init)
    x3 = _post_mix(x1.reshape(bsz, seq, d), oa, ob.reshape(bsz, seq, w_b),
                   mkb.reshape(bsz, n_mem, d), mvb.reshape(bsz, n_mem, d),
                   woa, wob, g_ln[1], b_ln[1], wq_x, wo_x, g_ln[2], b_ln[2],
                   alpha=alpha, qscale=xq_scale, n_heads=h_m, tm=tm)
    y_p = _ffn_ln(x3.reshape(n_tok, d), *ffn2, g_ln[3], b_ln[3], alpha=alpha, tm=tm)

    xs = x_sample.reshape(n_dec, d)
    s1 = _ffn_ln(xs, *ffn1, g_ln[0], b_ln[0], alpha=alpha, tm=n_dec)
    q_s, k_s, v_s, ob_s, vn_s = _mix_in_decode(s1, w_mix, lng_s, lnb_s, w00, b00,
                                               w_a=w_a, w_b=w_b, qscale=LOG2E)
    oa_s = _decode_attention(page_table, lam4, g_sub, q_s.reshape(n_dec, 1, w_a),
                             k_s.reshape(n_dec, 1, w_a), v_s.reshape(n_dec, 1, w_a), dec_bias, new_bias,
                             cache_k.reshape(depth, -1, page_rows * n_heads, E_A),
                             cache_v.reshape(depth, -1, page_rows * n_heads, E_A),
                             layer=l, pages=32, lambda_init=lambda_init)
    s2, qx_s = _mix_out(s1, oa_s.reshape(n_dec, w_a), ob_s, woa, wob, g_ln[1], b_ln[1], wq_x,
                        alpha=alpha, qscale=xq_scale, tm=n_dec)
    ox_s = _cross_attention_decode(qx_s.reshape(n_dec, 1, d),
                                   cache_mem_k, cache_mem_v,
                                   layer=l, n_heads=h_m)
    s3 = _proj_ln(s2, ox_s.reshape(n_dec, d), wo_x, g_ln[2], b_ln[2], alpha=alpha, tm=n_dec)
    y_s = _ffn_ln(s3, *ffn2, g_ln[3], b_ln[3], alpha=alpha, tm=n_dec)

    return (y_p.reshape(bsz, seq, d), y_s.reshape(n_dec, 1, d),
            k_p.reshape(1, bsz, seq, n_heads, E_A), v_p.reshape(1, bsz, seq, n_heads, E_A),
            mk_p.reshape(1, bsz, n_mem, h_m, dh_m), mv_p.reshape(1, bsz, n_mem, h_m, dh_m),
            k_s.reshape(1, n_dec, 1, n_heads, E_A), v_s.reshape(1, n_dec, 1, n_heads, E_A),
            vn_s.reshape(1, n_dec, 1, w_b))
```

```python
import functools
import math

import jax
import jax.numpy as jnp
from jax import lax
from jax.experimental import pallas as pl
from jax.experimental.pallas import tpu as pltpu

F32 = jnp.float32
BF16 = jnp.bfloat16

LN_EPS = 1e-5
NEG_INF = -1e30
LOG2E = 1.4426950408889634
DH_A = 64
E_A = 2 * DH_A
CHUNK = 128
NB = 32
MAX_DIST = 128

LANES = 128
SUBLANES = 8
MXU_EDGE = 256
MIB = 1024 * 1024

TOKEN_TILE = 1024
SUB_ROWS = 256
ATTN_TILE = 512
DECODE_PAGES = 32


def _params(semantics, vmem_mib):
    return pltpu.CompilerParams(dimension_semantics=semantics, vmem_limit_bytes=vmem_mib * MIB)


def _resident(shape):
    nd = len(shape)
    return pl.BlockSpec(shape, lambda *_: (0,) * nd, pipeline_mode=pl.Buffered(1))


def _layer_norm(x, g, b):
    mu = jnp.mean(x, -1, keepdims=True)
    xc = x - mu
    var = jnp.mean(xc * xc, -1, keepdims=True)
    return xc * lax.rsqrt(var + LN_EPS) * g + b


def _gelu(x):
    return 0.5 * x * (1.0 + lax.erf(x * math.sqrt(0.5)))


def _dot(a, b):
    return jnp.dot(a, b, preferred_element_type=F32)


def _dot_nt(a, b):
    return lax.dot_general(a, b, (((1,), (1,)), ((), ())), preferred_element_type=F32)


def _diff_lambda(lam_ref, lambda_init):
    lv = lam_ref[...]
    a = jnp.sum(lv[0:1] * lv[1:2], axis=-1, keepdims=True)
    b = jnp.sum(lv[2:3] * lv[3:4], axis=-1, keepdims=True)
    return jnp.exp(a) - jnp.exp(b) + lambda_init


def _head_rmsnorm(o, g, lambda_init):
    return o * lax.rsqrt(jnp.mean(o * o, -1, keepdims=True) + LN_EPS) * g * (1.0 - lambda_init)


def _ffn_ln_kernel(x_ref, wi_ref, wo_ref, g_ref, b_ref, o_ref, act_ref, xb_ref, *, alpha, fc):
    tm = x_ref.shape[0]
    sub = min(tm, SUB_ROWS)
    subs = [slice(r0, r0 + sub) for r0 in range(0, tm, sub)]
    xb_ref[...] = x_ref[...].astype(BF16)
    d_ff = wo_ref.shape[0]
    for j in range(d_ff // fc):
        cols = slice(j * fc, (j + 1) * fc)
        up_cols = slice(d_ff + j * fc, d_ff + (j + 1) * fc)
        for rs in subs:
            ha = _dot(xb_ref[rs, :], wi_ref[:, cols])
            hb = _dot(xb_ref[rs, :], wi_ref[:, up_cols])
            act_ref[rs, cols] = (ha * jax.nn.sigmoid(ha) * hb).astype(BF16)
    for rs in subs:
        y = _dot(act_ref[rs, :], wo_ref[...])
        o_ref[rs, :] = _layer_norm(alpha * x_ref[rs, :] + 0.5 * y, g_ref[...], b_ref[...])


def _ffn_ln(x, wi, wo, g, b, *, alpha, tm):
    n, d = x.shape
    d_ff = wo.shape[0]
    assert d_ff % MXU_EDGE == 0 and wi.shape[1] == 2 * d_ff
    kern = functools.partial(_ffn_ln_kernel, alpha=alpha, fc=MXU_EDGE)
    return pl.pallas_call(
        kern,
        grid=(n // tm,),
        in_specs=[
            pl.BlockSpec((tm, d), lambda i: (i, 0)),
            _resident(wi.shape), _resident(wo.shape), _resident(g.shape), _resident(b.shape),
        ],
        out_specs=pl.BlockSpec((tm, d), lambda i: (i, 0)),
        out_shape=jax.ShapeDtypeStruct((n, d), F32),
        scratch_shapes=[pltpu.VMEM((tm, d_ff), BF16), pltpu.VMEM((tm, d), BF16)],
        compiler_params=_params(("parallel",), 52),
        name="ffn_ln",
    )(x, wi, wo, g, b)


def _mix_in_kernel(x_ref, w_ref, lng_ref, lnb_ref, tw_ref, sb_ref,
                   q_ref, k_ref, kb_ref, v_ref, vb_ref, ob_ref, xb_ref, *, w_a, w_b, qscale):
    tm = x_ref.shape[0]
    n_groups = tw_ref.shape[0]
    cg = w_b // n_groups
    starts = list(range(0, tm, SUB_ROWS))
    gated = {}

    def gate_inputs(r0):
        rs = slice(r0, r0 + SUB_ROWS)
        xb_ref[rs, :] = x_ref[rs, :].astype(BF16)
        gv = _gelu(_dot(xb_ref[rs, :], w_ref[:, 3 * w_a + w_b:3 * w_a + 2 * w_b]))
        gu = _gelu(_dot(xb_ref[rs, :], w_ref[:, 3 * w_a:3 * w_a + w_b]))
        gated[r0] = (gu, _layer_norm(gv, lng_ref[...], lnb_ref[...]).astype(BF16))

    def qkv(r0):
        rs = slice(r0, r0 + SUB_ROWS)
        q_ref[rs, :] = (_dot(xb_ref[rs, :], w_ref[:, 0:w_a]) * qscale).astype(BF16)
        hk = _dot(xb_ref[rs, :], w_ref[:, w_a:2 * w_a])
        kb_ref[rs, :] = hk.astype(BF16)
        hv = _dot(xb_ref[rs, :], w_ref[:, 2 * w_a:3 * w_a])
        vb_ref[rs, :] = hv.astype(BF16)
        k_ref[rs] = hk.reshape(SUB_ROWS, w_a // E_A, E_A)
        v_ref[rs] = hv.reshape(SUB_ROWS, w_a // E_A, E_A)

    def gate(r0):
        gu, vn = gated.pop(r0)
        for c in range(SUB_ROWS // CHUNK):
            rows = slice(c * CHUNK, (c + 1) * CHUNK)
            out_rows = slice(r0 + c * CHUNK, r0 + (c + 1) * CHUNK)
            for g in range(n_groups):
                cols = slice(g * cg, (g + 1) * cg)
                mixed = _dot(tw_ref[g], vn[rows, cols]) + sb_ref[g]
                ob_ref[out_rows, cols] = (gu[rows, cols] * mixed).astype(BF16)

    gate_inputs(starts[0])
    for cur, nxt in zip(starts, starts[1:] + [None]):
        if nxt is not None:
            gate_inputs(nxt)
        qkv(cur)
        gate(cur)


def _mix_in(x, w, lng, lnb, tw, sb, *, w_a, w_b, qscale, tm):
    n, d = x.shape
    kern = functools.partial(_mix_in_kernel, w_a=w_a, w_b=w_b, qscale=qscale)
    row = lambda width: pl.BlockSpec((tm, width), lambda i: (i, 0))
    shp = lambda width, dt: jax.ShapeDtypeStruct((n, width), dt)
    n_heads = w_a // E_A
    row_h = pl.BlockSpec((tm, n_heads, E_A), lambda i: (i, 0, 0))
    shp_h = jax.ShapeDtypeStruct((n, n_heads, E_A), F32)
    return pl.pallas_call(
        kern,
        grid=(n // tm,),
        in_specs=[row(d), _resident(w.shape), _resident(lng.shape), _resident(lnb.shape),
                  _resident(tw.shape), _resident(sb.shape)],
        out_specs=[row(w_a), row_h, row(w_a), row_h, row(w_a), row(w_b)],
        out_shape=[shp(w_a, BF16), shp_h, shp(w_a, BF16), shp_h, shp(w_a, BF16), shp(w_b, BF16)],
        scratch_shapes=[pltpu.VMEM((tm, d), BF16)],
        compiler_params=_params(("parallel",), 48),
        name="mix_in_sgu",
    )(x, w, lng, lnb, tw, sb)


def _mix_in_decode_kernel(x_ref, w_ref, lng_ref, lnb_ref, w00_ref, b0_ref,
                          q_ref, k_ref, v_ref, ob_ref, vn_ref, *, w_a, w_b, qscale):
    xb = x_ref[...].astype(BF16)
    q_ref[...] = _dot(xb, w_ref[:, 0:w_a]) * qscale
    k_ref[...] = _dot(xb, w_ref[:, w_a:2 * w_a])
    v_ref[...] = _dot(xb, w_ref[:, 2 * w_a:3 * w_a])
    gu = _gelu(_dot(xb, w_ref[:, 3 * w_a:3 * w_a + w_b]))
    gv = _gelu(_dot(xb, w_ref[:, 3 * w_a + w_b:3 * w_a + 2 * w_b]))
    vn = _layer_norm(gv, lng_ref[...], lnb_ref[...])
    vn_ref[...] = vn
    ob_ref[...] = (gu * (w00_ref[...] * vn + b0_ref[...])).astype(BF16)


def _mix_in_decode(x, w, lng, lnb, w00, b0, *, w_a, w_b, qscale):
    n, d = x.shape
    kern = functools.partial(_mix_in_decode_kernel, w_a=w_a, w_b=w_b, qscale=qscale)
    full = lambda shape: pl.BlockSpec(shape, lambda i: (0,) * len(shape))
    return pl.pallas_call(
        kern,
        grid=(1,),
        in_specs=[full(x.shape), full(w.shape), full(lng.shape), full(lnb.shape),
                  full(w00.shape), full(b0.shape)],
        out_specs=[full((n, w_a)), full((n, w_a)), full((n, w_a)), full((n, w_b)), full((n, w_b))],
        out_shape=[jax.ShapeDtypeStruct((n, w_a), F32), jax.ShapeDtypeStruct((n, w_a), F32),
                   jax.ShapeDtypeStruct((n, w_a), F32), jax.ShapeDtypeStruct((n, w_b), BF16),
                   jax.ShapeDtypeStruct((n, w_b), F32)],
        compiler_params=_params(("arbitrary",), 32),
        name="mix_in_decode",
    )(x, w, lng, lnb, w00, b0)


def _attn_kernel(lam_ref, g_ref, q_ref, k_ref, v_ref, b0_ref, b1_ref, o_ref,
                 qq_ref, m_ref, acc_ref, p_last, *, t, n_heads, lambda_init):
    qi = pl.program_id(1)
    lane = lax.broadcasted_iota(jnp.int32, (t, E_A), 1)
    for h in range(n_heads):
        q = q_ref[:, h * E_A:(h + 1) * E_A]
        zero = jnp.zeros_like(q)
        qq_ref[2 * h] = jnp.where(lane < DH_A, q, zero)
        qq_ref[2 * h + 1] = jnp.where(lane < DH_A, zero, q)
    ones = jnp.ones((t, LANES), BF16)

    last_u = 2 * n_heads - 1
    last_cols = slice((n_heads - 1) * E_A, n_heads * E_A)

    def flush_last(j_done):
        off = pl.multiple_of(j_done * t, t)
        va = jnp.concatenate([v_ref[pl.ds(off, t), last_cols], ones], axis=1)
        acc_ref[last_u] += _dot(p_last[...], va)

    def update(u, blocks, va, first):
        lane_max = functools.reduce(jnp.maximum, blocks)
        row_max = jnp.broadcast_to(jnp.max(lane_max, axis=-1, keepdims=True), lane_max.shape)
        m_new = row_max if first else jnp.maximum(m_ref[u], row_max)
        p = jnp.concatenate([jnp.exp2(blk - m_new).astype(BF16) for blk in blocks], axis=1)
        if first:
            if u == last_u:
                acc_ref[u] = jnp.zeros(acc_ref.shape[1:], F32)
            else:
                acc_ref[u] = _dot(p, va)
        else:
            alpha = jnp.exp2(m_ref[u] - m_new)
            alpha2 = jnp.concatenate([alpha, alpha], axis=1)
            if u == last_u:
                acc_ref[u] = alpha2 * acc_ref[u]
            else:
                acc_ref[u] = alpha2 * acc_ref[u] + _dot(p, va)
        if u == last_u:
            p_last[...] = p
        m_ref[u] = m_new

    def split(s):
        return [s[:, i * LANES:(i + 1) * LANES] for i in range(s.shape[1] // LANES)]

    def step(j, kind, j_pending):
        if j_pending is not None:
            flush_last(j_pending)
        off = pl.multiple_of(j * t, t)
        if kind == "diag":
            visible = (lax.broadcasted_iota(jnp.int32, (t, t), 0)
                       >= lax.broadcasted_iota(jnp.int32, (t, t), 1))
        for h in range(n_heads):
            cols = slice(h * E_A, (h + 1) * E_A)
            kt = k_ref[pl.ds(off, t), cols]
            va = jnp.concatenate([v_ref[pl.ds(off, t), cols], ones], axis=1)
            for u in (2 * h, 2 * h + 1):
                s = _dot_nt(qq_ref[u], kt)
                if kind == "diag":
                    s = jnp.where(visible, s + b0_ref[h], NEG_INF)
                blocks = split(s)
                if kind == "prev":
                    corner = blocks[-1]
                    blocks[-1] = jnp.concatenate(
                        [corner[0:MAX_DIST] + b1_ref[h], corner[MAX_DIST:]], axis=0)
                update(u, blocks, va, first=(kind == "diag"))

    step(qi, "diag", None)

    @pl.when(qi >= 1)
    def _():
        step(qi - 1, "prev", qi)

    def far(j, carry):
        step(j, "far", jnp.where(j == 0, qi - 1, j - 1))
        return carry

    lax.fori_loop(0, jnp.maximum(qi - 1, 0), far, 0)
    flush_last(jnp.maximum(qi - 2, 0))

    lam = _diff_lambda(lam_ref, lambda_init)
    for h in range(n_heads):
        a0 = acc_ref[2 * h]
        a1 = acc_ref[2 * h + 1]
        o = a0[:, 0:E_A] / a0[:, E_A:2 * E_A] - lam * (a1[:, 0:E_A] / a1[:, E_A:2 * E_A])
        o_ref[:, h * E_A:(h + 1) * E_A] = _head_rmsnorm(o, g_ref[...], lambda_init).astype(BF16)


def _attention(lam4, subln_g, q, k, v, b0, b1, *, t, lambda_init):
    bsz, s, w_a = q.shape
    n_heads = w_a // E_A
    kern = functools.partial(_attn_kernel, t=t, n_heads=n_heads, lambda_init=lambda_init)
    const = lambda shape: pl.BlockSpec(shape, lambda b, i: (0,) * len(shape))
    seq_spec = pl.BlockSpec((None, s, w_a), lambda b, i: (b, 0, 0))
    return pl.pallas_call(
        kern,
        grid=(bsz, s // t),
        in_specs=[
            const(lam4.shape), const(subln_g.shape),
            pl.BlockSpec((None, t, w_a), lambda b, i: (b, i, 0)),
            seq_spec, seq_spec,
            _resident(b0.shape), _resident(b1.shape),
        ],
        out_specs=pl.BlockSpec((None, t, w_a), lambda b, i: (b, i, 0)),
        out_shape=jax.ShapeDtypeStruct((bsz, s, w_a), BF16),
        scratch_shapes=[pltpu.VMEM((2 * n_heads, t, E_A), BF16),
                        pltpu.VMEM((2 * n_heads, t, LANES), F32),
                        pltpu.VMEM((2 * n_heads, t, E_A + LANES), F32),
                        pltpu.VMEM((t, t), BF16)],
        compiler_params=_params(("parallel", "arbitrary"), 52),
        name="diff_attn_prompt",
    )(lam4, subln_g, q, k, v, b0, b1)


def _decode_attn_kernel(pt_ref, lam_ref, g_ref, q_ref, kn_ref, vn_ref, bias_ref, new_bias_ref, *rest,
                        pages, n_heads, lambda_init):
    del pt_ref
    k_refs = rest[:pages]
    v_refs = rest[pages:2 * pages]
    o_ref, kb_ref, vb_ref, m_ref, l_ref, acc_ref = rest[2 * pages:]
    j = pl.program_id(1)
    last = pl.num_programs(1) - 1
    rows = 2 * n_heads
    page_len = k_refs[0].shape[0]

    @pl.when(j == 0)
    def _():
        m_ref[...] = jnp.full_like(m_ref, NEG_INF)
        l_ref[...] = jnp.zeros_like(l_ref)
        acc_ref[...] = jnp.zeros_like(acc_ref)

    r_id = lax.broadcasted_iota(jnp.int32, (rows, E_A), 0)
    l_id = lax.broadcasted_iota(jnp.int32, (rows, E_A), 1)

    def per_row_head(tok_ref):
        out = jnp.zeros((rows, E_A), F32)
        for h in range(n_heads):
            piece = jnp.broadcast_to(tok_ref[:, h * E_A:(h + 1) * E_A], (rows, E_A))
            out = jnp.where((r_id // 2) == h, piece, out)
        return out

    qm = jnp.where((l_id // DH_A) == (r_id % 2), per_row_head(q_ref), 0.0)

    for i in range(pages):
        kb_ref[i * page_len:(i + 1) * page_len, :] = k_refs[i][...].astype(BF16)
        vb_ref[i * page_len:(i + 1) * page_len, :] = v_refs[i][...].astype(BF16)

    n_keys = pages * page_len
    s = _dot_nt(qm.astype(BF16), kb_ref[...])
    near = jnp.where(j == last, 1.0, 0.0)
    s_tail = s[:, n_keys - page_len:] + near * bias_ref[...]
    s = jnp.concatenate([s[:, :n_keys - page_len], s_tail], axis=1) if pages > 1 else s_tail
    row_s = lax.broadcasted_iota(jnp.int32, s.shape, 0)
    col_s = lax.broadcasted_iota(jnp.int32, s.shape, 1)
    s = jnp.where((col_s % n_heads) == (row_s // 2), s, NEG_INF)

    m_old = m_ref[...]
    m_new = jnp.maximum(m_old, jnp.max(s, axis=-1, keepdims=True))
    alpha = jnp.exp2(m_old - m_new)
    p = jnp.exp2(s - m_new)
    l_ref[...] = alpha * l_ref[...] + jnp.sum(p, axis=-1, keepdims=True)
    acc_ref[...] = alpha * acc_ref[...] + _dot(p.astype(BF16), vb_ref[...])
    m_ref[...] = m_new

    @pl.when(j == last)
    def _():
        s_new = (jnp.sum(qm * per_row_head(kn_ref), axis=-1, keepdims=True) + new_bias_ref[...])
        m_old = m_ref[...]
        m_new = jnp.maximum(m_old, s_new)
        alpha = jnp.exp2(m_old - m_new)
        p_new = jnp.exp2(s_new - m_new)
        l_fin = alpha * l_ref[...] + p_new
        o = (alpha * acc_ref[...] + p_new * per_row_head(vn_ref)) / l_fin
        lam = _diff_lambda(lam_ref, lambda_init)
        g = g_ref[...]
        for h in range(n_heads):
            oh = o[2 * h:2 * h + 1] - lam * o[2 * h + 1:2 * h + 2]
            o_ref[:, h * E_A:(h + 1) * E_A] = _head_rmsnorm(oh, g, lambda_init).astype(BF16)


def _decode_attention(page_table, lam4, subln_g, q, k_new, v_new, bias, new_bias, cache_k, cache_v, *,
                      layer, pages, lambda_init):
    n_seq, _, w_a = q.shape
    n_pages = page_table.shape[1]
    assert n_pages % pages == 0
    n_heads = w_a // E_A
    rows = 2 * n_heads
    page_len = cache_k.shape[2]
    kern = functools.partial(_decode_attn_kernel, pages=pages, n_heads=n_heads,
                             lambda_init=lambda_init)
    const = lambda shape: pl.BlockSpec(shape, lambda b, j, pt: (0,) * len(shape))
    tok = pl.BlockSpec((None, 1, w_a), lambda b, j, pt: (b, 0, 0))

    def page_spec(i):
        return pl.BlockSpec((None, None, page_len, E_A),
                            lambda b, j, pt: (layer, pt[b * n_pages + j * pages + i], 0, 0))

    grid_spec = pltpu.PrefetchScalarGridSpec(
        num_scalar_prefetch=1,
        grid=(n_seq, n_pages // pages),
        in_specs=[const(lam4.shape), const(subln_g.shape), tok, tok, tok, const(bias.shape),
                  const(new_bias.shape)]
        + [page_spec(i) for i in range(pages)] + [page_spec(i) for i in range(pages)],
        out_specs=pl.BlockSpec((None, 1, w_a), lambda b, j, pt: (b, 0, 0)),
        scratch_shapes=[pltpu.VMEM((pages * page_len, E_A), BF16),
                        pltpu.VMEM((pages * page_len, E_A), BF16),
                        pltpu.VMEM((rows, 1), F32), pltpu.VMEM((rows, 1), F32),
                        pltpu.VMEM((rows, E_A), F32)],
    )
    return pl.pallas_call(
        kern,
        grid_spec=grid_spec,
        out_shape=jax.ShapeDtypeStruct((n_seq, 1, w_a), BF16),
        compiler_params=_params(("parallel", "arbitrary"), 52),
        name="diff_attn_decode",
    )(page_table.reshape(-1), lam4, subln_g, q, k_new, v_new, bias, new_bias,
      *([cache_k] * pages), *([cache_v] * pages))


def _mix_out_kernel(x_ref, oa_ref, ob_ref, woa_ref, wob_ref, g_ref, b_ref, wq_ref,
                    x2_ref, qx_ref, *, alpha, qscale):
    y = _dot(oa_ref[...], woa_ref[...]) + _dot(ob_ref[...], wob_ref[...])
    x2 = _layer_norm(alpha * x_ref[...] + y, g_ref[...], b_ref[...])
    x2_ref[...] = x2
    qx_ref[...] = (_dot(x2.astype(BF16), wq_ref[...]) * qscale).astype(BF16)


def _mix_out(x, oa, ob, woa, wob, g, b, wq, *, alpha, qscale, tm):
    n, d = x.shape
    kern = functools.partial(_mix_out_kernel, alpha=alpha, qscale=qscale)
    row = lambda width: pl.BlockSpec((tm, width), lambda i: (i, 0))
    return pl.pallas_call(
        kern,
        grid=(n // tm,),
        in_specs=[row(d), row(oa.shape[1]), row(ob.shape[1]), _resident(woa.shape),
                  _resident(wob.shape), _resident(g.shape), _resident(b.shape), _resident(wq.shape)],
        out_specs=[row(d), row(d)],
        out_shape=[jax.ShapeDtypeStruct((n, d), F32), jax.ShapeDtypeStruct((n, d), BF16)],
        compiler_params=_params(("parallel",), 40),
        name="mix_out_ln_q",
    )(x, oa, ob, woa, wob, g, b, wq)


def _post_mix_kernel(x_ref, oa_ref, ob_ref, mk_ref, mv_ref, woa_ref, wob_ref, g1_ref, b1_ref,
                     wq_ref, wo_ref, g2_ref, b2_ref, y_ref, o_scr, x2_scr, xb_scr, q_scr,
                     *, alpha, qscale, n_heads):
    tm, d = x_ref.shape
    dh = d // n_heads
    subs = [slice(r0, r0 + SUB_ROWS) for r0 in range(0, tm, SUB_ROWS)]
    for rs in subs:
        y = _dot(oa_ref[rs, :], woa_ref[...]) + _dot(ob_ref[rs, :], wob_ref[...])
        x2 = _layer_norm(alpha * x_ref[rs, :] + y, g1_ref[...], b1_ref[...])
        x2_scr[rs, :] = x2
        xb_scr[rs, :] = x2.astype(BF16)
    for rs in subs:
        q_scr[rs, :] = (_dot(xb_scr[rs, :], wq_ref[...]) * qscale).astype(BF16)
    for h in range(n_heads):
        cols = slice(h * dh, (h + 1) * dh)
        s = _dot_nt(q_scr[:, cols], mk_ref[:, cols])
        p = jnp.exp2(s - jnp.max(s, axis=-1, keepdims=True))
        l = jnp.sum(p, axis=-1, keepdims=True)
        o_scr[:, cols] = (_dot(p.astype(BF16), mv_ref[:, cols]) / l).astype(BF16)
    for rs in subs:
        y2 = _dot(o_scr[rs, :], wo_ref[...])
        y_ref[rs, :] = _layer_norm(alpha * x2_scr[rs, :] + y2, g2_ref[...], b2_ref[...])


def _post_mix(x, oa, ob, mk, mv, woa, wob, g1, b1, wq, wo, g2, b2, *, alpha, qscale, n_heads, tm):
    bsz, s, d = x.shape
    n_mem = mk.shape[1]
    kern = functools.partial(_post_mix_kernel, alpha=alpha, qscale=qscale, n_heads=n_heads)
    row = lambda width: pl.BlockSpec((None, tm, width), lambda b, i: (b, i, 0))
    mem = pl.BlockSpec((None, n_mem, d), lambda b, i: (b, 0, 0))
    weights = [woa, wob, g1, b1, wq, wo, g2, b2]
    return pl.pallas_call(
        kern,
        grid=(bsz, s // tm),
        in_specs=[row(d), row(oa.shape[2]), row(ob.shape[2]), mem, mem]
        + [_resident(w.shape) for w in weights],
        out_specs=row(d),
        out_shape=jax.ShapeDtypeStruct((bsz, s, d), F32),
        scratch_shapes=[pltpu.VMEM((tm, d), BF16), pltpu.VMEM((tm, d), F32),
                        pltpu.VMEM((tm, d), BF16), pltpu.VMEM((tm, d), BF16)],
        compiler_params=_params(("parallel", "parallel"), 48),
        name="post_mix",
    )(x, oa, ob, mk, mv, *weights)


def _cross_decode_kernel(q_ref, mk_ref, mv_ref, o_ref, *, n_heads):
    n_mem, _, dh = mk_ref.shape
    n_keys = n_mem * n_heads
    sublanes = SUBLANES
    assert n_heads <= sublanes
    q = q_ref[...].astype(F32)
    r_id = lax.broadcasted_iota(jnp.int32, (sublanes, LANES), 0)
    s = None
    for c in range(dh // LANES):
        qm = jnp.zeros((sublanes, LANES), F32)
        for h in range(n_heads):
            lo = h * dh + c * LANES
            piece = jnp.broadcast_to(q[:, lo:lo + LANES], (sublanes, LANES))
            qm = jnp.where((r_id % n_heads) == h, piece, qm)
        xk = mk_ref[:, :, c * LANES:(c + 1) * LANES].reshape(n_keys, LANES).astype(BF16)
        part = _dot_nt(qm.astype(BF16), xk)
        s = part if s is None else s + part
    row_s = lax.broadcasted_iota(jnp.int32, s.shape, 0)
    col_s = lax.broadcasted_iota(jnp.int32, s.shape, 1)
    s = jnp.where((col_s % n_heads) == (row_s % n_heads), s, NEG_INF)
    p = jnp.exp2(s - jnp.max(s, axis=-1, keepdims=True))
    l = jnp.sum(p, axis=-1, keepdims=True)
    pb = p.astype(BF16)
    for c in range(dh // LANES):
        xv = mv_ref[:, :, c * LANES:(c + 1) * LANES].reshape(n_keys, LANES).astype(BF16)
        o = _dot(pb, xv) / l
        for h in range(n_heads):
            lo = h * dh + c * LANES
            o_ref[:, lo:lo + LANES] = o[h:h + 1].astype(BF16)


def _cross_attention_decode(qx, mk, mv, *, layer, n_heads):
    n_seq, _, d = qx.shape
    n_mem, _, dh = mk.shape[2:]
    kern = functools.partial(_cross_decode_kernel, n_heads=n_heads)
    mem_spec = pl.BlockSpec((None, None, n_mem, n_heads, dh), lambda b: (layer, b, 0, 0, 0))
    return pl.pallas_call(
        kern,
        grid=(n_seq,),
        in_specs=[pl.BlockSpec((None, 1, d), lambda b: (b, 0, 0)), mem_spec, mem_spec],
        out_specs=pl.BlockSpec((None, 1, d), lambda b: (b, 0, 0)),
        out_shape=jax.ShapeDtypeStruct((n_seq, 1, d), BF16),
        compiler_params=_params(("parallel",), 32),
        name="cross_attn_decode",
    )(qx, mk, mv)


def _proj_ln_kernel(x_ref, o_ref, w_ref, g_ref, b_ref, y_ref, *, alpha):
    y = _dot(o_ref[...], w_ref[...])
    y_ref[...] = _layer_norm(alpha * x_ref[...] + y, g_ref[...], b_ref[...])


def _proj_ln(x, o, w, g, b, *, alpha, tm):
    n, d = x.shape
    kern = functools.partial(_proj_ln_kernel, alpha=alpha)
    row = pl.BlockSpec((tm, d), lambda i: (i, 0))
    return pl.pallas_call(
        kern,
        grid=(n // tm,),
        in_specs=[row, row, _resident(w.shape), _resident(g.shape), _resident(b.shape)],
        out_specs=row,
        out_shape=jax.ShapeDtypeStruct((n, d), F32),
        compiler_params=_params(("parallel",), 32),
        name="proj_ln",
    )(x, o, w, g, b)


def _mem_kv_kernel(m_ref, w_ref, k_ref, v_ref, kb_ref, vb_ref):
    d = kb_ref.shape[1]
    n_heads, dh = k_ref.shape[1:]
    mb = m_ref[...].astype(BF16)
    k = _dot(mb, w_ref[:, 0:d])
    v = _dot(mb, w_ref[:, d:2 * d])
    kb_ref[...] = k.astype(BF16)
    vb_ref[...] = v.astype(BF16)
    k_ref[...] = k.reshape(k.shape[0], n_heads, dh)
    v_ref[...] = v.reshape(v.shape[0], n_heads, dh)


def _mem_kv(mem, w, *, n_heads, tm):
    n, d = mem.shape
    dh = d // n_heads
    row = pl.BlockSpec((tm, d), lambda i: (i, 0))
    row_h = pl.BlockSpec((tm, n_heads, dh), lambda i: (i, 0, 0))
    return pl.pallas_call(
        _mem_kv_kernel,
        grid=(n // tm,),
        in_specs=[row, _resident(w.shape)],
        out_specs=[row_h, row_h, row, row],
        out_shape=[jax.ShapeDtypeStruct((n, n_heads, dh), F32)] * 2
        + [jax.ShapeDtypeStruct((n, d), BF16)] * 2,
        compiler_params=_params(("parallel",), 32),
        name="mem_kv",
    )(mem, w)


def _shifted_bias(table, dist):
    n = jnp.maximum(dist, 0)
    max_exact = NB // 2
    nf = jnp.maximum(n, 1).astype(F32)
    large = max_exact + (jnp.log(nf / max_exact) / math.log(MAX_DIST / max_exact)
                         * (NB - max_exact)).astype(jnp.int32)
    large = jnp.minimum(large, NB - 1)
    bucket = jnp.where(n < max_exact, n, large)
    tab = (table.astype(F32) - table[NB - 1].astype(F32)) * LOG2E
    hit = bucket[..., None, None] == jnp.arange(NB)[:, None]
    return jnp.sum(jnp.where(hit, tab, 0.0), axis=-2)


def _place_blocks(sel, tile):
    nb, n = sel.shape[0], tile.shape[1]
    out = sel[None, :, None, :, None] * tile[:, None, :, None, :]
    return out.reshape(tile.shape[0], nb * n, nb * n)


def _toeplitz(f_pos, f_neg, t):
    period = 2 * t
    v = jnp.concatenate([f_neg, jnp.zeros_like(f_pos[:1]), f_pos[:0:-1]], axis=0)
    tiled = jnp.tile(v, (t, 1))[: t * (period - 1)]
    skew = tiled.reshape(t, period - 1, -1)
    return jnp.transpose(skew[:, :t], (2, 0, 1))


def kernel(x_prompt, x_sample, mem_prompt, cache_k, cache_v, cache_mem_k, cache_mem_v, page_table, rel_bias, ln_g, ln_b, ffn1_w_in, ffn1_w_out, w_mix_in, w_mix_out, lambda_q1, lambda_k1, lambda_q2, lambda_k2, subln_g, sgu_ln_g, sgu_ln_b, sgu_w, sgu_b, xq_w, xkv_w, xo_w, ffn2_w_in, ffn2_w_out):
    bsz, seq, d = x_prompt.shape
    n_dec = x_sample.shape[0]
    depth = ln_g.shape[0]
    assert depth == 1 and x_sample.shape[1] == 1
    w_b = sgu_ln_g.shape[1]
    w_a = (w_mix_in.shape[2] - 2 * w_b) // 3
    n_heads = w_a // E_A
    n_groups = sgu_w.shape[1]
    n_mem, h_m, dh_m = cache_mem_k.shape[2:]
    page_rows = cache_k.shape[2]
    alpha = (2 * depth) ** 0.25
    lambda_init = 0.8 - 0.6 * math.exp(-0.3 * 0)
    l = 0
    t_attn = ATTN_TILE
    tm = TOKEN_TILE
    assert seq % tm == 0 and seq % t_attn == 0 and t_attn % MAX_DIST == 0 and tm % SUB_ROWS == 0
    assert SUB_ROWS % CHUNK == 0
    assert page_rows == CHUNK and MAX_DIST <= page_rows

    row2 = lambda a: a.reshape(1, -1)
    g_ln = [row2(ln_g[l, i]) for i in range(4)]
    b_ln = [row2(ln_b[l, i]) for i in range(4)]
    ffn1 = (ffn1_w_in[l].astype(BF16), ffn1_w_out[l].astype(BF16))
    ffn2 = (ffn2_w_in[l].astype(BF16), ffn2_w_out[l].astype(BF16))
    q_fold = jnp.concatenate([jnp.full((w_a,), DH_A ** -0.5, F32),
                              jnp.ones((w_mix_in.shape[2] - w_a,), F32)])
    w_mix = (w_mix_in[l] * q_fold).astype(BF16)
    woa = w_mix_out[l, :w_a].astype(BF16)
    wob = w_mix_out[l, w_a:].astype(BF16)
    wq_x = xq_w[l].astype(BF16)
    wo_x = xo_w[l].astype(BF16)
    wkv = xkv_w[l].astype(BF16)
    lam4 = jnp.stack([lambda_q1[l], lambda_k1[l], lambda_q2[l], lambda_k2[l]]).astype(F32)
    g_sub = row2(subln_g[l])
    lng_s, lnb_s = row2(sgu_ln_g[l]), row2(sgu_ln_b[l])
    tril_w = jnp.tril(sgu_w[l]).astype(BF16)
    sgu_bias = sgu_b[l].reshape(n_groups, CHUNK, 1)
    cg = w_b // n_groups
    w00 = jnp.repeat(sgu_w[l, :, 0, 0], cg).reshape(1, w_b)
    b00 = jnp.repeat(sgu_b[l, :, 0], cg).reshape(1, w_b)
    xq_scale = dh_m ** -0.5 * LOG2E

    ar = jnp.arange(MAX_DIST)
    near = _toeplitz(_shifted_bias(rel_bias, ar), _shifted_bias(rel_bias, 0 * ar), MAX_DIST)
    edge = _toeplitz(_shifted_bias(rel_bias, MAX_DIST + ar), _shifted_bias(rel_bias, MAX_DIST - ar),
                     MAX_DIST)
    nb = t_attn // MAX_DIST
    bias0 = (_place_blocks(jnp.eye(nb, dtype=F32), near)
             + _place_blocks(jnp.eye(nb, k=-1, dtype=F32), edge))
    dec_near = _shifted_bias(rel_bias, page_rows - jnp.arange(page_rows))
    dec_bias = jnp.repeat(jnp.repeat(dec_near.T, 2, axis=0), n_heads, axis=1)
    new_bias = jnp.repeat(_shifted_bias(rel_bias, jnp.zeros((1,), jnp.int32)).T, 2, axis=0)

    n_tok = bsz * seq
    xp = x_prompt.reshape(n_tok, d)
    mk_p, mv_p, mkb, mvb = _mem_kv(mem_prompt.reshape(bsz * n_mem, d), wkv, n_heads=h_m, tm=n_mem)
    x1 = _ffn_ln(xp, *ffn1, g_ln[0], b_ln[0], alpha=alpha, tm=tm)
    qb, k_p, kb, v_p, vb, ob = _mix_in(x1, w_mix, lng_s, lnb_s, tril_w, sgu_bias,
                                       w_a=w_a, w_b=w_b, qscale=LOG2E, tm=tm)
    oa = _attention(lam4, g_sub, qb.reshape(bsz, seq, w_a), kb.reshape(bsz, seq, w_a),
                    vb.reshape(bsz, seq, w_a), bias0, edge, t=t_attn, lambda_init=lambda_init)
    x3 = _post_mix(x1.reshape(bsz, seq, d), oa, ob.reshape(bsz, seq, w_b),
                   mkb.reshape(bsz, n_mem, d), mvb.reshape(bsz, n_mem, d),
                   woa, wob, g_ln[1], b_ln[1], wq_x, wo_x, g_ln[2], b_ln[2],
                   alpha=alpha, qscale=xq_scale, n_heads=h_m, tm=tm)
    y_p = _ffn_ln(x3.reshape(n_tok, d), *ffn2, g_ln[3], b_ln[3], alpha=alpha, tm=tm)

    xs = x_sample.reshape(n_dec, d)
    s1 = _ffn_ln(xs, *ffn1, g_ln[0], b_ln[0], alpha=alpha, tm=n_dec)
    q_s, k_s, v_s, ob_s, vn_s = _mix_in_decode(s1, w_mix, lng_s, lnb_s, w00, b00,
                                               w_a=w_a, w_b=w_b, qscale=LOG2E)
    oa_s = _decode_attention(page_table, lam4, g_sub, q_s.reshape(n_dec, 1, w_a),
                             k_s.reshape(n_dec, 1, w_a), v_s.reshape(n_dec, 1, w_a), dec_bias, new_bias,
                             cache_k.reshape(depth, -1, page_rows * n_heads, E_A),
                             cache_v.reshape(depth, -1, page_rows * n_heads, E_A),
                             layer=l, pages=DECODE_PAGES, lambda_init=lambda_init)
    s2, qx_s = _mix_out(s1, oa_s.reshape(n_dec, w_a), ob_s, woa, wob, g_ln[1], b_ln[1], wq_x,
                        alpha=alpha, qscale=xq_scale, tm=n_dec)
    ox_s = _cross_attention_decode(qx_s.reshape(n_dec, 1, d),
                                   cache_mem_k, cache_mem_v,
                                   layer=l, n_heads=h_m)
    s3 = _proj_ln(s2, ox_s.reshape(n_dec, d), wo_x, g_ln[2], b_ln[2], alpha=alpha, tm=n_dec)
    y_s = _ffn_ln(s3, *ffn2, g_ln[3], b_ln[3], alpha=alpha, tm=n_dec)

    return (y_p.reshape(bsz, seq, d), y_s.reshape(n_dec, 1, d),
            k_p.reshape(1, bsz, seq, n_heads, E_A), v_p.reshape(1, bsz, seq, n_heads, E_A),
            mk_p.reshape(1, bsz, n_mem, h_m, dh_m), mv_p.reshape(1, bsz, n_mem, h_m, dh_m),
            k_s.reshape(1, n_dec, 1, n_heads, E_A), v_s.reshape(1, n_dec, 1, n_heads, E_A),
            vn_s.reshape(1, n_dec, 1, w_b))
```

```python
import functools
import math

import jax
import jax.numpy as jnp
from jax import lax
from jax.experimental import pallas as pl
from jax.experimental.pallas import tpu as pltpu

F32 = jnp.float32
BF16 = jnp.bfloat16

LN_EPS = 1e-5
NEG_INF = -1e30
LOG2E = 1.4426950408889634
DH_A = 64
E_A = 2 * DH_A
CHUNK = 128
NB = 32
MAX_DIST = 128

LANES = 128
SUBLANES = 8
MXU_EDGE = 256
MIB = 1024 * 1024

TOKEN_TILE = 1024
SUB_ROWS = 256
ATTN_TILE = 512
DECODE_PAGES = 32
CROSS_DECODE_SEQS = 4


def _params(semantics, vmem_mib):
    return pltpu.CompilerParams(dimension_semantics=semantics, vmem_limit_bytes=vmem_mib * MIB)


def _resident(shape):
    nd = len(shape)
    return pl.BlockSpec(shape, lambda *_: (0,) * nd, pipeline_mode=pl.Buffered(1))


def _layer_norm(x, g, b):
    mu = jnp.mean(x, -1, keepdims=True)
    xc = x - mu
    var = jnp.mean(xc * xc, -1, keepdims=True)
    return xc * lax.rsqrt(var + LN_EPS) * g + b


def _gelu(x):
    return 0.5 * x * (1.0 + lax.erf(x * math.sqrt(0.5)))


def _dot(a, b):
    return jnp.dot(a, b, preferred_element_type=F32)


def _dot_nt(a, b):
    return lax.dot_general(a, b, (((1,), (1,)), ((), ())), preferred_element_type=F32)


def _diff_lambda(lam_ref, lambda_init):
    lv = lam_ref[...]
    a = jnp.sum(lv[0:1] * lv[1:2], axis=-1, keepdims=True)
    b = jnp.sum(lv[2:3] * lv[3:4], axis=-1, keepdims=True)
    return jnp.exp(a) - jnp.exp(b) + lambda_init


def _head_rmsnorm(o, g, lambda_init):
    return o * lax.rsqrt(jnp.mean(o * o, -1, keepdims=True) + LN_EPS) * g * (1.0 - lambda_init)


def _ffn_ln_kernel(x_ref, wi_ref, wo_ref, g_ref, b_ref, o_ref, act_ref, xb_ref, *, alpha, fc):
    tm = x_ref.shape[0]
    sub = min(tm, SUB_ROWS)
    subs = [slice(r0, r0 + sub) for r0 in range(0, tm, sub)]
    xb_ref[...] = x_ref[...].astype(BF16)
    d_ff = wo_ref.shape[0]
    for j in range(d_ff // fc):
        cols = slice(j * fc, (j + 1) * fc)
        up_cols = slice(d_ff + j * fc, d_ff + (j + 1) * fc)
        for rs in subs:
            ha = _dot(xb_ref[rs, :], wi_ref[:, cols])
            hb = _dot(xb_ref[rs, :], wi_ref[:, up_cols])
            act_ref[rs, cols] = (ha * jax.nn.sigmoid(ha) * hb).astype(BF16)
    for rs in subs:
        y = _dot(act_ref[rs, :], wo_ref[...])
        o_ref[rs, :] = _layer_norm(alpha * x_ref[rs, :] + 0.5 * y, g_ref[...], b_ref[...])


def _ffn_ln(x, wi, wo, g, b, *, alpha, tm):
    n, d = x.shape
    d_ff = wo.shape[0]
    assert d_ff % MXU_EDGE == 0 and wi.shape[1] == 2 * d_ff
    kern = functools.partial(_ffn_ln_kernel, alpha=alpha, fc=MXU_EDGE)
    return pl.pallas_call(
        kern,
        grid=(n // tm,),
        in_specs=[
            pl.BlockSpec((tm, d), lambda i: (i, 0)),
            _resident(wi.shape), _resident(wo.shape), _resident(g.shape), _resident(b.shape),
        ],
        out_specs=pl.BlockSpec((tm, d), lambda i: (i, 0)),
        out_shape=jax.ShapeDtypeStruct((n, d), F32),
        scratch_shapes=[pltpu.VMEM((tm, d_ff), BF16), pltpu.VMEM((tm, d), BF16)],
        compiler_params=_params(("parallel",), 52),
        name="ffn_ln",
    )(x, wi, wo, g, b)


def _mix_in_kernel(x_ref, w_ref, lng_ref, lnb_ref, tw_ref, sb_ref,
                   q_ref, k_ref, kb_ref, v_ref, vb_ref, ob_ref, xb_ref, *, w_a, w_b, qscale):
    tm = x_ref.shape[0]
    n_groups = tw_ref.shape[0]
    cg = w_b // n_groups
    starts = list(range(0, tm, SUB_ROWS))
    gated = {}

    def gate_inputs(r0):
        rs = slice(r0, r0 + SUB_ROWS)
        xb_ref[rs, :] = x_ref[rs, :].astype(BF16)
        gv = _gelu(_dot(xb_ref[rs, :], w_ref[:, 3 * w_a + w_b:3 * w_a + 2 * w_b]))
        gu = _gelu(_dot(xb_ref[rs, :], w_ref[:, 3 * w_a:3 * w_a + w_b]))
        gated[r0] = (gu, _layer_norm(gv, lng_ref[...], lnb_ref[...]).astype(BF16))

    def qkv(r0):
        rs = slice(r0, r0 + SUB_ROWS)
        q_ref[rs, :] = (_dot(xb_ref[rs, :], w_ref[:, 0:w_a]) * qscale).astype(BF16)
        hk = _dot(xb_ref[rs, :], w_ref[:, w_a:2 * w_a])
        kb_ref[rs, :] = hk.astype(BF16)
        hv = _dot(xb_ref[rs, :], w_ref[:, 2 * w_a:3 * w_a])
        vb_ref[rs, :] = hv.astype(BF16)
        k_ref[rs] = hk.reshape(SUB_ROWS, w_a // E_A, E_A)
        v_ref[rs] = hv.reshape(SUB_ROWS, w_a // E_A, E_A)

    def gate(r0):
        gu, vn = gated.pop(r0)
        for c in range(SUB_ROWS // CHUNK):
            rows = slice(c * CHUNK, (c + 1) * CHUNK)
            out_rows = slice(r0 + c * CHUNK, r0 + (c + 1) * CHUNK)
            for g in range(n_groups):
                cols = slice(g * cg, (g + 1) * cg)
                mixed = _dot(tw_ref[g], vn[rows, cols]) + sb_ref[g]
                ob_ref[out_rows, cols] = (gu[rows, cols] * mixed).astype(BF16)

    gate_inputs(starts[0])
    for cur, nxt in zip(starts, starts[1:] + [None]):
        if nxt is not None:
            gate_inputs(nxt)
        qkv(cur)
        gate(cur)


def _mix_in(x, w, lng, lnb, tw, sb, *, w_a, w_b, qscale, tm):
    n, d = x.shape
    kern = functools.partial(_mix_in_kernel, w_a=w_a, w_b=w_b, qscale=qscale)
    row = lambda width: pl.BlockSpec((tm, width), lambda i: (i, 0))
    shp = lambda width, dt: jax.ShapeDtypeStruct((n, width), dt)
    n_heads = w_a // E_A
    row_h = pl.BlockSpec((tm, n_heads, E_A), lambda i: (i, 0, 0))
    shp_h = jax.ShapeDtypeStruct((n, n_heads, E_A), F32)
    return pl.pallas_call(
        kern,
        grid=(n // tm,),
        in_specs=[row(d), _resident(w.shape), _resident(lng.shape), _resident(lnb.shape),
                  _resident(tw.shape), _resident(sb.shape)],
        out_specs=[row(w_a), row_h, row(w_a), row_h, row(w_a), row(w_b)],
        out_shape=[shp(w_a, BF16), shp_h, shp(w_a, BF16), shp_h, shp(w_a, BF16), shp(w_b, BF16)],
        scratch_shapes=[pltpu.VMEM((tm, d), BF16)],
        compiler_params=_params(("parallel",), 48),
        name="mix_in_sgu",
    )(x, w, lng, lnb, tw, sb)


def _mix_in_decode_kernel(x_ref, w_ref, lng_ref, lnb_ref, w00_ref, b0_ref,
                          q_ref, k_ref, v_ref, ob_ref, vn_ref, *, w_a, w_b, qscale):
    xb = x_ref[...].astype(BF16)
    q_ref[...] = _dot(xb, w_ref[:, 0:w_a]) * qscale
    k_ref[...] = _dot(xb, w_ref[:, w_a:2 * w_a])
    v_ref[...] = _dot(xb, w_ref[:, 2 * w_a:3 * w_a])
    gu = _gelu(_dot(xb, w_ref[:, 3 * w_a:3 * w_a + w_b]))
    gv = _gelu(_dot(xb, w_ref[:, 3 * w_a + w_b:3 * w_a + 2 * w_b]))
    vn = _layer_norm(gv, lng_ref[...], lnb_ref[...])
    vn_ref[...] = vn
    ob_ref[...] = (gu * (w00_ref[...] * vn + b0_ref[...])).astype(BF16)


def _mix_in_decode(x, w, lng, lnb, w00, b0, *, w_a, w_b, qscale):
    n, d = x.shape
    kern = functools.partial(_mix_in_decode_kernel, w_a=w_a, w_b=w_b, qscale=qscale)
    full = lambda shape: pl.BlockSpec(shape, lambda i: (0,) * len(shape))
    return pl.pallas_call(
        kern,
        grid=(1,),
        in_specs=[full(x.shape), full(w.shape), full(lng.shape), full(lnb.shape),
                  full(w00.shape), full(b0.shape)],
        out_specs=[full((n, w_a)), full((n, w_a)), full((n, w_a)), full((n, w_b)), full((n, w_b))],
        out_shape=[jax.ShapeDtypeStruct((n, w_a), F32), jax.ShapeDtypeStruct((n, w_a), F32),
                   jax.ShapeDtypeStruct((n, w_a), F32), jax.ShapeDtypeStruct((n, w_b), BF16),
                   jax.ShapeDtypeStruct((n, w_b), F32)],
        compiler_params=_params(("arbitrary",), 32),
        name="mix_in_decode",
    )(x, w, lng, lnb, w00, b0)


def _attn_kernel(lam_ref, g_ref, q_ref, k_ref, v_ref, b0_ref, b1_ref, o_ref,
                 qq_ref, m_ref, acc_ref, p_last, *, t, n_heads, lambda_init):
    qi = pl.program_id(1)
    lane = lax.broadcasted_iota(jnp.int32, (t, E_A), 1)
    for h in range(n_heads):
        q = q_ref[:, h * E_A:(h + 1) * E_A]
        zero = jnp.zeros_like(q)
        qq_ref[2 * h] = jnp.where(lane < DH_A, q, zero)
        qq_ref[2 * h + 1] = jnp.where(lane < DH_A, zero, q)
    ones = jnp.ones((t, LANES), BF16)

    last_u = 2 * n_heads - 1
    last_cols = slice((n_heads - 1) * E_A, n_heads * E_A)

    def flush_last(j_done):
        off = pl.multiple_of(j_done * t, t)
        va = jnp.concatenate([v_ref[pl.ds(off, t), last_cols], ones], axis=1)
        acc_ref[last_u] += _dot(p_last[...], va)

    def update(u, blocks, va, first):
        lane_max = functools.reduce(jnp.maximum, blocks)
        row_max = jnp.broadcast_to(jnp.max(lane_max, axis=-1, keepdims=True), lane_max.shape)
        m_new = row_max if first else jnp.maximum(m_ref[u], row_max)
        p = jnp.concatenate([jnp.exp2(blk - m_new).astype(BF16) for blk in blocks], axis=1)
        if first:
            if u == last_u:
                acc_ref[u] = jnp.zeros(acc_ref.shape[1:], F32)
            else:
                acc_ref[u] = _dot(p, va)
        else:
            alpha = jnp.exp2(m_ref[u] - m_new)
            alpha2 = jnp.concatenate([alpha, alpha], axis=1)
            if u == last_u:
                acc_ref[u] = alpha2 * acc_ref[u]
            else:
                acc_ref[u] = alpha2 * acc_ref[u] + _dot(p, va)
        if u == last_u:
            p_last[...] = p
        m_ref[u] = m_new

    def split(s):
        return [s[:, i * LANES:(i + 1) * LANES] for i in range(s.shape[1] // LANES)]

    def step(j, kind, j_pending):
        if j_pending is not None:
            flush_last(j_pending)
        off = pl.multiple_of(j * t, t)
        if kind == "diag":
            visible = (lax.broadcasted_iota(jnp.int32, (t, t), 0)
                       >= lax.broadcasted_iota(jnp.int32, (t, t), 1))
        for h in range(n_heads):
            cols = slice(h * E_A, (h + 1) * E_A)
            kt = k_ref[pl.ds(off, t), cols]
            va = jnp.concatenate([v_ref[pl.ds(off, t), cols], ones], axis=1)
            for u in (2 * h, 2 * h + 1):
                s = _dot_nt(qq_ref[u], kt)
                if kind == "diag":
                    s = jnp.where(visible, s + b0_ref[h], NEG_INF)
                blocks = split(s)
                if kind == "prev":
                    corner = blocks[-1]
                    blocks[-1] = jnp.concatenate(
                        [corner[0:MAX_DIST] + b1_ref[h], corner[MAX_DIST:]], axis=0)
                update(u, blocks, va, first=(kind == "diag"))

    step(qi, "diag", None)

    @pl.when(qi >= 1)
    def _():
        step(qi - 1, "prev", qi)

    def far(j, carry):
        step(j, "far", jnp.where(j == 0, qi - 1, j - 1))
        return carry

    lax.fori_loop(0, jnp.maximum(qi - 1, 0), far, 0)
    flush_last(jnp.maximum(qi - 2, 0))

    lam = _diff_lambda(lam_ref, lambda_init)
    for h in range(n_heads):
        a0 = acc_ref[2 * h]
        a1 = acc_ref[2 * h + 1]
        o = a0[:, 0:E_A] / a0[:, E_A:2 * E_A] - lam * (a1[:, 0:E_A] / a1[:, E_A:2 * E_A])
        o_ref[:, h * E_A:(h + 1) * E_A] = _head_rmsnorm(o, g_ref[...], lambda_init).astype(BF16)


def _attention(lam4, subln_g, q, k, v, b0, b1, *, t, lambda_init):
    bsz, s, w_a = q.shape
    n_heads = w_a // E_A
    kern = functools.partial(_attn_kernel, t=t, n_heads=n_heads, lambda_init=lambda_init)
    const = lambda shape: pl.BlockSpec(shape, lambda b, i: (0,) * len(shape))
    seq_spec = pl.BlockSpec((None, s, w_a), lambda b, i: (b, 0, 0))
    return pl.pallas_call(
        kern,
        grid=(bsz, s // t),
        in_specs=[
            const(lam4.shape), const(subln_g.shape),
            pl.BlockSpec((None, t, w_a), lambda b, i: (b, i, 0)),
            seq_spec, seq_spec,
            _resident(b0.shape), _resident(b1.shape),
        ],
        out_specs=pl.BlockSpec((None, t, w_a), lambda b, i: (b, i, 0)),
        out_shape=jax.ShapeDtypeStruct((bsz, s, w_a), BF16),
        scratch_shapes=[pltpu.VMEM((2 * n_heads, t, E_A), BF16),
                        pltpu.VMEM((2 * n_heads, t, LANES), F32),
                        pltpu.VMEM((2 * n_heads, t, E_A + LANES), F32),
                        pltpu.VMEM((t, t), BF16)],
        compiler_params=_params(("parallel", "arbitrary"), 52),
        name="diff_attn_prompt",
    )(lam4, subln_g, q, k, v, b0, b1)


def _decode_attn_kernel(pt_ref, lam_ref, g_ref, q_ref, kn_ref, vn_ref, bias_ref, new_bias_ref, *rest,
                        pages, n_heads, lambda_init):
    del pt_ref
    k_refs = rest[:pages]
    v_refs = rest[pages:2 * pages]
    o_ref, kb_ref, vb_ref, m_ref, l_ref, acc_ref = rest[2 * pages:]
    j = pl.program_id(1)
    last = pl.num_programs(1) - 1
    rows = 2 * n_heads
    page_len = k_refs[0].shape[0]

    @pl.when(j == 0)
    def _():
        m_ref[...] = jnp.full_like(m_ref, NEG_INF)
        l_ref[...] = jnp.zeros_like(l_ref)
        acc_ref[...] = jnp.zeros_like(acc_ref)

    r_id = lax.broadcasted_iota(jnp.int32, (rows, E_A), 0)
    l_id = lax.broadcasted_iota(jnp.int32, (rows, E_A), 1)

    def per_row_head(tok_ref):
        out = jnp.zeros((rows, E_A), F32)
        for h in range(n_heads):
            piece = jnp.broadcast_to(tok_ref[:, h * E_A:(h + 1) * E_A], (rows, E_A))
            out = jnp.where((r_id // 2) == h, piece, out)
        return out

    qm = jnp.where((l_id // DH_A) == (r_id % 2), per_row_head(q_ref), 0.0)

    for i in range(pages):
        kb_ref[i * page_len:(i + 1) * page_len, :] = k_refs[i][...].astype(BF16)
        vb_ref[i * page_len:(i + 1) * page_len, :] = v_refs[i][...].astype(BF16)

    n_keys = pages * page_len
    s = _dot_nt(qm.astype(BF16), kb_ref[...])
    near = jnp.where(j == last, 1.0, 0.0)
    s_tail = s[:, n_keys - page_len:] + near * bias_ref[...]
    s = jnp.concatenate([s[:, :n_keys - page_len], s_tail], axis=1) if pages > 1 else s_tail
    row_s = lax.broadcasted_iota(jnp.int32, s.shape, 0)
    col_s = lax.broadcasted_iota(jnp.int32, s.shape, 1)
    s = jnp.where((col_s % n_heads) == (row_s // 2), s, NEG_INF)

    m_old = m_ref[...]
    m_new = jnp.maximum(m_old, jnp.max(s, axis=-1, keepdims=True))
    alpha = jnp.exp2(m_old - m_new)
    p = jnp.exp2(s - m_new)
    l_ref[...] = alpha * l_ref[...] + jnp.sum(p, axis=-1, keepdims=True)
    acc_ref[...] = alpha * acc_ref[...] + _dot(p.astype(BF16), vb_ref[...])
    m_ref[...] = m_new

    @pl.when(j == last)
    def _():
        s_new = (jnp.sum(qm * per_row_head(kn_ref), axis=-1, keepdims=True) + new_bias_ref[...])
        m_old = m_ref[...]
        m_new = jnp.maximum(m_old, s_new)
        alpha = jnp.exp2(m_old - m_new)
        p_new = jnp.exp2(s_new - m_new)
        l_fin = alpha * l_ref[...] + p_new
        o = (alpha * acc_ref[...] + p_new * per_row_head(vn_ref)) / l_fin
        lam = _diff_lambda(lam_ref, lambda_init)
        g = g_ref[...]
        for h in range(n_heads):
            oh = o[2 * h:2 * h + 1] - lam * o[2 * h + 1:2 * h + 2]
            o_ref[:, h * E_A:(h + 1) * E_A] = _head_rmsnorm(oh, g, lambda_init).astype(BF16)


def _decode_attention(page_table, lam4, subln_g, q, k_new, v_new, bias, new_bias, cache_k, cache_v, *,
                      layer, pages, lambda_init):
    n_seq, _, w_a = q.shape
    n_pages = page_table.shape[1]
    assert n_pages % pages == 0
    n_heads = w_a // E_A
    rows = 2 * n_heads
    page_len = cache_k.shape[2]
    kern = functools.partial(_decode_attn_kernel, pages=pages, n_heads=n_heads,
                             lambda_init=lambda_init)
    const = lambda shape: pl.BlockSpec(shape, lambda b, j, pt: (0,) * len(shape))
    tok = pl.BlockSpec((None, 1, w_a), lambda b, j, pt: (b, 0, 0))

    def page_spec(i):
        return pl.BlockSpec((None, None, page_len, E_A),
                            lambda b, j, pt: (layer, pt[b * n_pages + j * pages + i], 0, 0))

    grid_spec = pltpu.PrefetchScalarGridSpec(
        num_scalar_prefetch=1,
        grid=(n_seq, n_pages // pages),
        in_specs=[const(lam4.shape), const(subln_g.shape), tok, tok, tok, const(bias.shape),
                  const(new_bias.shape)]
        + [page_spec(i) for i in range(pages)] + [page_spec(i) for i in range(pages)],
        out_specs=pl.BlockSpec((None, 1, w_a), lambda b, j, pt: (b, 0, 0)),
        scratch_shapes=[pltpu.VMEM((pages * page_len, E_A), BF16),
                        pltpu.VMEM((pages * page_len, E_A), BF16),
                        pltpu.VMEM((rows, 1), F32), pltpu.VMEM((rows, 1), F32),
                        pltpu.VMEM((rows, E_A), F32)],
    )
    return pl.pallas_call(
        kern,
        grid_spec=grid_spec,
        out_shape=jax.ShapeDtypeStruct((n_seq, 1, w_a), BF16),
        compiler_params=_params(("parallel", "arbitrary"), 52),
        name="diff_attn_decode",
    )(page_table.reshape(-1), lam4, subln_g, q, k_new, v_new, bias, new_bias,
      *([cache_k] * pages), *([cache_v] * pages))


def _mix_out_kernel(x_ref, oa_ref, ob_ref, woa_ref, wob_ref, g_ref, b_ref, wq_ref,
                    x2_ref, qx_ref, *, alpha, qscale):
    y = _dot(oa_ref[...], woa_ref[...]) + _dot(ob_ref[...], wob_ref[...])
    x2 = _layer_norm(alpha * x_ref[...] + y, g_ref[...], b_ref[...])
    x2_ref[...] = x2
    qx_ref[...] = (_dot(x2.astype(BF16), wq_ref[...]) * qscale).astype(BF16)


def _mix_out(x, oa, ob, woa, wob, g, b, wq, *, alpha, qscale, tm):
    n, d = x.shape
    kern = functools.partial(_mix_out_kernel, alpha=alpha, qscale=qscale)
    row = lambda width: pl.BlockSpec((tm, width), lambda i: (i, 0))
    return pl.pallas_call(
        kern,
        grid=(n // tm,),
        in_specs=[row(d), row(oa.shape[1]), row(ob.shape[1]), _resident(woa.shape),
                  _resident(wob.shape), _resident(g.shape), _resident(b.shape), _resident(wq.shape)],
        out_specs=[row(d), row(d)],
        out_shape=[jax.ShapeDtypeStruct((n, d), F32), jax.ShapeDtypeStruct((n, d), BF16)],
        compiler_params=_params(("parallel",), 40),
        name="mix_out_ln_q",
    )(x, oa, ob, woa, wob, g, b, wq)


def _post_mix_kernel(x_ref, oa_ref, ob_ref, mk_ref, mv_ref, woa_ref, wob_ref, g1_ref, b1_ref,
                     wq_ref, wo_ref, g2_ref, b2_ref, y_ref, o_scr, x2_scr, xb_scr, q_scr,
                     *, alpha, qscale, n_heads):
    tm, d = x_ref.shape
    dh = d // n_heads
    subs = [slice(r0, r0 + SUB_ROWS) for r0 in range(0, tm, SUB_ROWS)]
    for rs in subs:
        y = _dot(oa_ref[rs, :], woa_ref[...]) + _dot(ob_ref[rs, :], wob_ref[...])
        x2 = _layer_norm(alpha * x_ref[rs, :] + y, g1_ref[...], b1_ref[...])
        x2_scr[rs, :] = x2
        xb_scr[rs, :] = x2.astype(BF16)
    for rs in subs:
        q_scr[rs, :] = (_dot(xb_scr[rs, :], wq_ref[...]) * qscale).astype(BF16)
    for h in range(n_heads):
        cols = slice(h * dh, (h + 1) * dh)
        s = _dot_nt(q_scr[:, cols], mk_ref[:, cols])
        p = jnp.exp2(s - jnp.max(s, axis=-1, keepdims=True))
        l = jnp.sum(p, axis=-1, keepdims=True)
        o_scr[:, cols] = (_dot(p.astype(BF16), mv_ref[:, cols]) / l).astype(BF16)
    for rs in subs:
        y2 = _dot(o_scr[rs, :], wo_ref[...])
        y_ref[rs, :] = _layer_norm(alpha * x2_scr[rs, :] + y2, g2_ref[...], b2_ref[...])


def _post_mix(x, oa, ob, mk, mv, woa, wob, g1, b1, wq, wo, g2, b2, *, alpha, qscale, n_heads, tm):
    bsz, s, d = x.shape
    n_mem = mk.shape[1]
    kern = functools.partial(_post_mix_kernel, alpha=alpha, qscale=qscale, n_heads=n_heads)
    row = lambda width: pl.BlockSpec((None, tm, width), lambda b, i: (b, i, 0))
    mem = pl.BlockSpec((None, n_mem, d), lambda b, i: (b, 0, 0))
    weights = [woa, wob, g1, b1, wq, wo, g2, b2]
    return pl.pallas_call(
        kern,
        grid=(bsz, s // tm),
        in_specs=[row(d), row(oa.shape[2]), row(ob.shape[2]), mem, mem]
        + [_resident(w.shape) for w in weights],
        out_specs=row(d),
        out_shape=jax.ShapeDtypeStruct((bsz, s, d), F32),
        scratch_shapes=[pltpu.VMEM((tm, d), BF16), pltpu.VMEM((tm, d), F32),
                        pltpu.VMEM((tm, d), BF16), pltpu.VMEM((tm, d), BF16)],
        compiler_params=_params(("parallel", "parallel"), 48),
        name="post_mix",
    )(x, oa, ob, mk, mv, *weights)


def _cross_decode_kernel(q_ref, mk_ref, mv_ref, o_ref, *, n_heads):
    n_seq, n_mem, _, dh = mk_ref.shape
    n_keys = n_mem * n_heads
    sublanes = SUBLANES
    assert n_heads <= sublanes
    r_id = lax.broadcasted_iota(jnp.int32, (sublanes, LANES), 0)
    for i in range(n_seq):
        q = q_ref[i].astype(F32)
        s = None
        for c in range(dh // LANES):
            qm = jnp.zeros((sublanes, LANES), F32)
            for h in range(n_heads):
                lo = h * dh + c * LANES
                piece = jnp.broadcast_to(q[:, lo:lo + LANES], (sublanes, LANES))
                qm = jnp.where((r_id % n_heads) == h, piece, qm)
            xk = mk_ref[i, :, :, c * LANES:(c + 1) * LANES].reshape(n_keys, LANES).astype(BF16)
            part = _dot_nt(qm.astype(BF16), xk)
            s = part if s is None else s + part
        row_s = lax.broadcasted_iota(jnp.int32, s.shape, 0)
        col_s = lax.broadcasted_iota(jnp.int32, s.shape, 1)
        s = jnp.where((col_s % n_heads) == (row_s % n_heads), s, NEG_INF)
        p = jnp.exp2(s - jnp.max(s, axis=-1, keepdims=True))
        l = jnp.sum(p, axis=-1, keepdims=True)
        pb = p.astype(BF16)
        for c in range(dh // LANES):
            xv = mv_ref[i, :, :, c * LANES:(c + 1) * LANES].reshape(n_keys, LANES).astype(BF16)
            o = _dot(pb, xv) / l
            for h in range(n_heads):
                lo = h * dh + c * LANES
                o_ref[i, :, lo:lo + LANES] = o[h:h + 1].astype(BF16)


def _cross_attention_decode(qx, mk, mv, *, layer, n_heads, seqs_per_step):
    n_seq, _, d = qx.shape
    n_mem, _, dh = mk.shape[2:]
    assert n_seq % seqs_per_step == 0
    kern = functools.partial(_cross_decode_kernel, n_heads=n_heads)
    tok_spec = pl.BlockSpec((seqs_per_step, 1, d), lambda b: (b, 0, 0))
    mem_spec = pl.BlockSpec((None, seqs_per_step, n_mem, n_heads, dh),
                            lambda b: (layer, b, 0, 0, 0))
    return pl.pallas_call(
        kern,
        grid=(n_seq // seqs_per_step,),
        in_specs=[tok_spec, mem_spec, mem_spec],
        out_specs=tok_spec,
        out_shape=jax.ShapeDtypeStruct((n_seq, 1, d), BF16),
        compiler_params=_params(("parallel",), 32),
        name="cross_attn_decode",
    )(qx, mk, mv)


def _proj_ln_kernel(x_ref, o_ref, w_ref, g_ref, b_ref, y_ref, *, alpha):
    y = _dot(o_ref[...], w_ref[...])
    y_ref[...] = _layer_norm(alpha * x_ref[...] + y, g_ref[...], b_ref[...])


def _proj_ln(x, o, w, g, b, *, alpha, tm):
    n, d = x.shape
    kern = functools.partial(_proj_ln_kernel, alpha=alpha)
    row = pl.BlockSpec((tm, d), lambda i: (i, 0))
    return pl.pallas_call(
        kern,
        grid=(n // tm,),
        in_specs=[row, row, _resident(w.shape), _resident(g.shape), _resident(b.shape)],
        out_specs=row,
        out_shape=jax.ShapeDtypeStruct((n, d), F32),
        compiler_params=_params(("parallel",), 32),
        name="proj_ln",
    )(x, o, w, g, b)


def _mem_kv_kernel(m_ref, w_ref, k_ref, v_ref, kb_ref, vb_ref):
    d = kb_ref.shape[1]
    n_heads, dh = k_ref.shape[1:]
    mb = m_ref[...].astype(BF16)
    k = _dot(mb, w_ref[:, 0:d])
    v = _dot(mb, w_ref[:, d:2 * d])
    kb_ref[...] = k.astype(BF16)
    vb_ref[...] = v.astype(BF16)
    k_ref[...] = k.reshape(k.shape[0], n_heads, dh)
    v_ref[...] = v.reshape(v.shape[0], n_heads, dh)


def _mem_kv(mem, w, *, n_heads, tm):
    n, d = mem.shape
    dh = d // n_heads
    row = pl.BlockSpec((tm, d), lambda i: (i, 0))
    row_h = pl.BlockSpec((tm, n_heads, dh), lambda i: (i, 0, 0))
    return pl.pallas_call(
        _mem_kv_kernel,
        grid=(n // tm,),
        in_specs=[row, _resident(w.shape)],
        out_specs=[row_h, row_h, row, row],
        out_shape=[jax.ShapeDtypeStruct((n, n_heads, dh), F32)] * 2
        + [jax.ShapeDtypeStruct((n, d), BF16)] * 2,
        compiler_params=_params(("parallel",), 32),
        name="mem_kv",
    )(mem, w)


def _shifted_bias(table, dist):
    n = jnp.maximum(dist, 0)
    max_exact = NB // 2
    nf = jnp.maximum(n, 1).astype(F32)
    large = max_exact + (jnp.log(nf / max_exact) / math.log(MAX_DIST / max_exact)
                         * (NB - max_exact)).astype(jnp.int32)
    large = jnp.minimum(large, NB - 1)
    bucket = jnp.where(n < max_exact, n, large)
    tab = (table.astype(F32) - table[NB - 1].astype(F32)) * LOG2E
    hit = bucket[..., None, None] == jnp.arange(NB)[:, None]
    return jnp.sum(jnp.where(hit, tab, 0.0), axis=-2)


def _place_blocks(sel, tile):
    nb, n = sel.shape[0], tile.shape[1]
    out = sel[None, :, None, :, None] * tile[:, None, :, None, :]
    return out.reshape(tile.shape[0], nb * n, nb * n)


def _toeplitz(f_pos, f_neg, t):
    period = 2 * t
    v = jnp.concatenate([f_neg, jnp.zeros_like(f_pos[:1]), f_pos[:0:-1]], axis=0)
    tiled = jnp.tile(v, (t, 1))[: t * (period - 1)]
    skew = tiled.reshape(t, period - 1, -1)
    return jnp.transpose(skew[:, :t], (2, 0, 1))


def kernel(x_prompt, x_sample, mem_prompt, cache_k, cache_v, cache_mem_k, cache_mem_v, page_table, rel_bias, ln_g, ln_b, ffn1_w_in, ffn1_w_out, w_mix_in, w_mix_out, lambda_q1, lambda_k1, lambda_q2, lambda_k2, subln_g, sgu_ln_g, sgu_ln_b, sgu_w, sgu_b, xq_w, xkv_w, xo_w, ffn2_w_in, ffn2_w_out):
    bsz, seq, d = x_prompt.shape
    n_dec = x_sample.shape[0]
    depth = ln_g.shape[0]
    assert depth == 1 and x_sample.shape[1] == 1
    w_b = sgu_ln_g.shape[1]
    w_a = (w_mix_in.shape[2] - 2 * w_b) // 3
    n_heads = w_a // E_A
    n_groups = sgu_w.shape[1]
    n_mem, h_m, dh_m = cache_mem_k.shape[2:]
    page_rows = cache_k.shape[2]
    alpha = (2 * depth) ** 0.25
    lambda_init = 0.8 - 0.6 * math.exp(-0.3 * 0)
    l = 0
    t_attn = ATTN_TILE
    tm = TOKEN_TILE
    assert seq % tm == 0 and seq % t_attn == 0 and t_attn % MAX_DIST == 0 and tm % SUB_ROWS == 0
    assert SUB_ROWS % CHUNK == 0
    assert page_rows == CHUNK and MAX_DIST <= page_rows

    row2 = lambda a: a.reshape(1, -1)
    g_ln = [row2(ln_g[l, i]) for i in range(4)]
    b_ln = [row2(ln_b[l, i]) for i in range(4)]
    ffn1 = (ffn1_w_in[l].astype(BF16), ffn1_w_out[l].astype(BF16))
    ffn2 = (ffn2_w_in[l].astype(BF16), ffn2_w_out[l].astype(BF16))
    q_fold = jnp.concatenate([jnp.full((w_a,), DH_A ** -0.5, F32),
                              jnp.ones((w_mix_in.shape[2] - w_a,), F32)])
    w_mix = (w_mix_in[l] * q_fold).astype(BF16)
    woa = w_mix_out[l, :w_a].astype(BF16)
    wob = w_mix_out[l, w_a:].astype(BF16)
    wq_x = xq_w[l].astype(BF16)
    wo_x = xo_w[l].astype(BF16)
    wkv = xkv_w[l].astype(BF16)
    lam4 = jnp.stack([lambda_q1[l], lambda_k1[l], lambda_q2[l], lambda_k2[l]]).astype(F32)
    g_sub = row2(subln_g[l])
    lng_s, lnb_s = row2(sgu_ln_g[l]), row2(sgu_ln_b[l])
    tril_w = jnp.tril(sgu_w[l]).astype(BF16)
    sgu_bias = sgu_b[l].reshape(n_groups, CHUNK, 1)
    cg = w_b // n_groups
    w00 = jnp.repeat(sgu_w[l, :, 0, 0], cg).reshape(1, w_b)
    b00 = jnp.repeat(sgu_b[l, :, 0], cg).reshape(1, w_b)
    xq_scale = dh_m ** -0.5 * LOG2E

    ar = jnp.arange(MAX_DIST)
    near = _toeplitz(_shifted_bias(rel_bias, ar), _shifted_bias(rel_bias, 0 * ar), MAX_DIST)
    edge = _toeplitz(_shifted_bias(rel_bias, MAX_DIST + ar), _shifted_bias(rel_bias, MAX_DIST - ar),
                     MAX_DIST)
    nb = t_attn // MAX_DIST
    bias0 = (_place_blocks(jnp.eye(nb, dtype=F32), near)
             + _place_blocks(jnp.eye(nb, k=-1, dtype=F32), edge))
    dec_near = _shifted_bias(rel_bias, page_rows - jnp.arange(page_rows))
    dec_bias = jnp.repeat(jnp.repeat(dec_near.T, 2, axis=0), n_heads, axis=1)
    new_bias = jnp.repeat(_shifted_bias(rel_bias, jnp.zeros((1,), jnp.int32)).T, 2, axis=0)

    n_tok = bsz * seq
    xp = x_prompt.reshape(n_tok, d)
    mk_p, mv_p, mkb, mvb = _mem_kv(mem_prompt.reshape(bsz * n_mem, d), wkv, n_heads=h_m, tm=n_mem)
    x1 = _ffn_ln(xp, *ffn1, g_ln[0], b_ln[0], alpha=alpha, tm=tm)
    qb, k_p, kb, v_p, vb, ob = _mix_in(x1, w_mix, lng_s, lnb_s, tril_w, sgu_bias,
                                       w_a=w_a, w_b=w_b, qscale=LOG2E, tm=tm)
    oa = _attention(lam4, g_sub, qb.reshape(bsz, seq, w_a), kb.reshape(bsz, seq, w_a),
                    vb.reshape(bsz, seq, w_a), bias0, edge, t=t_attn, lambda_init=lambda_init)
    x3 = _post_mix(x1.reshape(bsz, seq, d), oa, ob.reshape(bsz, seq, w_b),
                   mkb.reshape(bsz, n_mem, d), mvb.reshape(bsz, n_mem, d),
                   woa, wob, g_ln[1], b_ln[1], wq_x, wo_x, g_ln[2], b_ln[2],
                   alpha=alpha, qscale=xq_scale, n_heads=h_m, tm=tm)
    y_p = _ffn_ln(x3.reshape(n_tok, d), *ffn2, g_ln[3], b_ln[3], alpha=alpha, tm=tm)

    xs = x_sample.reshape(n_dec, d)
    s1 = _ffn_ln(xs, *ffn1, g_ln[0], b_ln[0], alpha=alpha, tm=n_dec)
    q_s, k_s, v_s, ob_s, vn_s = _mix_in_decode(s1, w_mix, lng_s, lnb_s, w00, b00,
                                               w_a=w_a, w_b=w_b, qscale=LOG2E)
    oa_s = _decode_attention(page_table, lam4, g_sub, q_s.reshape(n_dec, 1, w_a),
                             k_s.reshape(n_dec, 1, w_a), v_s.reshape(n_dec, 1, w_a), dec_bias, new_bias,
                             cache_k.reshape(depth, -1, page_rows * n_heads, E_A),
                             cache_v.reshape(depth, -1, page_rows * n_heads, E_A),
                             layer=l, pages=DECODE_PAGES, lambda_init=lambda_init)
    s2, qx_s = _mix_out(s1, oa_s.reshape(n_dec, w_a), ob_s, woa, wob, g_ln[1], b_ln[1], wq_x,
                        alpha=alpha, qscale=xq_scale, tm=n_dec)
    ox_s = _cross_attention_decode(qx_s.reshape(n_dec, 1, d),
                                   cache_mem_k, cache_mem_v,
                                   layer=l, n_heads=h_m, seqs_per_step=CROSS_DECODE_SEQS)
    s3 = _proj_ln(s2, ox_s.reshape(n_dec, d), wo_x, g_ln[2], b_ln[2], alpha=alpha, tm=n_dec)
    y_s = _ffn_ln(s3, *ffn2, g_ln[3], b_ln[3], alpha=alpha, tm=n_dec)

    return (y_p.reshape(bsz, seq, d), y_s.reshape(n_dec, 1, d),
            k_p.reshape(1, bsz, seq, n_heads, E_A), v_p.reshape(1, bsz, seq, n_heads, E_A),
            mk_p.reshape(1, bsz, n_mem, h_m, dh_m), mv_p.reshape(1, bsz, n_mem, h_m, dh_m),
            k_s.reshape(1, n_dec, 1, n_heads, E_A), v_s.reshape(1, n_dec, 1, n_heads, E_A),
            vn_s.reshape(1, n_dec, 1, w_b))
```

```python
import functools
import math

import jax
import jax.numpy as jnp
from jax import lax
from jax.experimental import pallas as pl
from jax.experimental.pallas import tpu as pltpu

F32 = jnp.float32
BF16 = jnp.bfloat16

LN_EPS = 1e-5
NEG_INF = -1e30
LOG2E = 1.4426950408889634
DH_A = 64
E_A = 2 * DH_A
CHUNK = 128
NB = 32
MAX_DIST = 128

LANES = 128
SUBLANES = 8
MXU_EDGE = 256
MIB = 1024 * 1024

TOKEN_TILE = 1024
SUB_ROWS = 256
ATTN_TILE = 512
DECODE_PAGES = 32
CROSS_DECODE_SEQS = 4


def _params(semantics, vmem_mib):
    return pltpu.CompilerParams(dimension_semantics=semantics, vmem_limit_bytes=vmem_mib * MIB)


def _resident(shape):
    nd = len(shape)
    return pl.BlockSpec(shape, lambda *_: (0,) * nd, pipeline_mode=pl.Buffered(1))


def _layer_norm(x, g, b):
    mu = jnp.mean(x, -1, keepdims=True)
    xc = x - mu
    var = jnp.mean(xc * xc, -1, keepdims=True)
    return xc * lax.rsqrt(var + LN_EPS) * g + b


def _gelu(x):
    return 0.5 * x * (1.0 + lax.erf(x * math.sqrt(0.5)))


def _dot(a, b):
    return jnp.dot(a, b, preferred_element_type=F32)


def _dot_nt(a, b):
    return lax.dot_general(a, b, (((1,), (1,)), ((), ())), preferred_element_type=F32)


def _diff_lambda(lam_ref, lambda_init):
    lv = lam_ref[...]
    a = jnp.sum(lv[0:1] * lv[1:2], axis=-1, keepdims=True)
    b = jnp.sum(lv[2:3] * lv[3:4], axis=-1, keepdims=True)
    return jnp.exp(a) - jnp.exp(b) + lambda_init


def _head_rmsnorm(o, g, lambda_init):
    return o * lax.rsqrt(jnp.mean(o * o, -1, keepdims=True) + LN_EPS) * g * (1.0 - lambda_init)


def _ffn_ln_kernel(x_ref, wi_ref, wo_ref, g_ref, b_ref, o_ref, act_ref, xb_ref, *, alpha, fc):
    tm = x_ref.shape[0]
    sub = min(tm, SUB_ROWS)
    subs = [slice(r0, r0 + sub) for r0 in range(0, tm, sub)]
    xb_ref[...] = x_ref[...].astype(BF16)
    d_ff = wo_ref.shape[0]
    for j in range(d_ff // fc):
        cols = slice(j * fc, (j + 1) * fc)
        up_cols = slice(d_ff + j * fc, d_ff + (j + 1) * fc)
        for rs in subs:
            ha = _dot(xb_ref[rs, :], wi_ref[:, cols])
            hb = _dot(xb_ref[rs, :], wi_ref[:, up_cols])
            act_ref[rs, cols] = (ha * jax.nn.sigmoid(ha) * hb).astype(BF16)
    for rs in subs:
        y = _dot(act_ref[rs, :], wo_ref[...])
        o_ref[rs, :] = _layer_norm(alpha * x_ref[rs, :] + 0.5 * y, g_ref[...], b_ref[...])


def _ffn_ln(x, wi, wo, g, b, *, alpha, tm):
    n, d = x.shape
    d_ff = wo.shape[0]
    assert d_ff % MXU_EDGE == 0 and wi.shape[1] == 2 * d_ff
    kern = functools.partial(_ffn_ln_kernel, alpha=alpha, fc=MXU_EDGE)
    return pl.pallas_call(
        kern,
        grid=(n // tm,),
        in_specs=[
            pl.BlockSpec((tm, d), lambda i: (i, 0)),
            _resident(wi.shape), _resident(wo.shape), _resident(g.shape), _resident(b.shape),
        ],
        out_specs=pl.BlockSpec((tm, d), lambda i: (i, 0)),
        out_shape=jax.ShapeDtypeStruct((n, d), F32),
        scratch_shapes=[pltpu.VMEM((tm, d_ff), BF16), pltpu.VMEM((tm, d), BF16)],
        compiler_params=_params(("parallel",), 52),
        name="ffn_ln",
    )(x, wi, wo, g, b)


def _mix_in_kernel(x_ref, w_ref, lng_ref, lnb_ref, tw_ref, sb_ref,
                   q_ref, k_ref, kb_ref, v_ref, vb_ref, ob_ref, xb_ref, *, w_a, w_b, qscale):
    tm = x_ref.shape[0]
    n_groups = tw_ref.shape[0]
    cg = w_b // n_groups
    starts = list(range(0, tm, SUB_ROWS))
    gated = {}

    def gate_inputs(r0):
        rs = slice(r0, r0 + SUB_ROWS)
        xb_ref[rs, :] = x_ref[rs, :].astype(BF16)
        gv = _gelu(_dot(xb_ref[rs, :], w_ref[:, 3 * w_a + w_b:3 * w_a + 2 * w_b]))
        gu = _gelu(_dot(xb_ref[rs, :], w_ref[:, 3 * w_a:3 * w_a + w_b]))
        gated[r0] = (gu, _layer_norm(gv, lng_ref[...], lnb_ref[...]).astype(BF16))

    def qkv(r0):
        rs = slice(r0, r0 + SUB_ROWS)
        q_ref[rs, :] = (_dot(xb_ref[rs, :], w_ref[:, 0:w_a]) * qscale).astype(BF16)
        hk = _dot(xb_ref[rs, :], w_ref[:, w_a:2 * w_a])
        kb_ref[rs, :] = hk.astype(BF16)
        hv = _dot(xb_ref[rs, :], w_ref[:, 2 * w_a:3 * w_a])
        vb_ref[rs, :] = hv.astype(BF16)
        k_ref[rs] = hk.reshape(SUB_ROWS, w_a // E_A, E_A)
        v_ref[rs] = hv.reshape(SUB_ROWS, w_a // E_A, E_A)

    def gate(r0):
        gu, vn = gated.pop(r0)
        for c in range(SUB_ROWS // CHUNK):
            rows = slice(c * CHUNK, (c + 1) * CHUNK)
            out_rows = slice(r0 + c * CHUNK, r0 + (c + 1) * CHUNK)
            for g in range(n_groups):
                cols = slice(g * cg, (g + 1) * cg)
                mixed = _dot(tw_ref[g], vn[rows, cols]) + sb_ref[g]
                ob_ref[out_rows, cols] = (gu[rows, cols] * mixed).astype(BF16)

    gate_inputs(starts[0])
    for cur, nxt in zip(starts, starts[1:] + [None]):
        if nxt is not None:
            gate_inputs(nxt)
        qkv(cur)
        gate(cur)


def _mix_in(x, w, lng, lnb, tw, sb, *, w_a, w_b, qscale, tm):
    n, d = x.shape
    kern = functools.partial(_mix_in_kernel, w_a=w_a, w_b=w_b, qscale=qscale)
    row = lambda width: pl.BlockSpec((tm, width), lambda i: (i, 0))
    shp = lambda width, dt: jax.ShapeDtypeStruct((n, width), dt)
    n_heads = w_a // E_A
    row_h = pl.BlockSpec((tm, n_heads, E_A), lambda i: (i, 0, 0))
    shp_h = jax.ShapeDtypeStruct((n, n_heads, E_A), F32)
    return pl.pallas_call(
        kern,
        grid=(n // tm,),
        in_specs=[row(d), _resident(w.shape), _resident(lng.shape), _resident(lnb.shape),
                  _resident(tw.shape), _resident(sb.shape)],
        out_specs=[row(w_a), row_h, row(w_a), row_h, row(w_a), row(w_b)],
        out_shape=[shp(w_a, BF16), shp_h, shp(w_a, BF16), shp_h, shp(w_a, BF16), shp(w_b, BF16)],
        scratch_shapes=[pltpu.VMEM((tm, d), BF16)],
        compiler_params=_params(("parallel",), 48),
        name="mix_in_sgu",
    )(x, w, lng, lnb, tw, sb)


def _mix_in_decode_kernel(x_ref, w_ref, lng_ref, lnb_ref, w00_ref, b0_ref,
                          q_ref, k_ref, v_ref, ob_ref, vn_ref, *, w_a, w_b, qscale):
    xb = x_ref[...].astype(BF16)
    q_ref[...] = _dot(xb, w_ref[:, 0:w_a]) * qscale
    k_ref[...] = _dot(xb, w_ref[:, w_a:2 * w_a])
    v_ref[...] = _dot(xb, w_ref[:, 2 * w_a:3 * w_a])
    gu = _gelu(_dot(xb, w_ref[:, 3 * w_a:3 * w_a + w_b]))
    gv = _gelu(_dot(xb, w_ref[:, 3 * w_a + w_b:3 * w_a + 2 * w_b]))
    vn = _layer_norm(gv, lng_ref[...], lnb_ref[...])
    vn_ref[...] = vn
    ob_ref[...] = (gu * (w00_ref[...] * vn + b0_ref[...])).astype(BF16)


def _mix_in_decode(x, w, lng, lnb, w00, b0, *, w_a, w_b, qscale):
    n, d = x.shape
    kern = functools.partial(_mix_in_decode_kernel, w_a=w_a, w_b=w_b, qscale=qscale)
    full = lambda shape: pl.BlockSpec(shape, lambda i: (0,) * len(shape))
    return pl.pallas_call(
        kern,
        grid=(1,),
        in_specs=[full(x.shape), full(w.shape), full(lng.shape), full(lnb.shape),
                  full(w00.shape), full(b0.shape)],
        out_specs=[full((n, w_a)), full((n, w_a)), full((n, w_a)), full((n, w_b)), full((n, w_b))],
        out_shape=[jax.ShapeDtypeStruct((n, w_a), F32), jax.ShapeDtypeStruct((n, w_a), F32),
                   jax.ShapeDtypeStruct((n, w_a), F32), jax.ShapeDtypeStruct((n, w_b), BF16),
                   jax.ShapeDtypeStruct((n, w_b), F32)],
        compiler_params=_params(("arbitrary",), 32),
        name="mix_in_decode",
    )(x, w, lng, lnb, w00, b0)


def _attn_kernel(lam_ref, g_ref, q_ref, k_ref, v_ref, b0_ref, b1_ref, o_ref,
                 qq_ref, m_ref, acc_ref, p_last, *, t, n_heads, lambda_init):
    qi = pl.program_id(1)
    lane = lax.broadcasted_iota(jnp.int32, (t, E_A), 1)
    for h in range(n_heads):
        q = q_ref[:, h * E_A:(h + 1) * E_A]
        zero = jnp.zeros_like(q)
        qq_ref[2 * h] = jnp.where(lane < DH_A, q, zero)
        qq_ref[2 * h + 1] = jnp.where(lane < DH_A, zero, q)
    ones = jnp.ones((t, LANES), BF16)

    last_u = 2 * n_heads - 1
    last_cols = slice((n_heads - 1) * E_A, n_heads * E_A)

    def flush_last(j_done):
        off = pl.multiple_of(j_done * t, t)
        va = jnp.concatenate([v_ref[pl.ds(off, t), last_cols], ones], axis=1)
        acc_ref[last_u] += _dot(p_last[...], va)

    def update(u, blocks, va, first):
        lane_max = functools.reduce(jnp.maximum, blocks)
        row_max = jnp.broadcast_to(jnp.max(lane_max, axis=-1, keepdims=True), lane_max.shape)
        m_new = row_max if first else jnp.maximum(m_ref[u], row_max)
        p = jnp.concatenate([jnp.exp2(blk - m_new).astype(BF16) for blk in blocks], axis=1)
        if first:
            if u == last_u:
                acc_ref[u] = jnp.zeros(acc_ref.shape[1:], F32)
            else:
                acc_ref[u] = _dot(p, va)
        else:
            alpha = jnp.exp2(m_ref[u] - m_new)
            alpha2 = jnp.concatenate([alpha, alpha], axis=1)
            if u == last_u:
                acc_ref[u] = alpha2 * acc_ref[u]
            else:
                acc_ref[u] = alpha2 * acc_ref[u] + _dot(p, va)
        if u == last_u:
            p_last[...] = p
        m_ref[u] = m_new

    def split(s):
        return [s[:, i * LANES:(i + 1) * LANES] for i in range(s.shape[1] // LANES)]

    def step(j, kind, j_pending):
        if j_pending is not None:
            flush_last(j_pending)
        off = pl.multiple_of(j * t, t)
        if kind == "diag":
            visible = (lax.broadcasted_iota(jnp.int32, (t, t), 0)
                       >= lax.broadcasted_iota(jnp.int32, (t, t), 1))
        for h in range(n_heads):
            cols = slice(h * E_A, (h + 1) * E_A)
            kt = k_ref[pl.ds(off, t), cols]
            va = jnp.concatenate([v_ref[pl.ds(off, t), cols], ones], axis=1)
            for u in (2 * h, 2 * h + 1):
                s = _dot_nt(qq_ref[u], kt)
                if kind == "diag":
                    s = jnp.where(visible, s + b0_ref[h], NEG_INF)
                blocks = split(s)
                if kind == "prev":
                    corner = blocks[-1]
                    blocks[-1] = jnp.concatenate(
                        [corner[0:MAX_DIST] + b1_ref[h], corner[MAX_DIST:]], axis=0)
                update(u, blocks, va, first=(kind == "diag"))

    step(qi, "diag", None)

    @pl.when(qi >= 1)
    def _():
        step(qi - 1, "prev", qi)

    n_far = jnp.maximum(qi - 1, 0)

    def far_pair(i, carry):
        j = 2 * i
        step(j, "far", jnp.where(i == 0, qi - 1, j - 1))
        step(j + 1, "far", j)
        return carry

    lax.fori_loop(0, n_far // 2, far_pair, 0)

    @pl.when(n_far % 2 == 1)
    def _():
        step(n_far - 1, "far", jnp.where(n_far == 1, qi - 1, n_far - 2))

    flush_last(jnp.maximum(qi - 2, 0))

    lam = _diff_lambda(lam_ref, lambda_init)
    for h in range(n_heads):
        a0 = acc_ref[2 * h]
        a1 = acc_ref[2 * h + 1]
        o = a0[:, 0:E_A] / a0[:, E_A:2 * E_A] - lam * (a1[:, 0:E_A] / a1[:, E_A:2 * E_A])
        o_ref[:, h * E_A:(h + 1) * E_A] = _head_rmsnorm(o, g_ref[...], lambda_init).astype(BF16)


def _attention(lam4, subln_g, q, k, v, b0, b1, *, t, lambda_init):
    bsz, s, w_a = q.shape
    n_heads = w_a // E_A
    kern = functools.partial(_attn_kernel, t=t, n_heads=n_heads, lambda_init=lambda_init)
    const = lambda shape: pl.BlockSpec(shape, lambda b, i: (0,) * len(shape))
    seq_spec = pl.BlockSpec((None, s, w_a), lambda b, i: (b, 0, 0))
    return pl.pallas_call(
        kern,
        grid=(bsz, s // t),
        in_specs=[
            const(lam4.shape), const(subln_g.shape),
            pl.BlockSpec((None, t, w_a), lambda b, i: (b, i, 0)),
            seq_spec, seq_spec,
            _resident(b0.shape), _resident(b1.shape),
        ],
        out_specs=pl.BlockSpec((None, t, w_a), lambda b, i: (b, i, 0)),
        out_shape=jax.ShapeDtypeStruct((bsz, s, w_a), BF16),
        scratch_shapes=[pltpu.VMEM((2 * n_heads, t, E_A), BF16),
                        pltpu.VMEM((2 * n_heads, t, LANES), F32),
                        pltpu.VMEM((2 * n_heads, t, E_A + LANES), F32),
                        pltpu.VMEM((t, t), BF16)],
        compiler_params=_params(("parallel", "arbitrary"), 52),
        name="diff_attn_prompt",
    )(lam4, subln_g, q, k, v, b0, b1)


def _decode_attn_kernel(pt_ref, lam_ref, g_ref, q_ref, kn_ref, vn_ref, bias_ref, new_bias_ref, *rest,
                        pages, n_heads, lambda_init):
    del pt_ref
    k_refs = rest[:pages]
    v_refs = rest[pages:2 * pages]
    o_ref, kb_ref, vb_ref, m_ref, l_ref, acc_ref = rest[2 * pages:]
    j = pl.program_id(1)
    last = pl.num_programs(1) - 1
    rows = 2 * n_heads
    page_len = k_refs[0].shape[0]

    @pl.when(j == 0)
    def _():
        m_ref[...] = jnp.full_like(m_ref, NEG_INF)
        l_ref[...] = jnp.zeros_like(l_ref)
        acc_ref[...] = jnp.zeros_like(acc_ref)

    r_id = lax.broadcasted_iota(jnp.int32, (rows, E_A), 0)
    l_id = lax.broadcasted_iota(jnp.int32, (rows, E_A), 1)

    def per_row_head(tok_ref):
        out = jnp.zeros((rows, E_A), F32)
        for h in range(n_heads):
            piece = jnp.broadcast_to(tok_ref[:, h * E_A:(h + 1) * E_A], (rows, E_A))
            out = jnp.where((r_id // 2) == h, piece, out)
        return out

    qm = jnp.where((l_id // DH_A) == (r_id % 2), per_row_head(q_ref), 0.0)

    for i in range(pages):
        kb_ref[i * page_len:(i + 1) * page_len, :] = k_refs[i][...].astype(BF16)
        vb_ref[i * page_len:(i + 1) * page_len, :] = v_refs[i][...].astype(BF16)

    n_keys = pages * page_len
    s = _dot_nt(qm.astype(BF16), kb_ref[...])
    near = jnp.where(j == last, 1.0, 0.0)
    s_tail = s[:, n_keys - page_len:] + near * bias_ref[...]
    s = jnp.concatenate([s[:, :n_keys - page_len], s_tail], axis=1) if pages > 1 else s_tail
    row_s = lax.broadcasted_iota(jnp.int32, s.shape, 0)
    col_s = lax.broadcasted_iota(jnp.int32, s.shape, 1)
    s = jnp.where((col_s % n_heads) == (row_s // 2), s, NEG_INF)

    m_old = m_ref[...]
    m_new = jnp.maximum(m_old, jnp.max(s, axis=-1, keepdims=True))
    alpha = jnp.exp2(m_old - m_new)
    p = jnp.exp2(s - m_new)
    l_ref[...] = alpha * l_ref[...] + jnp.sum(p, axis=-1, keepdims=True)
    acc_ref[...] = alpha * acc_ref[...] + _dot(p.astype(BF16), vb_ref[...])
    m_ref[...] = m_new

    @pl.when(j == last)
    def _():
        s_new = (jnp.sum(qm * per_row_head(kn_ref), axis=-1, keepdims=True) + new_bias_ref[...])
        m_old = m_ref[...]
        m_new = jnp.maximum(m_old, s_new)
        alpha = jnp.exp2(m_old - m_new)
        p_new = jnp.exp2(s_new - m_new)
        l_fin = alpha * l_ref[...] + p_new
        o = (alpha * acc_ref[...] + p_new * per_row_head(vn_ref)) / l_fin
        lam = _diff_lambda(lam_ref, lambda_init)
        g = g_ref[...]
        for h in range(n_heads):
            oh = o[2 * h:2 * h + 1] - lam * o[2 * h + 1:2 * h + 2]
            o_ref[:, h * E_A:(h + 1) * E_A] = _head_rmsnorm(oh, g, lambda_init).astype(BF16)


def _decode_attention(page_table, lam4, subln_g, q, k_new, v_new, bias, new_bias, cache_k, cache_v, *,
                      layer, pages, lambda_init):
    n_seq, _, w_a = q.shape
    n_pages = page_table.shape[1]
    assert n_pages % pages == 0
    n_heads = w_a // E_A
    rows = 2 * n_heads
    page_len = cache_k.shape[2]
    kern = functools.partial(_decode_attn_kernel, pages=pages, n_heads=n_heads,
                             lambda_init=lambda_init)
    const = lambda shape: pl.BlockSpec(shape, lambda b, j, pt: (0,) * len(shape))
    tok = pl.BlockSpec((None, 1, w_a), lambda b, j, pt: (b, 0, 0))

    def page_spec(i):
        return pl.BlockSpec((None, None, page_len, E_A),
                            lambda b, j, pt: (layer, pt[b * n_pages + j * pages + i], 0, 0))

    grid_spec = pltpu.PrefetchScalarGridSpec(
        num_scalar_prefetch=1,
        grid=(n_seq, n_pages // pages),
        in_specs=[const(lam4.shape), const(subln_g.shape), tok, tok, tok, const(bias.shape),
                  const(new_bias.shape)]
        + [page_spec(i) for i in range(pages)] + [page_spec(i) for i in range(pages)],
        out_specs=pl.BlockSpec((None, 1, w_a), lambda b, j, pt: (b, 0, 0)),
        scratch_shapes=[pltpu.VMEM((pages * page_len, E_A), BF16),
                        pltpu.VMEM((pages * page_len, E_A), BF16),
                        pltpu.VMEM((rows, 1), F32), pltpu.VMEM((rows, 1), F32),
                        pltpu.VMEM((rows, E_A), F32)],
    )
    return pl.pallas_call(
        kern,
        grid_spec=grid_spec,
        out_shape=jax.ShapeDtypeStruct((n_seq, 1, w_a), BF16),
        compiler_params=_params(("parallel", "arbitrary"), 52),
        name="diff_attn_decode",
    )(page_table.reshape(-1), lam4, subln_g, q, k_new, v_new, bias, new_bias,
      *([cache_k] * pages), *([cache_v] * pages))


def _mix_out_kernel(x_ref, oa_ref, ob_ref, woa_ref, wob_ref, g_ref, b_ref, wq_ref,
                    x2_ref, qx_ref, *, alpha, qscale):
    y = _dot(oa_ref[...], woa_ref[...]) + _dot(ob_ref[...], wob_ref[...])
    x2 = _layer_norm(alpha * x_ref[...] + y, g_ref[...], b_ref[...])
    x2_ref[...] = x2
    qx_ref[...] = (_dot(x2.astype(BF16), wq_ref[...]) * qscale).astype(BF16)


def _mix_out(x, oa, ob, woa, wob, g, b, wq, *, alpha, qscale, tm):
    n, d = x.shape
    kern = functools.partial(_mix_out_kernel, alpha=alpha, qscale=qscale)
    row = lambda width: pl.BlockSpec((tm, width), lambda i: (i, 0))
    return pl.pallas_call(
        kern,
        grid=(n // tm,),
        in_specs=[row(d), row(oa.shape[1]), row(ob.shape[1]), _resident(woa.shape),
                  _resident(wob.shape), _resident(g.shape), _resident(b.shape), _resident(wq.shape)],
        out_specs=[row(d), row(d)],
        out_shape=[jax.ShapeDtypeStruct((n, d), F32), jax.ShapeDtypeStruct((n, d), BF16)],
        compiler_params=_params(("parallel",), 40),
        name="mix_out_ln_q",
    )(x, oa, ob, woa, wob, g, b, wq)


def _post_mix_kernel(x_ref, oa_ref, ob_ref, mk_ref, mv_ref, woa_ref, wob_ref, g1_ref, b1_ref,
                     wq_ref, wo_ref, g2_ref, b2_ref, y_ref, o_scr, x2_scr, xb_scr, q_scr,
                     *, alpha, qscale, n_heads):
    tm, d = x_ref.shape
    dh = d // n_heads
    subs = [slice(r0, r0 + SUB_ROWS) for r0 in range(0, tm, SUB_ROWS)]
    for rs in subs:
        y = _dot(oa_ref[rs, :], woa_ref[...]) + _dot(ob_ref[rs, :], wob_ref[...])
        x2 = _layer_norm(alpha * x_ref[rs, :] + y, g1_ref[...], b1_ref[...])
        x2_scr[rs, :] = x2
        xb_scr[rs, :] = x2.astype(BF16)
    for rs in subs:
        q_scr[rs, :] = (_dot(xb_scr[rs, :], wq_ref[...]) * qscale).astype(BF16)
    for h in range(n_heads):
        cols = slice(h * dh, (h + 1) * dh)
        s = _dot_nt(q_scr[:, cols], mk_ref[:, cols])
        p = jnp.exp2(s - jnp.max(s, axis=-1, keepdims=True))
        l = jnp.sum(p, axis=-1, keepdims=True)
        o_scr[:, cols] = (_dot(p.astype(BF16), mv_ref[:, cols]) / l).astype(BF16)
    for rs in subs:
        y2 = _dot(o_scr[rs, :], wo_ref[...])
        y_ref[rs, :] = _layer_norm(alpha * x2_scr[rs, :] + y2, g2_ref[...], b2_ref[...])


def _post_mix(x, oa, ob, mk, mv, woa, wob, g1, b1, wq, wo, g2, b2, *, alpha, qscale, n_heads, tm):
    bsz, s, d = x.shape
    n_mem = mk.shape[1]
    kern = functools.partial(_post_mix_kernel, alpha=alpha, qscale=qscale, n_heads=n_heads)
    row = lambda width: pl.BlockSpec((None, tm, width), lambda b, i: (b, i, 0))
    mem = pl.BlockSpec((None, n_mem, d), lambda b, i: (b, 0, 0))
    weights = [woa, wob, g1, b1, wq, wo, g2, b2]
    return pl.pallas_call(
        kern,
        grid=(bsz, s // tm),
        in_specs=[row(d), row(oa.shape[2]), row(ob.shape[2]), mem, mem]
        + [_resident(w.shape) for w in weights],
        out_specs=row(d),
        out_shape=jax.ShapeDtypeStruct((bsz, s, d), F32),
        scratch_shapes=[pltpu.VMEM((tm, d), BF16), pltpu.VMEM((tm, d), F32),
                        pltpu.VMEM((tm, d), BF16), pltpu.VMEM((tm, d), BF16)],
        compiler_params=_params(("parallel", "parallel"), 48),
        name="post_mix",
    )(x, oa, ob, mk, mv, *weights)


def _cross_decode_kernel(q_ref, mk_ref, mv_ref, o_ref, *, n_heads):
    n_seq, n_mem, _, dh = mk_ref.shape
    n_keys = n_mem * n_heads
    sublanes = SUBLANES
    assert n_heads <= sublanes
    r_id = lax.broadcasted_iota(jnp.int32, (sublanes, LANES), 0)
    for i in range(n_seq):
        q = q_ref[i].astype(F32)
        s = None
        for c in range(dh // LANES):
            qm = jnp.zeros((sublanes, LANES), F32)
            for h in range(n_heads):
                lo = h * dh + c * LANES
                piece = jnp.broadcast_to(q[:, lo:lo + LANES], (sublanes, LANES))
                qm = jnp.where((r_id % n_heads) == h, piece, qm)
            xk = mk_ref[i, :, :, c * LANES:(c + 1) * LANES].reshape(n_keys, LANES).astype(BF16)
            part = _dot_nt(qm.astype(BF16), xk)
            s = part if s is None else s + part
        row_s = lax.broadcasted_iota(jnp.int32, s.shape, 0)
        col_s = lax.broadcasted_iota(jnp.int32, s.shape, 1)
        s = jnp.where((col_s % n_heads) == (row_s % n_heads), s, NEG_INF)
        p = jnp.exp2(s - jnp.max(s, axis=-1, keepdims=True))
        l = jnp.sum(p, axis=-1, keepdims=True)
        pb = p.astype(BF16)
        for c in range(dh // LANES):
            xv = mv_ref[i, :, :, c * LANES:(c + 1) * LANES].reshape(n_keys, LANES).astype(BF16)
            o = _dot(pb, xv) / l
            for h in range(n_heads):
                lo = h * dh + c * LANES
                o_ref[i, :, lo:lo + LANES] = o[h:h + 1].astype(BF16)


def _cross_attention_decode(qx, mk, mv, *, layer, n_heads, seqs_per_step):
    n_seq, _, d = qx.shape
    n_mem, _, dh = mk.shape[2:]
    assert n_seq % seqs_per_step == 0
    kern = functools.partial(_cross_decode_kernel, n_heads=n_heads)
    tok_spec = pl.BlockSpec((seqs_per_step, 1, d), lambda b: (b, 0, 0))
    mem_spec = pl.BlockSpec((None, seqs_per_step, n_mem, n_heads, dh),
                            lambda b: (layer, b, 0, 0, 0))
    return pl.pallas_call(
        kern,
        grid=(n_seq // seqs_per_step,),
        in_specs=[tok_spec, mem_spec, mem_spec],
        out_specs=tok_spec,
        out_shape=jax.ShapeDtypeStruct((n_seq, 1, d), BF16),
        compiler_params=_params(("parallel",), 32),
        name="cross_attn_decode",
    )(qx, mk, mv)


def _proj_ln_kernel(x_ref, o_ref, w_ref, g_ref, b_ref, y_ref, *, alpha):
    y = _dot(o_ref[...], w_ref[...])
    y_ref[...] = _layer_norm(alpha * x_ref[...] + y, g_ref[...], b_ref[...])


def _proj_ln(x, o, w, g, b, *, alpha, tm):
    n, d = x.shape
    kern = functools.partial(_proj_ln_kernel, alpha=alpha)
    row = pl.BlockSpec((tm, d), lambda i: (i, 0))
    return pl.pallas_call(
        kern,
        grid=(n // tm,),
        in_specs=[row, row, _resident(w.shape), _resident(g.shape), _resident(b.shape)],
        out_specs=row,
        out_shape=jax.ShapeDtypeStruct((n, d), F32),
        compiler_params=_params(("parallel",), 32),
        name="proj_ln",
    )(x, o, w, g, b)


def _mem_kv_kernel(m_ref, w_ref, k_ref, v_ref, kb_ref, vb_ref):
    d = kb_ref.shape[1]
    n_heads, dh = k_ref.shape[1:]
    mb = m_ref[...].astype(BF16)
    k = _dot(mb, w_ref[:, 0:d])
    v = _dot(mb, w_ref[:, d:2 * d])
    kb_ref[...] = k.astype(BF16)
    vb_ref[...] = v.astype(BF16)
    k_ref[...] = k.reshape(k.shape[0], n_heads, dh)
    v_ref[...] = v.reshape(v.shape[0], n_heads, dh)


def _mem_kv(mem, w, *, n_heads, tm):
    n, d = mem.shape
    dh = d // n_heads
    row = pl.BlockSpec((tm, d), lambda i: (i, 0))
    row_h = pl.BlockSpec((tm, n_heads, dh), lambda i: (i, 0, 0))
    return pl.pallas_call(
        _mem_kv_kernel,
        grid=(n // tm,),
        in_specs=[row, _resident(w.shape)],
        out_specs=[row_h, row_h, row, row],
        out_shape=[jax.ShapeDtypeStruct((n, n_heads, dh), F32)] * 2
        + [jax.ShapeDtypeStruct((n, d), BF16)] * 2,
        compiler_params=_params(("parallel",), 32),
        name="mem_kv",
    )(mem, w)


def _shifted_bias(table, dist):
    n = jnp.maximum(dist, 0)
    max_exact = NB // 2
    nf = jnp.maximum(n, 1).astype(F32)
    large = max_exact + (jnp.log(nf / max_exact) / math.log(MAX_DIST / max_exact)
                         * (NB - max_exact)).astype(jnp.int32)
    large = jnp.minimum(large, NB - 1)
    bucket = jnp.where(n < max_exact, n, large)
    tab = (table.astype(F32) - table[NB - 1].astype(F32)) * LOG2E
    hit = bucket[..., None, None] == jnp.arange(NB)[:, None]
    return jnp.sum(jnp.where(hit, tab, 0.0), axis=-2)


def _place_blocks(sel, tile):
    nb, n = sel.shape[0], tile.shape[1]
    out = sel[None, :, None, :, None] * tile[:, None, :, None, :]
    return out.reshape(tile.shape[0], nb * n, nb * n)


def _toeplitz(f_pos, f_neg, t):
    period = 2 * t
    v = jnp.concatenate([f_neg, jnp.zeros_like(f_pos[:1]), f_pos[:0:-1]], axis=0)
    tiled = jnp.tile(v, (t, 1))[: t * (period - 1)]
    skew = tiled.reshape(t, period - 1, -1)
    return jnp.transpose(skew[:, :t], (2, 0, 1))


def kernel(x_prompt, x_sample, mem_prompt, cache_k, cache_v, cache_mem_k, cache_mem_v, page_table, rel_bias, ln_g, ln_b, ffn1_w_in, ffn1_w_out, w_mix_in, w_mix_out, lambda_q1, lambda_k1, lambda_q2, lambda_k2, subln_g, sgu_ln_g, sgu_ln_b, sgu_w, sgu_b, xq_w, xkv_w, xo_w, ffn2_w_in, ffn2_w_out):
    bsz, seq, d = x_prompt.shape
    n_dec = x_sample.shape[0]
    depth = ln_g.shape[0]
    assert depth == 1 and x_sample.shape[1] == 1
    w_b = sgu_ln_g.shape[1]
    w_a = (w_mix_in.shape[2] - 2 * w_b) // 3
    n_heads = w_a // E_A
    n_groups = sgu_w.shape[1]
    n_mem, h_m, dh_m = cache_mem_k.shape[2:]
    page_rows = cache_k.shape[2]
    alpha = (2 * depth) ** 0.25
    lambda_init = 0.8 - 0.6 * math.exp(-0.3 * 0)
    l = 0
    t_attn = ATTN_TILE
    tm = TOKEN_TILE
    assert seq % tm == 0 and seq % t_attn == 0 and t_attn % MAX_DIST == 0 and tm % SUB_ROWS == 0
    assert SUB_ROWS % CHUNK == 0
    assert page_rows == CHUNK and MAX_DIST <= page_rows

    row2 = lambda a: a.reshape(1, -1)
    g_ln = [row2(ln_g[l, i]) for i in range(4)]
    b_ln = [row2(ln_b[l, i]) for i in range(4)]
    ffn1 = (ffn1_w_in[l].astype(BF16), ffn1_w_out[l].astype(BF16))
    ffn2 = (ffn2_w_in[l].astype(BF16), ffn2_w_out[l].astype(BF16))
    q_fold = jnp.concatenate([jnp.full((w_a,), DH_A ** -0.5, F32),
                              jnp.ones((w_mix_in.shape[2] - w_a,), F32)])
    w_mix = (w_mix_in[l] * q_fold).astype(BF16)
    woa = w_mix_out[l, :w_a].astype(BF16)
    wob = w_mix_out[l, w_a:].astype(BF16)
    wq_x = xq_w[l].astype(BF16)
    wo_x = xo_w[l].astype(BF16)
    wkv = xkv_w[l].astype(BF16)
    lam4 = jnp.stack([lambda_q1[l], lambda_k1[l], lambda_q2[l], lambda_k2[l]]).astype(F32)
    g_sub = row2(subln_g[l])
    lng_s, lnb_s = row2(sgu_ln_g[l]), row2(sgu_ln_b[l])
    tril_w = jnp.tril(sgu_w[l]).astype(BF16)
    sgu_bias = sgu_b[l].reshape(n_groups, CHUNK, 1)
    cg = w_b // n_groups
    w00 = jnp.repeat(sgu_w[l, :, 0, 0], cg).reshape(1, w_b)
    b00 = jnp.repeat(sgu_b[l, :, 0], cg).reshape(1, w_b)
    xq_scale = dh_m ** -0.5 * LOG2E

    ar = jnp.arange(MAX_DIST)
    near = _toeplitz(_shifted_bias(rel_bias, ar), _shifted_bias(rel_bias, 0 * ar), MAX_DIST)
    edge = _toeplitz(_shifted_bias(rel_bias, MAX_DIST + ar), _shifted_bias(rel_bias, MAX_DIST - ar),
                     MAX_DIST)
    nb = t_attn // MAX_DIST
    bias0 = (_place_blocks(jnp.eye(nb, dtype=F32), near)
             + _place_blocks(jnp.eye(nb, k=-1, dtype=F32), edge))
    dec_near = _shifted_bias(rel_bias, page_rows - jnp.arange(page_rows))
    dec_bias = jnp.repeat(jnp.repeat(dec_near.T, 2, axis=0), n_heads, axis=1)
    new_bias = jnp.repeat(_shifted_bias(rel_bias, jnp.zeros((1,), jnp.int32)).T, 2, axis=0)

    n_tok = bsz * seq
    xp = x_prompt.reshape(n_tok, d)
    mk_p, mv_p, mkb, mvb = _mem_kv(mem_prompt.reshape(bsz * n_mem, d), wkv, n_heads=h_m, tm=n_mem)
    x1 = _ffn_ln(xp, *ffn1, g_ln[0], b_ln[0], alpha=alpha, tm=tm)
    qb, k_p, kb, v_p, vb, ob = _mix_in(x1, w_mix, lng_s, lnb_s, tril_w, sgu_bias,
                                       w_a=w_a, w_b=w_b, qscale=LOG2E, tm=tm)
    oa = _attention(lam4, g_sub, qb.reshape(bsz, seq, w_a), kb.reshape(bsz, seq, w_a),
                    vb.reshape(bsz, seq, w_a), bias0, edge, t=t_attn, lambda_init=lambda_init)
    x3 = _post_mix(x1.reshape(bsz, seq, d), oa, ob.reshape(bsz, seq, w_b),
                   mkb.reshape(bsz, n_mem, d), mvb.reshape(bsz, n_mem, d),
                   woa, wob, g_ln[1], b_ln[1], wq_x, wo_x, g_ln[2], b_ln[2],
                   alpha=alpha, qscale=xq_scale, n_heads=h_m, tm=tm)
    y_p = _ffn_ln(x3.reshape(n_tok, d), *ffn2, g_ln[3], b_ln[3], alpha=alpha, tm=tm)

    xs = x_sample.reshape(n_dec, d)
    s1 = _ffn_ln(xs, *ffn1, g_ln[0], b_ln[0], alpha=alpha, tm=n_dec)
    q_s, k_s, v_s, ob_s, vn_s = _mix_in_decode(s1, w_mix, lng_s, lnb_s, w00, b00,
                                               w_a=w_a, w_b=w_b, qscale=LOG2E)
    oa_s = _decode_attention(page_table, lam4, g_sub, q_s.reshape(n_dec, 1, w_a),
                             k_s.reshape(n_dec, 1, w_a), v_s.reshape(n_dec, 1, w_a), dec_bias, new_bias,
                             cache_k.reshape(depth, -1, page_rows * n_heads, E_A),
                             cache_v.reshape(depth, -1, page_rows * n_heads, E_A),
                             layer=l, pages=DECODE_PAGES, lambda_init=lambda_init)
    s2, qx_s = _mix_out(s1, oa_s.reshape(n_dec, w_a), ob_s, woa, wob, g_ln[1], b_ln[1], wq_x,
                        alpha=alpha, qscale=xq_scale, tm=n_dec)
    ox_s = _cross_attention_decode(qx_s.reshape(n_dec, 1, d),
                                   cache_mem_k, cache_mem_v,
                                   layer=l, n_heads=h_m, seqs_per_step=CROSS_DECODE_SEQS)
    s3 = _proj_ln(s2, ox_s.reshape(n_dec, d), wo_x, g_ln[2], b_ln[2], alpha=alpha, tm=n_dec)
    y_s = _ffn_ln(s3, *ffn2, g_ln[3], b_ln[3], alpha=alpha, tm=n_dec)

    return (y_p.reshape(bsz, seq, d), y_s.reshape(n_dec, 1, d),
            k_p.reshape(1, bsz, seq, n_heads, E_A), v_p.reshape(1, bsz, seq, n_heads, E_A),
            mk_p.reshape(1, bsz, n_mem, h_m, dh_m), mv_p.reshape(1, bsz, n_mem, h_m, dh_m),
            k_s.reshape(1, n_dec, 1, n_heads, E_A), v_s.reshape(1, n_dec, 1, n_heads, E_A),
            vn_s.reshape(1, n_dec, 1, w_b))
```

```python
import functools
import math

import jax
import jax.numpy as jnp
from jax import lax
from jax.experimental import pallas as pl
from jax.experimental.pallas import tpu as pltpu

F32 = jnp.float32
BF16 = jnp.bfloat16

LN_EPS = 1e-5
NEG_INF = -1e30
LOG2E = 1.4426950408889634
DH_A = 64
E_A = 2 * DH_A
CHUNK = 128
NB = 32
MAX_DIST = 128

LANES = 128
SUBLANES = 8
MXU_EDGE = 256
MIB = 1024 * 1024

TOKEN_TILE = 1024
SUB_ROWS = 256
ATTN_TILE = 512
DECODE_PAGES = 32
CROSS_DECODE_SEQS = 4


def _params(semantics, vmem_mib):
    return pltpu.CompilerParams(dimension_semantics=semantics, vmem_limit_bytes=vmem_mib * MIB)


def _resident(shape):
    nd = len(shape)
    return pl.BlockSpec(shape, lambda *_: (0,) * nd, pipeline_mode=pl.Buffered(1))


def _layer_norm(x, g, b):
    mu = jnp.mean(x, -1, keepdims=True)
    xc = x - mu
    var = jnp.mean(xc * xc, -1, keepdims=True)
    return xc * lax.rsqrt(var + LN_EPS) * g + b


def _gelu(x):
    return 0.5 * x * (1.0 + lax.erf(x * math.sqrt(0.5)))


def _dot(a, b):
    return jnp.dot(a, b, preferred_element_type=F32)


def _dot_nt(a, b):
    return lax.dot_general(a, b, (((1,), (1,)), ((), ())), preferred_element_type=F32)


def _diff_lambda(lam_ref, lambda_init):
    lv = lam_ref[...]
    a = jnp.sum(lv[0:1] * lv[1:2], axis=-1, keepdims=True)
    b = jnp.sum(lv[2:3] * lv[3:4], axis=-1, keepdims=True)
    return jnp.exp(a) - jnp.exp(b) + lambda_init


def _head_rmsnorm(o, g, lambda_init):
    return o * lax.rsqrt(jnp.mean(o * o, -1, keepdims=True) + LN_EPS) * g * (1.0 - lambda_init)


def _ffn_ln_kernel(x_ref, wi_ref, wo_ref, g_ref, b_ref, o_ref, act_ref, xb_ref, *, alpha, fc):
    tm = x_ref.shape[0]
    sub = min(tm, SUB_ROWS)
    subs = [slice(r0, r0 + sub) for r0 in range(0, tm, sub)]
    xb_ref[...] = x_ref[...].astype(BF16)
    d_ff = wo_ref.shape[0]
    for j in range(d_ff // fc):
        cols = slice(j * fc, (j + 1) * fc)
        up_cols = slice(d_ff + j * fc, d_ff + (j + 1) * fc)
        for rs in subs:
            ha = _dot(xb_ref[rs, :], wi_ref[:, cols])
            hb = _dot(xb_ref[rs, :], wi_ref[:, up_cols])
            act_ref[rs, cols] = (ha * jax.nn.sigmoid(ha) * hb).astype(BF16)
    for rs in subs:
        y = _dot(act_ref[rs, :], wo_ref[...])
        o_ref[rs, :] = _layer_norm(alpha * x_ref[rs, :] + 0.5 * y, g_ref[...], b_ref[...])


def _ffn_ln(x, wi, wo, g, b, *, alpha, tm):
    n, d = x.shape
    d_ff = wo.shape[0]
    assert d_ff % MXU_EDGE == 0 and wi.shape[1] == 2 * d_ff
    kern = functools.partial(_ffn_ln_kernel, alpha=alpha, fc=MXU_EDGE)
    return pl.pallas_call(
        kern,
        grid=(n // tm,),
        in_specs=[
            pl.BlockSpec((tm, d), lambda i: (i, 0)),
            _resident(wi.shape), _resident(wo.shape), _resident(g.shape), _resident(b.shape),
        ],
        out_specs=pl.BlockSpec((tm, d), lambda i: (i, 0)),
        out_shape=jax.ShapeDtypeStruct((n, d), F32),
        scratch_shapes=[pltpu.VMEM((tm, d_ff), BF16), pltpu.VMEM((tm, d), BF16)],
        compiler_params=_params(("parallel",), 52),
        name="ffn_ln",
    )(x, wi, wo, g, b)


def _mix_in_kernel(x_ref, w_ref, lng_ref, lnb_ref, tw_ref, sb_ref,
                   q_ref, k_ref, kb_ref, v_ref, vb_ref, ob_ref, xb_ref, *, w_a, w_b, qscale):
    tm = x_ref.shape[0]
    n_groups = tw_ref.shape[0]
    cg = w_b // n_groups
    starts = list(range(0, tm, SUB_ROWS))
    gated = {}

    def gate_inputs(r0):
        rs = slice(r0, r0 + SUB_ROWS)
        xb_ref[rs, :] = x_ref[rs, :].astype(BF16)
        gv = _gelu(_dot(xb_ref[rs, :], w_ref[:, 3 * w_a + w_b:3 * w_a + 2 * w_b]))
        gu = _gelu(_dot(xb_ref[rs, :], w_ref[:, 3 * w_a:3 * w_a + w_b]))
        gated[r0] = (gu, _layer_norm(gv, lng_ref[...], lnb_ref[...]).astype(BF16))

    def qkv(r0):
        rs = slice(r0, r0 + SUB_ROWS)
        q_ref[rs, :] = (_dot(xb_ref[rs, :], w_ref[:, 0:w_a]) * qscale).astype(BF16)
        hk = _dot(xb_ref[rs, :], w_ref[:, w_a:2 * w_a])
        kb_ref[rs, :] = hk.astype(BF16)
        hv = _dot(xb_ref[rs, :], w_ref[:, 2 * w_a:3 * w_a])
        vb_ref[rs, :] = hv.astype(BF16)
        k_ref[rs] = hk.reshape(SUB_ROWS, w_a // E_A, E_A)
        v_ref[rs] = hv.reshape(SUB_ROWS, w_a // E_A, E_A)

    def gate(r0):
        gu, vn = gated.pop(r0)
        n_chunks = SUB_ROWS // CHUNK
        for g in range(n_groups):
            cols = slice(g * cg, (g + 1) * cg)
            v_wide = jnp.concatenate(
                [vn[c * CHUNK:(c + 1) * CHUNK, cols] for c in range(n_chunks)], axis=1)
            mixed = _dot(tw_ref[g], v_wide)
            for c in range(n_chunks):
                rows = slice(c * CHUNK, (c + 1) * CHUNK)
                m_c = mixed[:, c * cg:(c + 1) * cg] + sb_ref[g]
                ob_ref[r0 + c * CHUNK:r0 + (c + 1) * CHUNK, cols] = (gu[rows, cols] * m_c).astype(BF16)

    gate_inputs(starts[0])
    for cur, nxt in zip(starts, starts[1:] + [None]):
        if nxt is not None:
            gate_inputs(nxt)
        qkv(cur)
        gate(cur)


def _mix_in(x, w, lng, lnb, tw, sb, *, w_a, w_b, qscale, tm):
    n, d = x.shape
    kern = functools.partial(_mix_in_kernel, w_a=w_a, w_b=w_b, qscale=qscale)
    row = lambda width: pl.BlockSpec((tm, width), lambda i: (i, 0))
    shp = lambda width, dt: jax.ShapeDtypeStruct((n, width), dt)
    n_heads = w_a // E_A
    row_h = pl.BlockSpec((tm, n_heads, E_A), lambda i: (i, 0, 0))
    shp_h = jax.ShapeDtypeStruct((n, n_heads, E_A), F32)
    return pl.pallas_call(
        kern,
        grid=(n // tm,),
        in_specs=[row(d), _resident(w.shape), _resident(lng.shape), _resident(lnb.shape),
                  _resident(tw.shape), _resident(sb.shape)],
        out_specs=[row(w_a), row_h, row(w_a), row_h, row(w_a), row(w_b)],
        out_shape=[shp(w_a, BF16), shp_h, shp(w_a, BF16), shp_h, shp(w_a, BF16), shp(w_b, BF16)],
        scratch_shapes=[pltpu.VMEM((tm, d), BF16)],
        compiler_params=_params(("parallel",), 48),
        name="mix_in_sgu",
    )(x, w, lng, lnb, tw, sb)


def _mix_in_decode_kernel(x_ref, w_ref, lng_ref, lnb_ref, w00_ref, b0_ref,
                          q_ref, k_ref, v_ref, ob_ref, vn_ref, *, w_a, w_b, qscale):
    xb = x_ref[...].astype(BF16)
    q_ref[...] = _dot(xb, w_ref[:, 0:w_a]) * qscale
    k_ref[...] = _dot(xb, w_ref[:, w_a:2 * w_a])
    v_ref[...] = _dot(xb, w_ref[:, 2 * w_a:3 * w_a])
    gu = _gelu(_dot(xb, w_ref[:, 3 * w_a:3 * w_a + w_b]))
    gv = _gelu(_dot(xb, w_ref[:, 3 * w_a + w_b:3 * w_a + 2 * w_b]))
    vn = _layer_norm(gv, lng_ref[...], lnb_ref[...])
    vn_ref[...] = vn
    ob_ref[...] = (gu * (w00_ref[...] * vn + b0_ref[...])).astype(BF16)


def _mix_in_decode(x, w, lng, lnb, w00, b0, *, w_a, w_b, qscale):
    n, d = x.shape
    kern = functools.partial(_mix_in_decode_kernel, w_a=w_a, w_b=w_b, qscale=qscale)
    full = lambda shape: pl.BlockSpec(shape, lambda i: (0,) * len(shape))
    return pl.pallas_call(
        kern,
        grid=(1,),
        in_specs=[full(x.shape), full(w.shape), full(lng.shape), full(lnb.shape),
                  full(w00.shape), full(b0.shape)],
        out_specs=[full((n, w_a)), full((n, w_a)), full((n, w_a)), full((n, w_b)), full((n, w_b))],
        out_shape=[jax.ShapeDtypeStruct((n, w_a), F32), jax.ShapeDtypeStruct((n, w_a), F32),
                   jax.ShapeDtypeStruct((n, w_a), F32), jax.ShapeDtypeStruct((n, w_b), BF16),
                   jax.ShapeDtypeStruct((n, w_b), F32)],
        compiler_params=_params(("arbitrary",), 32),
        name="mix_in_decode",
    )(x, w, lng, lnb, w00, b0)


def _attn_kernel(lam_ref, g_ref, q_ref, k_ref, v_ref, b0_ref, b1_ref, o_ref,
                 qq_ref, m_ref, acc_ref, p_last, *, t, n_heads, lambda_init):
    qi = pl.program_id(1)
    lane = lax.broadcasted_iota(jnp.int32, (t, E_A), 1)
    for h in range(n_heads):
        q = q_ref[:, h * E_A:(h + 1) * E_A]
        zero = jnp.zeros_like(q)
        qq_ref[2 * h] = jnp.where(lane < DH_A, q, zero)
        qq_ref[2 * h + 1] = jnp.where(lane < DH_A, zero, q)
    ones = jnp.ones((t, LANES), BF16)

    last_u = 2 * n_heads - 1
    last_cols = slice((n_heads - 1) * E_A, n_heads * E_A)

    def flush_last(j_done):
        off = pl.multiple_of(j_done * t, t)
        va = jnp.concatenate([v_ref[pl.ds(off, t), last_cols], ones], axis=1)
        acc_ref[last_u] += _dot(p_last[...], va)

    def update(u, blocks, va, first):
        lane_max = functools.reduce(jnp.maximum, blocks)
        row_max = jnp.broadcast_to(jnp.max(lane_max, axis=-1, keepdims=True), lane_max.shape)
        m_new = row_max if first else jnp.maximum(m_ref[u], row_max)
        p = jnp.concatenate([jnp.exp2(blk - m_new).astype(BF16) for blk in blocks], axis=1)
        if first:
            if u == last_u:
                acc_ref[u] = jnp.zeros(acc_ref.shape[1:], F32)
            else:
                acc_ref[u] = _dot(p, va)
        else:
            alpha = jnp.exp2(m_ref[u] - m_new)
            alpha2 = jnp.concatenate([alpha, alpha], axis=1)
            if u == last_u:
                acc_ref[u] = alpha2 * acc_ref[u]
            else:
                acc_ref[u] = alpha2 * acc_ref[u] + _dot(p, va)
        if u == last_u:
            p_last[...] = p
        m_ref[u] = m_new

    def split(s):
        return [s[:, i * LANES:(i + 1) * LANES] for i in range(s.shape[1] // LANES)]

    def step(j, kind, j_pending):
        if j_pending is not None:
            flush_last(j_pending)
        off = pl.multiple_of(j * t, t)
        if kind == "diag":
            visible = (lax.broadcasted_iota(jnp.int32, (t, t), 0)
                       >= lax.broadcasted_iota(jnp.int32, (t, t), 1))
        for h in range(n_heads):
            cols = slice(h * E_A, (h + 1) * E_A)
            kt = k_ref[pl.ds(off, t), cols]
            va = jnp.concatenate([v_ref[pl.ds(off, t), cols], ones], axis=1)
            for u in (2 * h, 2 * h + 1):
                s = _dot_nt(qq_ref[u], kt)
                if kind == "diag":
                    s = jnp.where(visible, s + b0_ref[h], NEG_INF)
                blocks = split(s)
                if kind == "prev":
                    corner = blocks[-1]
                    blocks[-1] = jnp.concatenate(
                        [corner[0:MAX_DIST] + b1_ref[h], corner[MAX_DIST:]], axis=0)
                update(u, blocks, va, first=(kind == "diag"))

    step(qi, "diag", None)

    @pl.when(qi >= 1)
    def _():
        step(qi - 1, "prev", qi)

    n_far = jnp.maximum(qi - 1, 0)

    def far_pair(i, carry):
        j = 2 * i
        step(j, "far", jnp.where(i == 0, qi - 1, j - 1))
        step(j + 1, "far", j)
        return carry

    lax.fori_loop(0, n_far // 2, far_pair, 0)

    @pl.when(n_far % 2 == 1)
    def _():
        step(n_far - 1, "far", jnp.where(n_far == 1, qi - 1, n_far - 2))

    flush_last(jnp.maximum(qi - 2, 0))

    lam = _diff_lambda(lam_ref, lambda_init)
    for h in range(n_heads):
        a0 = acc_ref[2 * h]
        a1 = acc_ref[2 * h + 1]
        o = a0[:, 0:E_A] / a0[:, E_A:2 * E_A] - lam * (a1[:, 0:E_A] / a1[:, E_A:2 * E_A])
        o_ref[:, h * E_A:(h + 1) * E_A] = _head_rmsnorm(o, g_ref[...], lambda_init).astype(BF16)


def _attention(lam4, subln_g, q, k, v, b0, b1, *, t, lambda_init):
    bsz, s, w_a = q.shape
    n_heads = w_a // E_A
    kern = functools.partial(_attn_kernel, t=t, n_heads=n_heads, lambda_init=lambda_init)
    const = lambda shape: pl.BlockSpec(shape, lambda b, i: (0,) * len(shape))
    seq_spec = pl.BlockSpec((None, s, w_a), lambda b, i: (b, 0, 0))
    return pl.pallas_call(
        kern,
        grid=(bsz, s // t),
        in_specs=[
            const(lam4.shape), const(subln_g.shape),
            pl.BlockSpec((None, t, w_a), lambda b, i: (b, i, 0)),
            seq_spec, seq_spec,
            _resident(b0.shape), _resident(b1.shape),
        ],
        out_specs=pl.BlockSpec((None, t, w_a), lambda b, i: (b, i, 0)),
        out_shape=jax.ShapeDtypeStruct((bsz, s, w_a), BF16),
        scratch_shapes=[pltpu.VMEM((2 * n_heads, t, E_A), BF16),
                        pltpu.VMEM((2 * n_heads, t, LANES), F32),
                        pltpu.VMEM((2 * n_heads, t, E_A + LANES), F32),
                        pltpu.VMEM((t, t), BF16)],
        compiler_params=_params(("parallel", "arbitrary"), 52),
        name="diff_attn_prompt",
    )(lam4, subln_g, q, k, v, b0, b1)


def _decode_attn_kernel(pt_ref, lam_ref, g_ref, q_ref, kn_ref, vn_ref, bias_ref, new_bias_ref, *rest,
                        pages, n_heads, lambda_init):
    del pt_ref
    k_refs = rest[:pages]
    v_refs = rest[pages:2 * pages]
    o_ref, kb_ref, vb_ref, m_ref, l_ref, acc_ref = rest[2 * pages:]
    j = pl.program_id(1)
    last = pl.num_programs(1) - 1
    rows = 2 * n_heads
    page_len = k_refs[0].shape[0]

    @pl.when(j == 0)
    def _():
        m_ref[...] = jnp.full_like(m_ref, NEG_INF)
        l_ref[...] = jnp.zeros_like(l_ref)
        acc_ref[...] = jnp.zeros_like(acc_ref)

    r_id = lax.broadcasted_iota(jnp.int32, (rows, E_A), 0)
    l_id = lax.broadcasted_iota(jnp.int32, (rows, E_A), 1)

    def per_row_head(tok_ref):
        out = jnp.zeros((rows, E_A), F32)
        for h in range(n_heads):
            piece = jnp.broadcast_to(tok_ref[:, h * E_A:(h + 1) * E_A], (rows, E_A))
            out = jnp.where((r_id // 2) == h, piece, out)
        return out

    qm = jnp.where((l_id // DH_A) == (r_id % 2), per_row_head(q_ref), 0.0)

    for i in range(pages):
        kb_ref[i * page_len:(i + 1) * page_len, :] = k_refs[i][...].astype(BF16)
        vb_ref[i * page_len:(i + 1) * page_len, :] = v_refs[i][...].astype(BF16)

    n_keys = pages * page_len
    s = _dot_nt(qm.astype(BF16), kb_ref[...])
    near = jnp.where(j == last, 1.0, 0.0)
    s_tail = s[:, n_keys - page_len:] + near * bias_ref[...]
    s = jnp.concatenate([s[:, :n_keys - page_len], s_tail], axis=1) if pages > 1 else s_tail
    row_s = lax.broadcasted_iota(jnp.int32, s.shape, 0)
    col_s = lax.broadcasted_iota(jnp.int32, s.shape, 1)
    s = jnp.where((col_s % n_heads) == (row_s // 2), s, NEG_INF)

    m_old = m_ref[...]
    m_new = jnp.maximum(m_old, jnp.max(s, axis=-1, keepdims=True))
    alpha = jnp.exp2(m_old - m_new)
    p = jnp.exp2(s - m_new)
    l_ref[...] = alpha * l_ref[...] + jnp.sum(p, axis=-1, keepdims=True)
    acc_ref[...] = alpha * acc_ref[...] + _dot(p.astype(BF16), vb_ref[...])
    m_ref[...] = m_new

    @pl.when(j == last)
    def _():
        s_new = (jnp.sum(qm * per_row_head(kn_ref), axis=-1, keepdims=True) + new_bias_ref[...])
        m_old = m_ref[...]
        m_new = jnp.maximum(m_old, s_new)
        alpha = jnp.exp2(m_old - m_new)
        p_new = jnp.exp2(s_new - m_new)
        l_fin = alpha * l_ref[...] + p_new
        o = (alpha * acc_ref[...] + p_new * per_row_head(vn_ref)) / l_fin
        lam = _diff_lambda(lam_ref, lambda_init)
        g = g_ref[...]
        for h in range(n_heads):
            oh = o[2 * h:2 * h + 1] - lam * o[2 * h + 1:2 * h + 2]
            o_ref[:, h * E_A:(h + 1) * E_A] = _head_rmsnorm(oh, g, lambda_init).astype(BF16)


def _decode_attention(page_table, lam4, subln_g, q, k_new, v_new, bias, new_bias, cache_k, cache_v, *,
                      layer, pages, lambda_init):
    n_seq, _, w_a = q.shape
    n_pages = page_table.shape[1]
    assert n_pages % pages == 0
    n_heads = w_a // E_A
    rows = 2 * n_heads
    page_len = cache_k.shape[2]
    kern = functools.partial(_decode_attn_kernel, pages=pages, n_heads=n_heads,
                             lambda_init=lambda_init)
    const = lambda shape: pl.BlockSpec(shape, lambda b, j, pt: (0,) * len(shape))
    tok = pl.BlockSpec((None, 1, w_a), lambda b, j, pt: (b, 0, 0))

    def page_spec(i):
        return pl.BlockSpec((None, None, page_len, E_A),
                            lambda b, j, pt: (layer, pt[b * n_pages + j * pages + i], 0, 0))

    grid_spec = pltpu.PrefetchScalarGridSpec(
        num_scalar_prefetch=1,
        grid=(n_seq, n_pages // pages),
        in_specs=[const(lam4.shape), const(subln_g.shape), tok, tok, tok, const(bias.shape),
                  const(new_bias.shape)]
        + [page_spec(i) for i in range(pages)] + [page_spec(i) for i in range(pages)],
        out_specs=pl.BlockSpec((None, 1, w_a), lambda b, j, pt: (b, 0, 0)),
        scratch_shapes=[pltpu.VMEM((pages * page_len, E_A), BF16),
                        pltpu.VMEM((pages * page_len, E_A), BF16),
                        pltpu.VMEM((rows, 1), F32), pltpu.VMEM((rows, 1), F32),
                        pltpu.VMEM((rows, E_A), F32)],
    )
    return pl.pallas_call(
        kern,
        grid_spec=grid_spec,
        out_shape=jax.ShapeDtypeStruct((n_seq, 1, w_a), BF16),
        compiler_params=_params(("parallel", "arbitrary"), 52),
        name="diff_attn_decode",
    )(page_table.reshape(-1), lam4, subln_g, q, k_new, v_new, bias, new_bias,
      *([cache_k] * pages), *([cache_v] * pages))


def _mix_out_kernel(x_ref, oa_ref, ob_ref, woa_ref, wob_ref, g_ref, b_ref, wq_ref,
                    x2_ref, qx_ref, *, alpha, qscale):
    y = _dot(oa_ref[...], woa_ref[...]) + _dot(ob_ref[...], wob_ref[...])
    x2 = _layer_norm(alpha * x_ref[...] + y, g_ref[...], b_ref[...])
    x2_ref[...] = x2
    qx_ref[...] = (_dot(x2.astype(BF16), wq_ref[...]) * qscale).astype(BF16)


def _mix_out(x, oa, ob, woa, wob, g, b, wq, *, alpha, qscale, tm):
    n, d = x.shape
    kern = functools.partial(_mix_out_kernel, alpha=alpha, qscale=qscale)
    row = lambda width: pl.BlockSpec((tm, width), lambda i: (i, 0))
    return pl.pallas_call(
        kern,
        grid=(n // tm,),
        in_specs=[row(d), row(oa.shape[1]), row(ob.shape[1]), _resident(woa.shape),
                  _resident(wob.shape), _resident(g.shape), _resident(b.shape), _resident(wq.shape)],
        out_specs=[row(d), row(d)],
        out_shape=[jax.ShapeDtypeStruct((n, d), F32), jax.ShapeDtypeStruct((n, d), BF16)],
        compiler_params=_params(("parallel",), 40),
        name="mix_out_ln_q",
    )(x, oa, ob, woa, wob, g, b, wq)


def _post_mix_kernel(x_ref, oa_ref, ob_ref, mk_ref, mv_ref, woa_ref, wob_ref, g1_ref, b1_ref,
                     wq_ref, wo_ref, g2_ref, b2_ref, y_ref, o_scr, x2_scr, xb_scr, q_scr,
                     *, alpha, qscale, n_heads):
    tm, d = x_ref.shape
    dh = d // n_heads
    subs = [slice(r0, r0 + SUB_ROWS) for r0 in range(0, tm, SUB_ROWS)]
    for rs in subs:
        y = _dot(oa_ref[rs, :], woa_ref[...]) + _dot(ob_ref[rs, :], wob_ref[...])
        x2 = _layer_norm(alpha * x_ref[rs, :] + y, g1_ref[...], b1_ref[...])
        x2_scr[rs, :] = x2
        xb_scr[rs, :] = x2.astype(BF16)
    for rs in subs:
        q_scr[rs, :] = (_dot(xb_scr[rs, :], wq_ref[...]) * qscale).astype(BF16)
    for h in range(n_heads):
        cols = slice(h * dh, (h + 1) * dh)
        s = _dot_nt(q_scr[:, cols], mk_ref[:, cols])
        p = jnp.exp2(s - jnp.max(s, axis=-1, keepdims=True))
        l = jnp.sum(p, axis=-1, keepdims=True)
        o_scr[:, cols] = (_dot(p.astype(BF16), mv_ref[:, cols]) / l).astype(BF16)
    for rs in subs:
        y2 = _dot(o_scr[rs, :], wo_ref[...])
        y_ref[rs, :] = _layer_norm(alpha * x2_scr[rs, :] + y2, g2_ref[...], b2_ref[...])


def _post_mix(x, oa, ob, mk, mv, woa, wob, g1, b1, wq, wo, g2, b2, *, alpha, qscale, n_heads, tm):
    bsz, s, d = x.shape
    n_mem = mk.shape[1]
    kern = functools.partial(_post_mix_kernel, alpha=alpha, qscale=qscale, n_heads=n_heads)
    row = lambda width: pl.BlockSpec((None, tm, width), lambda b, i: (b, i, 0))
    mem = pl.BlockSpec((None, n_mem, d), lambda b, i: (b, 0, 0))
    weights = [woa, wob, g1, b1, wq, wo, g2, b2]
    return pl.pallas_call(
        kern,
        grid=(bsz, s // tm),
        in_specs=[row(d), row(oa.shape[2]), row(ob.shape[2]), mem, mem]
        + [_resident(w.shape) for w in weights],
        out_specs=row(d),
        out_shape=jax.ShapeDtypeStruct((bsz, s, d), F32),
        scratch_shapes=[pltpu.VMEM((tm, d), BF16), pltpu.VMEM((tm, d), F32),
                        pltpu.VMEM((tm, d), BF16), pltpu.VMEM((tm, d), BF16)],
        compiler_params=_params(("parallel", "parallel"), 48),
        name="post_mix",
    )(x, oa, ob, mk, mv, *weights)


def _cross_decode_kernel(q_ref, mk_ref, mv_ref, o_ref, *, n_heads):
    n_seq, n_mem, _, dh = mk_ref.shape
    n_keys = n_mem * n_heads
    sublanes = SUBLANES
    assert n_heads <= sublanes
    r_id = lax.broadcasted_iota(jnp.int32, (sublanes, LANES), 0)
    for i in range(n_seq):
        q = q_ref[i].astype(F32)
        s = None
        for c in range(dh // LANES):
            qm = jnp.zeros((sublanes, LANES), F32)
            for h in range(n_heads):
                lo = h * dh + c * LANES
                piece = jnp.broadcast_to(q[:, lo:lo + LANES], (sublanes, LANES))
                qm = jnp.where((r_id % n_heads) == h, piece, qm)
            xk = mk_ref[i, :, :, c * LANES:(c + 1) * LANES].reshape(n_keys, LANES).astype(BF16)
            part = _dot_nt(qm.astype(BF16), xk)
            s = part if s is None else s + part
        row_s = lax.broadcasted_iota(jnp.int32, s.shape, 0)
        col_s = lax.broadcasted_iota(jnp.int32, s.shape, 1)
        s = jnp.where((col_s % n_heads) == (row_s % n_heads), s, NEG_INF)
        p = jnp.exp2(s - jnp.max(s, axis=-1, keepdims=True))
        l = jnp.sum(p, axis=-1, keepdims=True)
        pb = p.astype(BF16)
        for c in range(dh // LANES):
            xv = mv_ref[i, :, :, c * LANES:(c + 1) * LANES].reshape(n_keys, LANES).astype(BF16)
            o = _dot(pb, xv) / l
            for h in range(n_heads):
                lo = h * dh + c * LANES
                o_ref[i, :, lo:lo + LANES] = o[h:h + 1].astype(BF16)


def _cross_attention_decode(qx, mk, mv, *, layer, n_heads, seqs_per_step):
    n_seq, _, d = qx.shape
    n_mem, _, dh = mk.shape[2:]
    assert n_seq % seqs_per_step == 0
    kern = functools.partial(_cross_decode_kernel, n_heads=n_heads)
    tok_spec = pl.BlockSpec((seqs_per_step, 1, d), lambda b: (b, 0, 0))
    mem_spec = pl.BlockSpec((None, seqs_per_step, n_mem, n_heads, dh),
                            lambda b: (layer, b, 0, 0, 0))
    return pl.pallas_call(
        kern,
        grid=(n_seq // seqs_per_step,),
        in_specs=[tok_spec, mem_spec, mem_spec],
        out_specs=tok_spec,
        out_shape=jax.ShapeDtypeStruct((n_seq, 1, d), BF16),
        compiler_params=_params(("parallel",), 32),
        name="cross_attn_decode",
    )(qx, mk, mv)


def _proj_ln_kernel(x_ref, o_ref, w_ref, g_ref, b_ref, y_ref, *, alpha):
    y = _dot(o_ref[...], w_ref[...])
    y_ref[...] = _layer_norm(alpha * x_ref[...] + y, g_ref[...], b_ref[...])


def _proj_ln(x, o, w, g, b, *, alpha, tm):
    n, d = x.shape
    kern = functools.partial(_proj_ln_kernel, alpha=alpha)
    row = pl.BlockSpec((tm, d), lambda i: (i, 0))
    return pl.pallas_call(
        kern,
        grid=(n // tm,),
        in_specs=[row, row, _resident(w.shape), _resident(g.shape), _resident(b.shape)],
        out_specs=row,
        out_shape=jax.ShapeDtypeStruct((n, d), F32),
        compiler_params=_params(("parallel",), 32),
        name="proj_ln",
    )(x, o, w, g, b)


def _mem_kv_kernel(m_ref, w_ref, k_ref, v_ref, kb_ref, vb_ref):
    d = kb_ref.shape[1]
    n_heads, dh = k_ref.shape[1:]
    mb = m_ref[...].astype(BF16)
    k = _dot(mb, w_ref[:, 0:d])
    v = _dot(mb, w_ref[:, d:2 * d])
    kb_ref[...] = k.astype(BF16)
    vb_ref[...] = v.astype(BF16)
    k_ref[...] = k.reshape(k.shape[0], n_heads, dh)
    v_ref[...] = v.reshape(v.shape[0], n_heads, dh)


def _mem_kv(mem, w, *, n_heads, tm):
    n, d = mem.shape
    dh = d // n_heads
    row = pl.BlockSpec((tm, d), lambda i: (i, 0))
    row_h = pl.BlockSpec((tm, n_heads, dh), lambda i: (i, 0, 0))
    return pl.pallas_call(
        _mem_kv_kernel,
        grid=(n // tm,),
        in_specs=[row, _resident(w.shape)],
        out_specs=[row_h, row_h, row, row],
        out_shape=[jax.ShapeDtypeStruct((n, n_heads, dh), F32)] * 2
        + [jax.ShapeDtypeStruct((n, d), BF16)] * 2,
        compiler_params=_params(("parallel",), 32),
        name="mem_kv",
    )(mem, w)


def _shifted_bias(table, dist):
    n = jnp.maximum(dist, 0)
    max_exact = NB // 2
    nf = jnp.maximum(n, 1).astype(F32)
    large = max_exact + (jnp.log(nf / max_exact) / math.log(MAX_DIST / max_exact)
                         * (NB - max_exact)).astype(jnp.int32)
    large = jnp.minimum(large, NB - 1)
    bucket = jnp.where(n < max_exact, n, large)
    tab = (table.astype(F32) - table[NB - 1].astype(F32)) * LOG2E
    hit = bucket[..., None, None] == jnp.arange(NB)[:, None]
    return jnp.sum(jnp.where(hit, tab, 0.0), axis=-2)


def _place_blocks(sel, tile):
    nb, n = sel.shape[0], tile.shape[1]
    out = sel[None, :, None, :, None] * tile[:, None, :, None, :]
    return out.reshape(tile.shape[0], nb * n, nb * n)


def _toeplitz(f_pos, f_neg, t):
    period = 2 * t
    v = jnp.concatenate([f_neg, jnp.zeros_like(f_pos[:1]), f_pos[:0:-1]], axis=0)
    tiled = jnp.tile(v, (t, 1))[: t * (period - 1)]
    skew = tiled.reshape(t, period - 1, -1)
    return jnp.transpose(skew[:, :t], (2, 0, 1))


def kernel(x_prompt, x_sample, mem_prompt, cache_k, cache_v, cache_mem_k, cache_mem_v, page_table, rel_bias, ln_g, ln_b, ffn1_w_in, ffn1_w_out, w_mix_in, w_mix_out, lambda_q1, lambda_k1, lambda_q2, lambda_k2, subln_g, sgu_ln_g, sgu_ln_b, sgu_w, sgu_b, xq_w, xkv_w, xo_w, ffn2_w_in, ffn2_w_out):
    bsz, seq, d = x_prompt.shape
    n_dec = x_sample.shape[0]
    depth = ln_g.shape[0]
    assert depth == 1 and x_sample.shape[1] == 1
    w_b = sgu_ln_g.shape[1]
    w_a = (w_mix_in.shape[2] - 2 * w_b) // 3
    n_heads = w_a // E_A
    n_groups = sgu_w.shape[1]
    n_mem, h_m, dh_m = cache_mem_k.shape[2:]
    page_rows = cache_k.shape[2]
    alpha = (2 * depth) ** 0.25
    lambda_init = 0.8 - 0.6 * math.exp(-0.3 * 0)
    l = 0
    t_attn = ATTN_TILE
    tm = TOKEN_TILE
    assert seq % tm == 0 and seq % t_attn == 0 and t_attn % MAX_DIST == 0 and tm % SUB_ROWS == 0
    assert SUB_ROWS % CHUNK == 0
    assert page_rows == CHUNK and MAX_DIST <= page_rows

    row2 = lambda a: a.reshape(1, -1)
    g_ln = [row2(ln_g[l, i]) for i in range(4)]
    b_ln = [row2(ln_b[l, i]) for i in range(4)]
    ffn1 = (ffn1_w_in[l].astype(BF16), ffn1_w_out[l].astype(BF16))
    ffn2 = (ffn2_w_in[l].astype(BF16), ffn2_w_out[l].astype(BF16))
    q_fold = jnp.concatenate([jnp.full((w_a,), DH_A ** -0.5, F32),
                              jnp.ones((w_mix_in.shape[2] - w_a,), F32)])
    w_mix = (w_mix_in[l] * q_fold).astype(BF16)
    woa = w_mix_out[l, :w_a].astype(BF16)
    wob = w_mix_out[l, w_a:].astype(BF16)
    wq_x = xq_w[l].astype(BF16)
    wo_x = xo_w[l].astype(BF16)
    wkv = xkv_w[l].astype(BF16)
    lam4 = jnp.stack([lambda_q1[l], lambda_k1[l], lambda_q2[l], lambda_k2[l]]).astype(F32)
    g_sub = row2(subln_g[l])
    lng_s, lnb_s = row2(sgu_ln_g[l]), row2(sgu_ln_b[l])
    tril_w = jnp.tril(sgu_w[l]).astype(BF16)
    sgu_bias = sgu_b[l].reshape(n_groups, CHUNK, 1)
    cg = w_b // n_groups
    w00 = jnp.repeat(sgu_w[l, :, 0, 0], cg).reshape(1, w_b)
    b00 = jnp.repeat(sgu_b[l, :, 0], cg).reshape(1, w_b)
    xq_scale = dh_m ** -0.5 * LOG2E

    ar = jnp.arange(MAX_DIST)
    near = _toeplitz(_shifted_bias(rel_bias, ar), _shifted_bias(rel_bias, 0 * ar), MAX_DIST)
    edge = _toeplitz(_shifted_bias(rel_bias, MAX_DIST + ar), _shifted_bias(rel_bias, MAX_DIST - ar),
                     MAX_DIST)
    nb = t_attn // MAX_DIST
    bias0 = (_place_blocks(jnp.eye(nb, dtype=F32), near)
             + _place_blocks(jnp.eye(nb, k=-1, dtype=F32), edge))
    dec_near = _shifted_bias(rel_bias, page_rows - jnp.arange(page_rows))
    dec_bias = jnp.repeat(jnp.repeat(dec_near.T, 2, axis=0), n_heads, axis=1)
    new_bias = jnp.repeat(_shifted_bias(rel_bias, jnp.zeros((1,), jnp.int32)).T, 2, axis=0)

    n_tok = bsz * seq
    xp = x_prompt.reshape(n_tok, d)
    mk_p, mv_p, mkb, mvb = _mem_kv(mem_prompt.reshape(bsz * n_mem, d), wkv, n_heads=h_m, tm=n_mem)
    x1 = _ffn_ln(xp, *ffn1, g_ln[0], b_ln[0], alpha=alpha, tm=tm)
    qb, k_p, kb, v_p, vb, ob = _mix_in(x1, w_mix, lng_s, lnb_s, tril_w, sgu_bias,
                                       w_a=w_a, w_b=w_b, qscale=LOG2E, tm=tm)
    oa = _attention(lam4, g_sub, qb.reshape(bsz, seq, w_a), kb.reshape(bsz, seq, w_a),
                    vb.reshape(bsz, seq, w_a), bias0, edge, t=t_attn, lambda_init=lambda_init)
    x3 = _post_mix(x1.reshape(bsz, seq, d), oa, ob.reshape(bsz, seq, w_b),
                   mkb.reshape(bsz, n_mem, d), mvb.reshape(bsz, n_mem, d),
                   woa, wob, g_ln[1], b_ln[1], wq_x, wo_x, g_ln[2], b_ln[2],
                   alpha=alpha, qscale=xq_scale, n_heads=h_m, tm=tm)
    y_p = _ffn_ln(x3.reshape(n_tok, d), *ffn2, g_ln[3], b_ln[3], alpha=alpha, tm=tm)

    xs = x_sample.reshape(n_dec, d)
    s1 = _ffn_ln(xs, *ffn1, g_ln[0], b_ln[0], alpha=alpha, tm=n_dec)
    q_s, k_s, v_s, ob_s, vn_s = _mix_in_decode(s1, w_mix, lng_s, lnb_s, w00, b00,
                                               w_a=w_a, w_b=w_b, qscale=LOG2E)
    oa_s = _decode_attention(page_table, lam4, g_sub, q_s.reshape(n_dec, 1, w_a),
                             k_s.reshape(n_dec, 1, w_a), v_s.reshape(n_dec, 1, w_a), dec_bias, new_bias,
                             cache_k.reshape(depth, -1, page_rows * n_heads, E_A),
                             cache_v.reshape(depth, -1, page_rows * n_heads, E_A),
                             layer=l, pages=DECODE_PAGES, lambda_init=lambda_init)
    s2, qx_s = _mix_out(s1, oa_s.reshape(n_dec, w_a), ob_s, woa, wob, g_ln[1], b_ln[1], wq_x,
                        alpha=alpha, qscale=xq_scale, tm=n_dec)
    ox_s = _cross_attention_decode(qx_s.reshape(n_dec, 1, d),
                                   cache_mem_k, cache_mem_v,
                                   layer=l, n_heads=h_m, seqs_per_step=CROSS_DECODE_SEQS)
    s3 = _proj_ln(s2, ox_s.reshape(n_dec, d), wo_x, g_ln[2], b_ln[2], alpha=alpha, tm=n_dec)
    y_s = _ffn_ln(s3, *ffn2, g_ln[3], b_ln[3], alpha=alpha, tm=n_dec)

    return (y_p.reshape(bsz, seq, d), y_s.reshape(n_dec, 1, d),
            k_p.reshape(1, bsz, seq, n_heads, E_A), v_p.reshape(1, bsz, seq, n_heads, E_A),
            mk_p.reshape(1, bsz, n_mem, h_m, dh_m), mv_p.reshape(1, bsz, n_mem, h_m, dh_m),
            k_s.reshape(1, n_dec, 1, n_heads, E_A), v_s.reshape(1, n_dec, 1, n_heads, E_A),
            vn_s.reshape(1, n_dec, 1, w_b))
```
